```python
import math
import jax, jax.numpy as jnp
from jax import lax
import numpy as np

D_MODEL = 1024
BATCH = 4
SEQ = 4096
DEPTH = 2
DEC_BATCH = 128
DEC_SEQ = 4
PAST_LEN = 2048
PAGE_SIZE = 128

N_A_LAYERS = DEPTH // 2
N_B_LAYERS = DEPTH - N_A_LAYERS
A_HEADS = 4
A_INNER = 2 * D_MODEL
A_HEAD_DIM = A_INNER // A_HEADS
A_CONV = 4
A_CHUNK = 64
B_HEADS = D_MODEL // 128
B_DK = 64
B_DV = 2 * B_DK
B_QBLOCK = 128
RPB_BUCKETS = 32
RPB_MAX_DIST = 128
N_EXPERTS = 16
N_GROUPS = 4
TOP_K = 2
D_EXPERT = D_MODEL // 2
EPS = 1e-6

kernel_name = 'yoco_mlstm_diffattn_groupmoe_step'


def rms(x, g):
    xf = x.astype(jnp.float32)
    y = xf * lax.rsqrt(jnp.mean(xf * xf, axis=-1, keepdims=True) + EPS)
    return (y * g.astype(jnp.float32)).astype(x.dtype)


def ada(c, w, b):
    return (jax.nn.silu(c) @ w + b)[:, None, :]


def modulate(x, g, shift, scale):
    return rms(x, g) * (1 + scale) + shift


def mlstm_chunk(carry, inp):
    C0, n0, m0 = carry
    q, k, v, ig, lf = inp
    L = q.shape[1]
    b = jnp.cumsum(lf, axis=1).transpose(0, 2, 1)
    igt = ig.transpose(0, 2, 1)
    causal = jnp.tril(jnp.ones((L, L), dtype=bool))
    dlog = jnp.where(causal, b[..., :, None] - b[..., None, :] + igt[..., None, :], -jnp.inf)
    g = b + m0[..., None]
    m = jnp.maximum(g, jnp.max(dlog, axis=-1))
    w_intra = jnp.exp(dlog - m[..., None])
    w_inter = jnp.exp(g - m)
    s = w_intra * jnp.einsum('bthd,bshd->bhts', q, k)
    num = w_inter[..., None] * jnp.einsum('bhvd,bthd->bhtv', C0, q) + jnp.einsum('bhts,bshv->bhtv', s, v)
    den = w_inter * jnp.einsum('bhd,bthd->bht', n0, q) + jnp.sum(s, axis=-1)
    h = num / jnp.maximum(jnp.abs(den), jnp.exp(-m))[..., None]
    m_end = m[..., -1]
    we_inter = jnp.exp(g[..., -1] - m_end)
    we_intra = jnp.exp(dlog[..., -1, :] - m_end[..., None])
    kw = k * we_intra.transpose(0, 2, 1)[..., None]
    C1 = we_inter[..., None, None] * C0 + jnp.einsum('bshv,bshd->bhvd', v, kw)
    n1 = we_inter[..., None] * n0 + jnp.sum(kw, axis=1)
    return (C1, n1, m_end), h.transpose(0, 2, 1, 3)


def mlstm_mixer(h, w_in, b_gate, w_conv, b_conv, g_hn, w_out, conv_buf, C0, n0, m0):
    Bsz, T, _ = h.shape
    f32 = jnp.float32
    proj = h @ w_in
    qk_pre, v, o, gates = jnp.split(proj, [2 * A_INNER, 3 * A_INNER, 4 * A_INNER], axis=-1)
    if conv_buf is None:
        conv_buf = jnp.zeros((Bsz, A_CONV - 1, 2 * A_INNER), qk_pre.dtype)
    xp = jnp.concatenate([conv_buf.astype(qk_pre.dtype), qk_pre], axis=1)
    conv = b_conv
    for j in range(A_CONV):
        conv = conv + xp[:, j:j + T] * w_conv[j]
    new_buf = xp[:, T:]
    q, k = jnp.split(jax.nn.silu(conv).astype(f32), 2, axis=-1)
    q = q.reshape(Bsz, T, A_HEADS, A_HEAD_DIM)
    k = k.reshape(Bsz, T, A_HEADS, A_HEAD_DIM) * (A_HEAD_DIM ** -0.5)
    v = v.astype(f32).reshape(Bsz, T, A_HEADS, A_HEAD_DIM)
    gz = gates.astype(f32) + b_gate.astype(f32)
    ig = gz[..., :A_HEADS]
    lf = jax.nn.log_sigmoid(gz[..., A_HEADS:])
    if C0 is None:
        C0 = jnp.zeros((Bsz, A_HEADS, A_HEAD_DIM, A_HEAD_DIM), f32)
        n0 = jnp.zeros((Bsz, A_HEADS, A_HEAD_DIM), f32)
        m0 = jnp.zeros((Bsz, A_HEADS), f32)
    else:
        C0, n0, m0 = C0.astype(f32), n0.astype(f32), m0.astype(f32)
    L = math.gcd(T, A_CHUNK)
    nC = T // L

    def chunks(a):
        return a.reshape(Bsz, nC, L, *a.shape[2:]).swapaxes(0, 1)

    (C1, n1, m1), hs = lax.scan(mlstm_chunk, (C0, n0, m0),
                                (chunks(q), chunks(k), chunks(v), chunks(ig), chunks(lf)))
    hs = hs.swapaxes(0, 1).reshape(Bsz, T, A_HEADS, A_HEAD_DIM)
    hs = rms(hs, g_hn).astype(h.dtype)
    out = (jax.nn.sigmoid(o) * hs.reshape(Bsz, T, A_INNER)) @ w_out
    return out, new_buf, C1, n1, m1


def shared_kv(x, c, g_kv, w_ada_kv, b_ada_kv, w_kv, g_kn):
    Bsz, T, _ = x.shape
    shift, scale = jnp.split(ada(c, w_ada_kv, b_ada_kv), 2, axis=-1)
    hk = modulate(x, g_kv, shift, scale)
    kv = hk @ w_kv
    k, v = jnp.split(kv, [B_HEADS * 2 * B_DK], axis=-1)
    k = rms(k.reshape(Bsz, T, B_HEADS, 2, B_DK), g_kn)
    v = v.reshape(Bsz, T, B_HEADS, B_DV)
    return k, v


def rel_bucket(q_pos, k_pos):
    n = jnp.maximum(q_pos[:, None] - k_pos[None, :], 0)
    max_exact = RPB_BUCKETS // 2
    nf = jnp.maximum(n, 1).astype(jnp.float32)
    large = max_exact + (jnp.log(nf / max_exact) / math.log(RPB_MAX_DIST / max_exact)
                         * (RPB_BUCKETS - max_exact)).astype(jnp.int32)
    large = jnp.minimum(large, RPB_BUCKETS - 1)
    return jnp.where(n < max_exact, n, large)


def diff_attn_block(qb, qpos, k, v, kpos, lam, rpb):
    s = jnp.einsum('bqhcd,bkhcd->bchqk', qb, k).astype(jnp.float32) * (B_DK ** -0.5)
    bias = rpb[rel_bucket(qpos, kpos)].transpose(2, 0, 1).astype(jnp.float32)
    mask = kpos[None, :] <= qpos[:, None]
    s = jnp.where(mask, s + bias, -jnp.inf)
    p = jax.nn.softmax(s, axis=-1)
    a = p[:, 0] - lam * p[:, 1]
    return jnp.einsum('bhqk,bkhd->bqhd', a.astype(v.dtype), v)


def diff_attn_mixer(h, k, v, kpos, q_pos0, w_q, g_qn, lam_vec, g_hn, w_o, rpb, lam_init):
    Bsz, T, _ = h.shape
    q = rms((h @ w_q).reshape(Bsz, T, B_HEADS, 2, B_DK), g_qn)
    lv = lam_vec.astype(jnp.float32)
    lam = jnp.exp(jnp.sum(lv[0] * lv[1])) - jnp.exp(jnp.sum(lv[2] * lv[3])) + lam_init
    qpos = q_pos0 + jnp.arange(T, dtype=jnp.int32)
    if T > B_QBLOCK and T % B_QBLOCK == 0:
        nb = T // B_QBLOCK
        qs = q.reshape(Bsz, nb, B_QBLOCK, B_HEADS, 2, B_DK).swapaxes(0, 1)
        ps = qpos.reshape(nb, B_QBLOCK)
        o = lax.map(lambda blk: diff_attn_block(blk[0], blk[1], k, v, kpos, lam, rpb), (qs, ps))
        o = o.swapaxes(0, 1).reshape(Bsz, T, B_HEADS, B_DV)
    else:
        o = diff_attn_block(q, qpos, k, v, kpos, lam, rpb)
    o = rms(o, g_hn) * (1.0 - lam_init)
    return o.reshape(Bsz, T, B_HEADS * B_DV) @ w_o


def moe(h, w_router, b_router, w_g, w_u, w_d):
    Bsz, T, D = h.shape
    hf = h.reshape(-1, D)
    s = jax.nn.sigmoid((hf @ w_router).astype(jnp.float32))
    sel = s + b_router.astype(jnp.float32)
    per = N_EXPERTS // N_GROUPS
    gscore = jnp.sum(lax.top_k(sel.reshape(-1, N_GROUPS, per), TOP_K)[0], axis=-1)
    gidx = jnp.argmax(gscore, axis=-1)
    in_grp = (jnp.arange(N_EXPERTS) // per)[None, :] == gidx[:, None]
    _, eidx = lax.top_k(jnp.where(in_grp, sel, -jnp.inf), TOP_K)
    w = jnp.take_along_axis(s, eidx, axis=-1)
    w = w / jnp.sum(w, axis=-1, keepdims=True)
    gates = jnp.sum(jax.nn.one_hot(eidx, N_EXPERTS, dtype=jnp.float32) * w[..., None], axis=1)
    a = jnp.einsum('nd,edf->nef', hf, w_g)
    u = jnp.einsum('nd,edf->nef', hf, w_u)
    act = jax.nn.silu(a) * u * gates[:, :, None].astype(h.dtype)
    return jnp.einsum('nef,efd->nd', act, w_d).reshape(Bsz, T, D)


def trunk(x, c, conv_st, C_st, n_st, m_st, past_k, past_v,
          w_ada, b_ada, g_norm, w_in_a, b_gate_a, w_conv_a, b_conv_a, g_hn_a, w_out_a,
          g_kv, w_ada_kv, b_ada_kv, w_kv, g_kn, w_q_b, g_qn_b, lam_b, g_hn_b, w_o_b, rpb,
          w_router, b_router, w_gate_e, w_up_e, w_down_e):
    Bsz, T, _ = x.shape
    P = 0 if past_k is None else past_k.shape[1]
    new_conv, new_C, new_n, new_m = [], [], [], []
    k_sh = v_sh = k_new = v_new = kpos = None
    for l in range(DEPTH):
        shift, scale, gate = jnp.split(ada(c, w_ada[l, 0], b_ada[l, 0]), 3, axis=-1)
        h = modulate(x, g_norm[l, 0], shift, scale)
        if l < N_A_LAYERS:
            if conv_st is None:
                st = (None, None, None, None)
            else:
                st = (conv_st[l], C_st[l], n_st[l], m_st[l])
            mix, cb, C1, n1, m1 = mlstm_mixer(h, w_in_a[l], b_gate_a[l], w_conv_a[l], b_conv_a[l],
                                              g_hn_a[l], w_out_a[l], *st)
            new_conv.append(cb.astype(x.dtype))
            new_C.append(C1.astype(x.dtype))
            new_n.append(n1.astype(x.dtype))
            new_m.append(m1.astype(x.dtype))
        else:
            j = l - N_A_LAYERS
            lam_init = 0.8 - 0.6 * math.exp(-0.3 * l)
            mix = diff_attn_mixer(h, k_sh, v_sh, kpos, P, w_q_b[j], g_qn_b[j], lam_b[j],
                                  g_hn_b[j], w_o_b[j], rpb, lam_init)
        x = x + gate * mix
        shift, scale, gate = jnp.split(ada(c, w_ada[l, 1], b_ada[l, 1]), 3, axis=-1)
        h = modulate(x, g_norm[l, 1], shift, scale)
        x = x + gate * moe(h, w_router, b_router, w_gate_e[l], w_up_e[l], w_down_e[l])
        if l == N_A_LAYERS - 1:
            k_new, v_new = shared_kv(x, c, g_kv, w_ada_kv, b_ada_kv, w_kv, g_kn)
            if past_k is None:
                k_sh, v_sh = k_new, v_new
            else:
                k_sh = jnp.concatenate([past_k.astype(k_new.dtype), k_new], axis=1)
                v_sh = jnp.concatenate([past_v.astype(v_new.dtype), v_new], axis=1)
            kpos = jnp.arange(P + T, dtype=jnp.int32)
    return x, jnp.stack(new_conv), jnp.stack(new_C), jnp.stack(new_n), jnp.stack(new_m), k_new, v_new


def setup_inputs(seed: int = 0) -> dict:
    key = jax.random.key(seed)
    ks = iter(jax.random.split(key, 48))
    f32 = jnp.float32

    def nrm(shape, scale=1.0):
        return jax.random.normal(next(ks), shape, f32) * scale

    D = D_MODEL
    n_pages = PAST_LEN // PAGE_SIZE
    n_used = DEC_BATCH * n_pages
    n_pool = (5 * n_used) // 4
    page_table = jax.random.permutation(next(ks), n_pool)[:n_used].reshape(DEC_BATCH, n_pages).astype(jnp.int32)
    b_gate_a = jnp.concatenate([nrm((N_A_LAYERS, A_HEADS), 0.1),
                                jnp.linspace(3.0, 6.0, A_HEADS, dtype=f32)[None, :] + nrm((N_A_LAYERS, A_HEADS), 0.1)], axis=-1)
    return {
        'x_prompt': nrm((BATCH, SEQ, D)),
        'x_sample': nrm((DEC_BATCH, DEC_SEQ, D)),
        'c_prompt': nrm((BATCH, D)),
        'c_sample': nrm((DEC_BATCH, D)),
        'state_conv': nrm((N_A_LAYERS, DEC_BATCH, A_CONV - 1, 2 * A_INNER)),
        'state_C': nrm((N_A_LAYERS, DEC_BATCH, A_HEADS, A_HEAD_DIM, A_HEAD_DIM), 0.05),
        'state_n': nrm((N_A_LAYERS, DEC_BATCH, A_HEADS, A_HEAD_DIM), 0.1),
        'state_m': nrm((N_A_LAYERS, DEC_BATCH, A_HEADS), 0.5) + 1.0,
        'cache_k': nrm((n_pool, PAGE_SIZE, B_HEADS, 2, B_DK)),
        'cache_v': nrm((n_pool, PAGE_SIZE, B_HEADS, B_DV)),
        'page_table': page_table,
        'w_ada': nrm((DEPTH, 2, D, 3 * D), 0.5 * D ** -0.5),
        'b_ada': nrm((DEPTH, 2, 3 * D), 0.02),
        'g_norm': 1.0 + nrm((DEPTH, 2, D), 0.02),
        'w_in_a': nrm((N_A_LAYERS, D, 4 * A_INNER + 2 * A_HEADS), D ** -0.5),
        'b_gate_a': b_gate_a,
        'w_conv_a': nrm((N_A_LAYERS, A_CONV, 2 * A_INNER), A_CONV ** -0.5),
        'b_conv_a': nrm((N_A_LAYERS, 2 * A_INNER), 0.02),
        'g_hn_a': 1.0 + nrm((N_A_LAYERS, A_HEADS, A_HEAD_DIM), 0.02),
        'w_out_a': nrm((N_A_LAYERS, A_INNER, D), A_INNER ** -0.5),
        'g_kv': 1.0 + nrm((D,), 0.02),
        'w_ada_kv': nrm((D, 2 * D), 0.5 * D ** -0.5),
        'b_ada_kv': nrm((2 * D,), 0.02),
        'w_kv': nrm((D, B_HEADS * (2 * B_DK + B_DV)), D ** -0.5),
        'g_kn': 1.0 + nrm((2, B_DK), 0.02),
        'w_q_b': nrm((N_B_LAYERS, D, B_HEADS * 2 * B_DK), D ** -0.5),
        'g_qn_b': 1.0 + nrm((N_B_LAYERS, 2, B_DK), 0.02),
        'lam_b': nrm((N_B_LAYERS, 4, B_DK), 0.1),
        'g_hn_b': 1.0 + nrm((N_B_LAYERS, B_HEADS, B_DV), 0.02),
        'w_o_b': nrm((N_B_LAYERS, B_HEADS * B_DV, D), (B_HEADS * B_DV) ** -0.5),
        'rpb': nrm((RPB_BUCKETS, B_HEADS), 0.5),
        'w_router': nrm((D, N_EXPERTS), D ** -0.5),
        'b_router': nrm((N_EXPERTS,), 0.01),
        'w_gate_e': nrm((DEPTH, N_EXPERTS, D, D_EXPERT), D ** -0.5),
        'w_up_e': nrm((DEPTH, N_EXPERTS, D, D_EXPERT), D ** -0.5),
        'w_down_e': nrm((DEPTH, N_EXPERTS, D_EXPERT, D), D_EXPERT ** -0.5),
    }


def reference(x_prompt, x_sample, c_prompt, c_sample, state_conv, state_C, state_n, state_m,
              cache_k, cache_v, page_table,
              w_ada, b_ada, g_norm, w_in_a, b_gate_a, w_conv_a, b_conv_a, g_hn_a, w_out_a,
              g_kv, w_ada_kv, b_ada_kv, w_kv, g_kn, w_q_b, g_qn_b, lam_b, g_hn_b, w_o_b, rpb,
              w_router, b_router, w_gate_e, w_up_e, w_down_e):
    weights = (w_ada, b_ada, g_norm, w_in_a, b_gate_a, w_conv_a, b_conv_a, g_hn_a, w_out_a,
               g_kv, w_ada_kv, b_ada_kv, w_kv, g_kn, w_q_b, g_qn_b, lam_b, g_hn_b, w_o_b, rpb,
               w_router, b_router, w_gate_e, w_up_e, w_down_e)
    y_p, conv_p, C_p, n_p, m_p, k_p, v_p = trunk(x_prompt, c_prompt, None, None, None, None, None, None, *weights)
    nb, npg = page_table.shape
    page = cache_k.shape[1]
    past_k = cache_k[page_table].reshape(nb, npg * page, *cache_k.shape[2:])
    past_v = cache_v[page_table].reshape(nb, npg * page, *cache_v.shape[2:])
    y_s, conv_s, C_s, n_s, m_s, k_s, v_s = trunk(x_sample, c_sample, state_conv, state_C, state_n, state_m,
                                                 past_k, past_v, *weights)
    return (y_p, y_s, conv_p, C_p, n_p, m_p, k_p, v_p, conv_s, C_s, n_s, m_s, k_s, v_s)
```

```python
import functools
import math

import numpy as np
import jax
import jax.numpy as jnp
from jax import lax
from jax.experimental import pallas as pl
from jax.experimental.pallas import tpu as pltpu

F32, BF16 = jnp.float32, jnp.bfloat16
HIGHEST = lax.Precision.HIGHEST
EPS = 1e-6

A_HEADS = 4
A_CONV = 4
B_DK = 64
B_DV = 128
N_EXPERTS = 16
N_GROUPS = 4
RPB_BUCKETS = 32
RPB_MAX_DIST = 128

LANES = 128
SUBLANES = 8
VMEM_LIMIT_BYTES = 56 * 1024 * 1024

ROW_TILE = 512
MLSTM_CHUNK = 256
ATTN_TQ = 512
ATTN_TK = 512


def _params(*sem):
    return pltpu.CompilerParams(dimension_semantics=sem, vmem_limit_bytes=VMEM_LIMIT_BYTES)


def _nt_dot(a, b):
    return lax.dot_general(a, b, (((1,), (1,)), ((), ())), preferred_element_type=F32)


def _tn_dot(a, b):
    return lax.dot_general(a, b, (((0,), (0,)), ((), ())), preferred_element_type=F32)


def _silu(x):
    return x * jax.nn.sigmoid(x)


def _rms_mod(x, g, scale, shift):
    y = x * lax.rsqrt(jnp.mean(x * x, axis=-1, keepdims=True) + EPS)
    return (y * g) * (1.0 + scale) + shift


def _group_rms(x, gmat, gmat_t, g, group):
    ss = jnp.dot(x * x, gmat, precision=HIGHEST, preferred_element_type=F32)
    r = lax.rsqrt(ss * (1.0 / group) + EPS)
    rf = jnp.dot(r, gmat_t, precision=HIGHEST, preferred_element_type=F32)
    return x * rf * g


def _ada_kernel(c_ref, w_ref, b_ref, o_ref):
    a = _silu(c_ref[...])
    o_ref[0] = jnp.dot(a, w_ref[0], precision=HIGHEST, preferred_element_type=F32) + b_ref[0]


def _ada(c_all, w, b):
    S, D, Fo = w.shape
    R = c_all.shape[0]
    tn = 1024
    return pl.pallas_call(
        _ada_kernel,
        grid=(S, Fo // tn),
        in_specs=[pl.BlockSpec((R, D), lambda s, j: (0, 0)),
                  pl.BlockSpec((1, D, tn), lambda s, j: (s, 0, j)),
                  pl.BlockSpec((1, 1, tn), lambda s, j: (s, 0, j))],
        out_specs=pl.BlockSpec((1, R, tn), lambda s, j: (s, 0, j)),
        out_shape=jax.ShapeDtypeStruct((S, R, Fo), F32),
        compiler_params=_params("parallel", "parallel"),
        name="ada",
    )(c_all, w, b)


class _Mod:
    def __init__(self, m, T, tm):
        B, D = m.shape
        if T % tm == 0:
            self.arr, self.tiles_per_group = m[:, None, :], T // tm
        else:
            assert (B * T) % tm == 0
            self.arr, self.tiles_per_group = jnp.repeat(m, T, axis=0).reshape(-1, tm, D), 1

    def spec(self, grid_rank):
        R, D = self.arr.shape[1:]
        tpg = self.tiles_per_group
        if grid_rank == 1:
            return pl.BlockSpec((1, R, D), lambda i: (i // tpg, 0, 0))
        return pl.BlockSpec((1, R, D), lambda i, j: (i // tpg, 0, 0))


def _inproj_kernel(x_ref, sh_ref, sc_ref, g_ref, w_ref, wg_ref, o_ref, og_ref, h_sc):
    @pl.when(pl.program_id(1) == 0)
    def _():
        h = _rms_mod(x_ref[...], g_ref[...], sc_ref[0], sh_ref[0])
        h_sc[...] = h.astype(BF16)
        og_ref[...] = jnp.dot(h, wg_ref[...], precision=HIGHEST, preferred_element_type=F32)

    o_ref[...] = jnp.dot(h_sc[...], w_ref[...], preferred_element_type=F32)


def _inproj(x2, shift, scale, g, w16, wgate, tm):
    N, D = x2.shape
    Fo = w16.shape[1]
    tn = 2048
    return pl.pallas_call(
        _inproj_kernel,
        grid=(N // tm, Fo // tn),
        in_specs=[pl.BlockSpec((tm, D), lambda i, j: (i, 0)),
                  shift.spec(2), scale.spec(2),
                  pl.BlockSpec((1, D), lambda i, j: (0, 0)),
                  pl.BlockSpec((D, tn), lambda i, j: (0, j)),
                  pl.BlockSpec((D, LANES), lambda i, j: (0, 0))],
        out_specs=[pl.BlockSpec((tm, tn), lambda i, j: (i, j)),
                   pl.BlockSpec((tm, LANES), lambda i, j: (i, 0))],
        out_shape=[jax.ShapeDtypeStruct((N, Fo), F32), jax.ShapeDtypeStruct((N, LANES), F32)],
        scratch_shapes=[pltpu.VMEM((tm, D), BF16)],
        compiler_params=_params("parallel", "arbitrary"),
        name="mlstm_inproj",
    )(x2, shift.arr, scale.arr, g, w16, wgate)


def _mlstm_kernel(q_ref, k_ref, v_ref, o_ref, gt_ref, gtt_ref, bg_ref, bgt_ref, cinit_ref,
                  c0_ref, n0_ref, m0_ref, wconv_ref, bconv_ref, ghn_ref,
                  hs_ref, c_ref, n_ref, m_ref, tail_sc, *, L, dh, heads, t_valid):
    inner = heads * dh

    @pl.when(pl.program_id(1) == 0)
    def _():
        c_ref[...] = c0_ref[...]
        n_ref[...] = n0_ref[...]
        m_ref[...] = m0_ref[...]
        tail_sc[...] = cinit_ref[0]

    row8 = lax.broadcasted_iota(jnp.int32, (SUBLANES, dh), 0)

    def conv(x, tail, w, b):
        acc = b + x * w[A_CONV - 1:A_CONV]
        for s in range(1, A_CONV):
            xs = pltpu.roll(x, s, 0)
            top = jnp.where(row8 < s, pltpu.roll(tail, s, 0), xs[:SUBLANES])
            xs = top if L == SUBLANES else jnp.concatenate([top, xs[SUBLANES:]], axis=0)
            acc = acc + xs * w[A_CONV - 1 - s:A_CONV - s]
        return acc

    gt = gt_ref[0] + bg_ref[...]
    gtt = gtt_ref[0] + bgt_ref[...]
    ti = lax.broadcasted_iota(jnp.int32, (L, L), 0)
    si = lax.broadcasted_iota(jnp.int32, (L, L), 1)
    causal = si <= ti
    tcol = lax.broadcasted_iota(jnp.int32, (L, 1), 0)
    trow = lax.broadcasted_iota(jnp.int32, (1, L), 1)

    for h in range(heads):
        sl = slice(h * dh, (h + 1) * dh)
        slk = slice(inner + h * dh, inner + (h + 1) * dh)
        qh = _silu(conv(q_ref[0, :, sl], tail_sc[:, sl], wconv_ref[:, sl], bconv_ref[:, sl]))
        kh = _silu(conv(k_ref[0, :, sl], tail_sc[:, slk], wconv_ref[:, slk], bconv_ref[:, slk])) * (dh ** -0.5)
        vb = v_ref[0, :, sl].astype(BF16)

        ig_col = gt[:, h:h + 1]
        lf_col = jax.nn.log_sigmoid(gt[:, heads + h:heads + h + 1])
        ig_row = gtt[h:h + 1, :]
        lf_row = jax.nn.log_sigmoid(gtt[heads + h:heads + h + 1, :])
        if t_valid is not None:
            ig_col = jnp.where(tcol < t_valid, ig_col, -jnp.inf)
            lf_col = jnp.where(tcol < t_valid, lf_col, 0.0)
            ig_row = jnp.where(trow < t_valid, ig_row, -jnp.inf)
            lf_row = jnp.where(trow < t_valid, lf_row, 0.0)

        b_col = jnp.sum(jnp.where(causal, lf_row, 0.0), axis=1, keepdims=True)
        b_row = jnp.sum(jnp.where(ti <= si, lf_col, 0.0), axis=0, keepdims=True)
        dlog = jnp.where(causal, b_col - b_row + ig_row, -jnp.inf)
        g_col = b_col + m_ref[0, h]
        m_col = jnp.maximum(g_col, jnp.max(dlog, axis=1, keepdims=True))
        w_intra = jnp.exp(dlog - m_col)
        w_inter = jnp.exp(g_col - m_col)

        qb = qh.astype(BF16)
        kb = kh.astype(BF16)
        s = w_intra * _nt_dot(qb, kb)
        ch = c_ref[0, h]
        nh = n_ref[0, h]
        num = w_inter * _nt_dot(qb, ch.astype(BF16)) + jnp.dot(s.astype(BF16), vb, preferred_element_type=F32)
        den = w_inter * jnp.sum(qh * nh, axis=1, keepdims=True) + jnp.sum(s, axis=1, keepdims=True)
        hv = num / jnp.maximum(jnp.abs(den), jnp.exp(-m_col))

        m_end = m_col[L - 1:L]
        we_inter = jnp.exp(g_col[L - 1:L] - m_end)
        we_col = jnp.exp(b_col[L - 1:L] - b_col + ig_col - m_end)
        kw = kh * we_col
        c_ref[0, h] = we_inter * ch + _tn_dot(vb, kw.astype(BF16))
        n_ref[0, h] = we_inter * nh + jnp.sum(kw, axis=0, keepdims=True)
        m_ref[0, h] = m_end

        hn = hv * lax.rsqrt(jnp.mean(hv * hv, axis=1, keepdims=True) + EPS) * ghn_ref[:, sl]
        hs_ref[0, :, sl] = (jax.nn.sigmoid(o_ref[0, :, sl]) * hn).astype(hs_ref.dtype)

    tail_sc[:, :inner] = q_ref[0, L - SUBLANES:, :]
    tail_sc[:, inner:] = k_ref[0, L - SUBLANES:, :]


def _mlstm(proj, gates, gates_t, bg, bgt, conv_init, c0, n0, m0, wconv, bconv, ghn, L, t_valid):
    B, Tp, _ = proj.shape
    heads, dh = c0.shape[1], c0.shape[2]
    inner = heads * dh
    nc = Tp // L
    kern = functools.partial(_mlstm_kernel, L=L, dh=dh, heads=heads, t_valid=t_valid)
    col = lambda j: pl.BlockSpec((1, L, inner), lambda b, c: (b, c, j))
    full = lambda shape: pl.BlockSpec(shape, lambda b, c: (0,) * len(shape))
    per_b = lambda shape: pl.BlockSpec((1,) + shape, lambda b, c: (b,) + (0,) * len(shape))
    return pl.pallas_call(
        kern,
        grid=(B, nc),
        in_specs=[col(0), col(1), col(2), col(3),
                  pl.BlockSpec((1, L, LANES), lambda b, c: (b, c, 0)),
                  pl.BlockSpec((1, SUBLANES, L), lambda b, c: (b, 0, c)),
                  full((1, LANES)), full((SUBLANES, 1)),
                  per_b((SUBLANES, 2 * inner)),
                  per_b((heads, dh, dh)), per_b((heads, 1, dh)), per_b((heads, 1, 1)),
                  full((A_CONV, 2 * inner)), full((1, 2 * inner)), full((1, inner))],
        out_specs=[pl.BlockSpec((1, L, inner), lambda b, c: (b, c, 0)),
                   per_b((heads, dh, dh)), per_b((heads, 1, dh)), per_b((heads, 1, 1))],
        out_shape=[jax.ShapeDtypeStruct((B, Tp, inner), BF16),
                   jax.ShapeDtypeStruct((B, heads, dh, dh), F32),
                   jax.ShapeDtypeStruct((B, heads, 1, dh), F32),
                   jax.ShapeDtypeStruct((B, heads, 1, 1), F32)],
        scratch_shapes=[pltpu.VMEM((SUBLANES, 2 * inner), F32)],
        compiler_params=_params("parallel", "arbitrary"),
        name="mlstm",
    )(proj, proj, proj, proj, gates, gates_t, bg, bgt, conv_init, c0, n0, m0, wconv, bconv, ghn)


def _route(h, wr, br):
    tm = h.shape[0]
    per = N_EXPERTS // N_GROUPS
    logits = jnp.dot(h, wr, precision=HIGHEST, preferred_element_type=F32)
    lt = logits.T[:N_EXPERTS]
    s = jax.nn.sigmoid(lt)
    sel = s + br
    neg = jnp.full((1, tm), -jnp.inf, F32)
    izero = jnp.zeros((1, tm), jnp.int32)

    best_score = best_e1 = best_e2 = best_w1 = best_w2 = None
    for grp in range(N_GROUPS):
        rows = [sel[grp * per + j:grp * per + j + 1] for j in range(per)]
        srow = [s[grp * per + j:grp * per + j + 1] for j in range(per)]
        t1, i1, w1 = rows[0], izero, srow[0]
        for j in range(1, per):
            better = rows[j] > t1
            t1 = jnp.where(better, rows[j], t1)
            i1 = jnp.where(better, j, i1)
            w1 = jnp.where(better, srow[j], w1)
        t2, i2, w2 = neg, izero, srow[0]
        for j in range(per):
            better = jnp.where(i1 == j, neg, rows[j]) > t2
            t2 = jnp.where(better, rows[j], t2)
            i2 = jnp.where(better, j, i2)
            w2 = jnp.where(better, srow[j], w2)
        score = t1 + t2
        e1, e2 = i1 + grp * per, i2 + grp * per
        if grp == 0:
            best_score, best_e1, best_e2, best_w1, best_w2 = score, e1, e2, w1, w2
        else:
            better = score > best_score
            best_score = jnp.where(better, score, best_score)
            best_e1 = jnp.where(better, e1, best_e1)
            best_e2 = jnp.where(better, e2, best_e2)
            best_w1 = jnp.where(better, w1, best_w1)
            best_w2 = jnp.where(better, w2, best_w2)
    tot = best_w1 + best_w2
    eid = lax.broadcasted_iota(jnp.int32, (LANES, tm), 0)
    gt = jnp.where(eid == best_e1, best_w1 / tot, 0.0) + jnp.where(eid == best_e2, best_w2 / tot, 0.0)
    return gt.T


def _proj_router_kernel(a_ref, w_ref, x_ref, gate_ref, sh_ref, sc_ref, g_ref, wr_ref, br_ref,
                        xo_ref, h_ref, gates_ref):
    mix = jnp.dot(a_ref[...], w_ref[...], preferred_element_type=F32)
    x = x_ref[...] + gate_ref[0] * mix
    xo_ref[...] = x
    h = _rms_mod(x, g_ref[...], sc_ref[0], sh_ref[0])
    h_ref[...] = h.astype(BF16)
    gates_ref[...] = _route(h, wr_ref[...], br_ref[...])


def _proj_router(a16, w16, x2, gate, shift, scale, g, wr, br, tm):
    N, D = x2.shape
    K = a16.shape[1]
    row = lambda w: pl.BlockSpec((tm, w), lambda i: (i, 0))
    full = lambda shape: pl.BlockSpec(shape, lambda i: (0,) * len(shape))
    return pl.pallas_call(
        _proj_router_kernel,
        grid=(N // tm,),
        in_specs=[row(K), full((K, D)), row(D), gate.spec(1), shift.spec(1), scale.spec(1),
                  full((1, D)), full((D, LANES)), full((N_EXPERTS, 1))],
        out_specs=[row(D), row(D), row(LANES)],
        out_shape=[jax.ShapeDtypeStruct((N, D), F32), jax.ShapeDtypeStruct((N, D), BF16),
                   jax.ShapeDtypeStruct((N, LANES), F32)],
        compiler_params=_params("parallel"),
        name="proj_router",
    )(a16, w16, x2, gate.arr, shift.arr, scale.arr, g, wr, br)


def _moe_kernel(h_ref, gates_ref, x_ref, gate_ref, wg_ref, wu_ref, wd_ref, o_ref, acc_sc):
    e = pl.program_id(1)

    @pl.when(e == 0)
    def _():
        acc_sc[...] = jnp.zeros_like(acc_sc)

    h = h_ref[...]
    a = jnp.dot(h, wg_ref[0], preferred_element_type=F32)
    u = jnp.dot(h, wu_ref[0], preferred_element_type=F32)
    gates = gates_ref[...]
    lane = lax.broadcasted_iota(jnp.int32, gates.shape, 1)
    gcol = jnp.sum(jnp.where(lane == e, gates, 0.0), axis=1, keepdims=True)
    act = _silu(a) * u * gcol
    acc_sc[...] += jnp.dot(act.astype(BF16), wd_ref[0], preferred_element_type=F32)

    @pl.when(e == pl.num_programs(1) - 1)
    def _():
        o_ref[...] = x_ref[...] + gate_ref[0] * acc_sc[...]


def _moe(h16, gates, x2, gate, wg16, wu16, wd16, tm):
    N, D = x2.shape
    E, _, Fe = wg16.shape
    row = lambda w: pl.BlockSpec((tm, w), lambda i, e: (i, 0))
    return pl.pallas_call(
        _moe_kernel,
        grid=(N // tm, E),
        in_specs=[row(D), row(LANES), row(D), gate.spec(2),
                  pl.BlockSpec((1, D, Fe), lambda i, e: (e, 0, 0)),
                  pl.BlockSpec((1, D, Fe), lambda i, e: (e, 0, 0)),
                  pl.BlockSpec((1, Fe, D), lambda i, e: (e, 0, 0))],
        out_specs=row(D),
        out_shape=jax.ShapeDtypeStruct((N, D), F32),
        scratch_shapes=[pltpu.VMEM((tm, D), F32)],
        compiler_params=_params("parallel", "arbitrary"),
        name="moe",
    )(h16, gates, x2, gate.arr, wg16, wu16, wd16)


def _kvq_kernel(x_ref, shk_ref, sck_ref, gk_ref, shq_ref, scq_ref, gq_ref, wkv_ref, wq_ref,
                gmat_ref, gmatt_ref, gkn_ref, gqn_ref, k32_ref, v32_ref, k16_ref, v16_ref, q_ref):
    x = x_ref[...]
    y = x * lax.rsqrt(jnp.mean(x * x, axis=-1, keepdims=True) + EPS)
    hk = ((y * gk_ref[...]) * (1.0 + sck_ref[0]) + shk_ref[0]).astype(BF16)
    hq = ((y * gq_ref[...]) * (1.0 + scq_ref[0]) + shq_ref[0]).astype(BF16)
    W = k32_ref.shape[1]
    kv = jnp.dot(hk, wkv_ref[...], preferred_element_type=F32)
    k = _group_rms(kv[:, :W], gmat_ref[...], gmatt_ref[...], gkn_ref[...], B_DK)
    v = kv[:, W:]
    k32_ref[...] = k
    v32_ref[...] = v
    k16_ref[...] = k.astype(BF16)
    v16_ref[...] = v.astype(BF16)
    q = jnp.dot(hq, wq_ref[...], preferred_element_type=F32)
    q = _group_rms(q, gmat_ref[...], gmatt_ref[...], gqn_ref[...], B_DK)
    q_ref[...] = q * (B_DK ** -0.5)


def _kvq(x2, shk, sck, gk, shq, scq, gq, wkv16, wq16, gmat, gmat_t, gkn, gqn, tm):
    N, D = x2.shape
    W = wq16.shape[1]
    row = lambda w: pl.BlockSpec((tm, w), lambda i: (i, 0))
    full = lambda shape: pl.BlockSpec(shape, lambda i: (0,) * len(shape))
    return pl.pallas_call(
        _kvq_kernel,
        grid=(N // tm,),
        in_specs=[row(D), shk.spec(1), sck.spec(1), full((1, D)), shq.spec(1), scq.spec(1), full((1, D)),
                  full((D, 2 * W)), full((D, W)), full((W, LANES)), full((LANES, W)),
                  full((1, W)), full((1, W))],
        out_specs=[row(W), row(W), row(W), row(W), row(W)],
        out_shape=[jax.ShapeDtypeStruct((N, W), F32), jax.ShapeDtypeStruct((N, W), F32),
                   jax.ShapeDtypeStruct((N, W), BF16), jax.ShapeDtypeStruct((N, W), BF16),
                   jax.ShapeDtypeStruct((N, W), F32)],
        compiler_params=_params("parallel"),
        name="kvq",
    )(x2, shk.arr, sck.arr, gk, shq.arr, scq.arr, gq, wkv16, wq16, gmat, gmat_t, gkn, gqn)


def _lambda(lam_ref, lam_init):
    lv = lam_ref[...]
    a = jnp.sum(lv[0:1] * lv[1:2], axis=1, keepdims=True)
    b = jnp.sum(lv[2:3] * lv[3:4], axis=1, keepdims=True)
    return jnp.exp(a) - jnp.exp(b) + lam_init


def _softmax_step(s, m_ref, l_ref, acc_ref, pv):
    m_old = m_ref[...]
    m_new = jnp.maximum(m_old, jnp.max(s, axis=1, keepdims=True))
    alpha = jnp.exp(m_old - m_new)
    p = jnp.exp(s - m_new)
    l_ref[...] = alpha * l_ref[...] + jnp.sum(p, axis=1, keepdims=True)
    acc_ref[...] = alpha * acc_ref[...] + pv(p)
    m_ref[...] = m_new


def _attn_kernel(qi_ref, kj_ref, ty_ref, fin_ref, q_ref, k_ref, v_ref, bias_ref, lam_ref, ghn_ref, o_ref,
                 qm_sc, m_sc, l_sc, acc_sc, *, n_types, lam_init):
    step = pl.program_id(2)
    ty = ty_ref[step]

    @pl.when(kj_ref[step] == 0)
    def _():
        q = q_ref[0].astype(BF16)
        lane = lax.broadcasted_iota(jnp.int32, q.shape, 1)
        zero = jnp.zeros_like(q)
        qm_sc[0] = jnp.where(lane < B_DK, q, zero)
        qm_sc[1] = jnp.where(lane >= B_DK, q, zero)
        m_sc[...] = jnp.full_like(m_sc, -jnp.inf)
        l_sc[...] = jnp.zeros_like(l_sc)
        acc_sc[...] = jnp.zeros_like(acc_sc)

    def update(adj):
        k = k_ref[0]
        v = v_ref[0]
        for c in range(2):
            s = _nt_dot(qm_sc[c], k)
            if adj is not None:
                s = s + adj
            _softmax_step(s, m_sc.at[c], l_sc.at[c], acc_sc.at[c],
                          lambda p: jnp.dot(p.astype(BF16), v, preferred_element_type=F32))

    @pl.when(ty < 0)
    def _():
        update(None)

    for t in range(n_types):
        @pl.when(ty == t)
        def _(t=t):
            update(bias_ref[0, t])

    @pl.when(fin_ref[step] == 1)
    def _():
        lam = _lambda(lam_ref, lam_init)
        o = acc_sc[0] / l_sc[0] - lam * (acc_sc[1] / l_sc[1])
        on = o * lax.rsqrt(jnp.mean(o * o, axis=1, keepdims=True) + EPS) * ghn_ref[...] * (1.0 - lam_init)
        o_ref[0] = on.astype(o_ref.dtype)


def _attn_schedule(T, tq, tk):
    offsets = sorted({qi * tq - kj * tk for qi in range(T // tq) for kj in range(T // tk)
                      if qi * tq + tq - 1 >= kj * tk and qi * tq - kj * tk - (tk - 1) < RPB_MAX_DIST})
    qi_l, kj_l, ty_l, fin_l = [], [], [], []
    for qi in range(T // tq):
        kjs = [kj for kj in range(T // tk) if qi * tq + tq - 1 >= kj * tk]
        for kj in kjs:
            off = qi * tq - kj * tk
            qi_l.append(qi)
            kj_l.append(kj)
            ty_l.append(offsets.index(off) if off in offsets else -1)
            fin_l.append(int(kj == kjs[-1]))
    as_i32 = lambda v: jnp.asarray(np.asarray(v, np.int32))
    return offsets, as_i32(qi_l), as_i32(kj_l), as_i32(ty_l), as_i32(fin_l)


def _bias_table(rpb):
    n = jnp.arange(RPB_MAX_DIST, dtype=jnp.int32)
    max_exact = RPB_BUCKETS // 2
    nf = jnp.maximum(n, 1).astype(F32)
    large = max_exact + (jnp.log(nf / max_exact) / math.log(RPB_MAX_DIST / max_exact)
                         * (RPB_BUCKETS - max_exact)).astype(jnp.int32)
    bucket = jnp.where(n < max_exact, n, jnp.minimum(large, RPB_BUCKETS - 1))
    return (rpb[bucket] - rpb[RPB_BUCKETS - 1][None, :]).T.astype(F32)


def _bias_tile(tbl, dist):
    d = np.asarray(dist)
    idx = jnp.asarray(np.clip(d, 0, RPB_MAX_DIST - 1).astype(np.int32))
    vals = jnp.take(tbl, idx, axis=1)
    vals = jnp.where(jnp.asarray(d >= RPB_MAX_DIST), 0.0, vals)
    return jnp.where(jnp.asarray(d < 0), -jnp.inf, vals)


def _attn_prompt(q, k16, v16, tbl, lam, ghn, lam_init, tq, tk):
    B, T, W = q.shape
    H = W // LANES
    offsets, qi, kj, ty, fin = _attn_schedule(T, tq, tk)
    r = np.arange(tq)[:, None]
    c = np.arange(tk)[None, :]
    bias = _bias_tile(tbl, np.stack([off + r - c for off in offsets]))
    kern = functools.partial(_attn_kernel, n_types=len(offsets), lam_init=lam_init)
    grid_spec = pltpu.PrefetchScalarGridSpec(
        num_scalar_prefetch=4,
        grid=(H, B, int(qi.shape[0])),
        in_specs=[pl.BlockSpec((1, tq, LANES), lambda h, b, s, qi, kj, ty, fin: (b, qi[s], h)),
                  pl.BlockSpec((1, tk, LANES), lambda h, b, s, qi, kj, ty, fin: (b, kj[s], h)),
                  pl.BlockSpec((1, tk, LANES), lambda h, b, s, qi, kj, ty, fin: (b, kj[s], h)),
                  pl.BlockSpec((1, len(offsets), tq, tk), lambda h, b, s, *_: (h, 0, 0, 0)),
                  pl.BlockSpec(lam.shape, lambda h, b, s, *_: (0, 0)),
                  pl.BlockSpec((1, LANES), lambda h, b, s, *_: (0, h))],
        out_specs=pl.BlockSpec((1, tq, LANES), lambda h, b, s, qi, kj, ty, fin: (b, qi[s], h)),
        scratch_shapes=[pltpu.VMEM((2, tq, LANES), BF16), pltpu.VMEM((2, tq, 1), F32),
                        pltpu.VMEM((2, tq, 1), F32), pltpu.VMEM((2, tq, LANES), F32)],
    )
    return pl.pallas_call(
        kern,
        grid_spec=grid_spec,
        out_shape=jax.ShapeDtypeStruct((B, T, W), BF16),
        compiler_params=_params("parallel", "parallel", "arbitrary"),
        name="attn_prompt",
    )(qi, kj, ty, fin, q, k16, v16, bias, lam, ghn)


def _attn_paged_kernel(pt_ref, q_ref, kc_ref, vc_ref, kn_ref, vn_ref, bias_ref, lam_ref, ghn_ref, o_ref,
                       kx_sc, vx_sc, qm_sc, m_sc, l_sc, acc_sc, *, heads, t_new, n_pages, lam_init):
    b = pl.program_id(0)
    p = pl.program_id(1)
    R = SUBLANES

    @pl.when((b == 0) & (p == 0))
    def _():
        kx_sc[...] = jnp.zeros_like(kx_sc)
        vx_sc[...] = jnp.zeros_like(vx_sc)

    @pl.when(p == 0)
    def _():
        q = q_ref[0]
        row = lax.broadcasted_iota(jnp.int32, (R, LANES), 0)
        lane = lax.broadcasted_iota(jnp.int32, (R, LANES), 1)
        keep = (row < t_new) == (lane < B_DK)
        for h in range(heads):
            qm_sc[h * R:(h + 1) * R, :] = jnp.where(keep, q[:, h * LANES:(h + 1) * LANES], 0.0)
        m_sc[...] = jnp.full_like(m_sc, -jnp.inf)
        l_sc[...] = jnp.zeros_like(l_sc)
        acc_sc[...] = jnp.zeros_like(acc_sc)
        kx_sc[0:R, :] = kn_ref[0]
        vx_sc[0:R, :] = vn_ref[0]

    def update(kp, vp, adj):
        kp = kp.astype(BF16)
        vp = vp.astype(BF16)
        qm = qm_sc[...].astype(BF16)
        s = jnp.concatenate([_nt_dot(qm[h * R:(h + 1) * R], kp[:, h * LANES:(h + 1) * LANES])
                             for h in range(heads)], axis=0)
        if adj is not None:
            s = s + adj

        def pv(pm):
            pm = pm.astype(BF16)
            return jnp.concatenate([jnp.dot(pm[h * R:(h + 1) * R], vp[:, h * LANES:(h + 1) * LANES],
                                            preferred_element_type=F32) for h in range(heads)], axis=0)

        _softmax_step(s, m_sc, l_sc, acc_sc, pv)

    @pl.when(p < n_pages - 1)
    def _():
        update(kc_ref[0], vc_ref[0], None)

    @pl.when(p == n_pages - 1)
    def _():
        update(kc_ref[0], vc_ref[0], bias_ref[0])

    @pl.when(p == n_pages)
    def _():
        update(kx_sc[...], vx_sc[...], bias_ref[1])
        lam = _lambda(lam_ref, lam_init)
        full = acc_sc[...] / l_sc[...]
        for h in range(heads):
            fh = full[h * R:(h + 1) * R]
            o = fh - lam * pltpu.roll(fh, R - t_new, 0)
            on = o * lax.rsqrt(jnp.mean(o * o, axis=1, keepdims=True) + EPS)
            o_ref[0, :, h * LANES:(h + 1) * LANES] = on * ghn_ref[:, h * LANES:(h + 1) * LANES] * (1.0 - lam_init)


def _attn_paged(q, cache_k, cache_v, page_table, k_new, v_new, tbl, lam, ghn, lam_init):
    B, t_new, W = q.shape
    H = W // LANES
    n_pool, page = cache_k.shape[:2]
    n_pages = page_table.shape[1]
    past = n_pages * page
    R = SUBLANES
    assert 2 * t_new == R and page == LANES and page >= RPB_MAX_DIST
    pad = lambda a: jnp.concatenate([a, jnp.zeros((B, R - t_new, W), a.dtype)], axis=1)
    q8 = jnp.concatenate([q, q], axis=1)
    t = np.arange(R)[:, None] % t_new
    c = np.arange(page)[None, :]
    d_last = past + t - ((n_pages - 1) * page + c)
    d_new = np.where(c < t_new, t - c, -1)
    bias = _bias_tile(tbl, np.stack([d_last, d_new]))
    bias = bias.transpose(1, 0, 2, 3).reshape(2, H * R, page)
    kern = functools.partial(_attn_paged_kernel, heads=H, t_new=t_new, n_pages=n_pages, lam_init=lam_init)
    page_spec = pl.BlockSpec((1, page, W), lambda b, p, pt: (pt[b * n_pages + jnp.minimum(p, n_pages - 1)], 0, 0))
    per_b = pl.BlockSpec((1, R, W), lambda b, p, pt: (b, 0, 0))
    grid_spec = pltpu.PrefetchScalarGridSpec(
        num_scalar_prefetch=1,
        grid=(B, n_pages + 1),
        in_specs=[per_b, page_spec, page_spec, per_b, per_b,
                  pl.BlockSpec(bias.shape, lambda b, p, pt: (0, 0, 0)),
                  pl.BlockSpec(lam.shape, lambda b, p, pt: (0, 0)),
                  pl.BlockSpec((1, W), lambda b, p, pt: (0, 0))],
        out_specs=per_b,
        scratch_shapes=[pltpu.VMEM((page, W), F32), pltpu.VMEM((page, W), F32),
                        pltpu.VMEM((H * R, LANES), F32), pltpu.VMEM((H * R, 1), F32),
                        pltpu.VMEM((H * R, 1), F32), pltpu.VMEM((H * R, LANES), F32)],
    )
    out = pl.pallas_call(
        kern,
        grid_spec=grid_spec,
        out_shape=jax.ShapeDtypeStruct((B, R, W), F32),
        compiler_params=_params("arbitrary", "arbitrary"),
        name="attn_paged",
    )(page_table.reshape(-1), q8, cache_k.reshape(n_pool, page, W), cache_v.reshape(n_pool, page, W),
      pad(k_new), pad(v_new), bias, lam, ghn)
    return out[:, :t_new]


def _trunk(x, mods, mods_kv, state, past, wts):
    B, T, D = x.shape
    N = B * T
    tm = ROW_TILE
    mod = lambda m: _Mod(m, T, tm)
    split3 = lambda m: (mod(m[:, :D]), mod(m[:, D:2 * D]), mod(m[:, 2 * D:]))
    x2 = x.reshape(N, D)

    heads = A_HEADS
    inner = wts["w_out16"].shape[0]
    dh = inner // heads
    shift, scale, gate = split3(mods[0])
    proj, gates = _inproj(x2, shift, scale, wts["g_norm"][0, 0][None], wts["w_in16"], wts["w_gate"], tm)
    proj = proj.reshape(B, T, 4 * inner)
    gates = gates.reshape(B, T, LANES)
    conv_new = proj[:, T - (A_CONV - 1):, :2 * inner]
    if state is None:
        L, t_valid = math.gcd(T, MLSTM_CHUNK), None
        conv_init = jnp.zeros((B, SUBLANES, 2 * inner), F32)
        c0 = jnp.zeros((B, heads, dh, dh), F32)
        n0 = jnp.zeros((B, heads, 1, dh), F32)
        m0 = jnp.zeros((B, heads, 1, 1), F32)
    else:
        conv_st, c_st, n_st, m_st = state
        assert T <= SUBLANES
        L, t_valid = SUBLANES, T
        rows = lambda a: jnp.concatenate([a, jnp.zeros((B, L - T, a.shape[2]), a.dtype)], axis=1)
        proj, gates = rows(proj), rows(gates)
        conv_init = jnp.concatenate([jnp.zeros((B, SUBLANES - (A_CONV - 1), 2 * inner), F32), conv_st], axis=1)
        c0, n0, m0 = c_st, n_st[:, :, None, :], m_st[:, :, None, None]
    gates_t = jnp.swapaxes(gates[:, :, :SUBLANES], 1, 2)
    hs, c1, n1, m1 = _mlstm(proj, gates, gates_t, wts["bg"], wts["bgt"], conv_init, c0, n0, m0,
                            wts["w_conv"], wts["b_conv"], wts["g_hn_a"], L, t_valid)
    hs = hs[:, :T].reshape(N, inner)
    new_state = (conv_new, c1, n1[:, :, 0, :], m1[:, :, 0, 0])

    shift2, scale2, gate2 = split3(mods[1])
    x2, h16, rg = _proj_router(hs, wts["w_out16"], x2, gate, shift2, scale2, wts["g_norm"][0, 1][None],
                               wts["w_router"], wts["b_router"], tm)
    x2 = _moe(h16, rg, x2, gate2, wts["wg16"][0], wts["wu16"][0], wts["wd16"][0], tm)

    shift_kv, scale_kv = mod(mods_kv[:, :D]), mod(mods_kv[:, D:])
    shift, scale, gate = split3(mods[2])
    k32, v32, k16, v16, q = _kvq(x2, shift_kv, scale_kv, wts["g_kv"], shift, scale, wts["g_norm"][1, 0][None],
                                 wts["w_kv16"], wts["w_q16"], wts["gmat"], wts["gmat_t"], wts["g_kn"], wts["g_qn"],
                                 tm)
    W = k32.shape[1]
    lam_init = 0.8 - 0.6 * math.exp(-0.3 * 1)
    if past is None:
        o = _attn_prompt(q.reshape(B, T, W), k16.reshape(B, T, W), v16.reshape(B, T, W), wts["rpb_tbl"],
                         wts["lam"], wts["g_hn_b"], lam_init, math.gcd(T, ATTN_TQ), math.gcd(T, ATTN_TK))
        o = o.reshape(N, W)
    else:
        cache_k, cache_v, page_table = past
        o = _attn_paged(q.reshape(B, T, W), cache_k, cache_v, page_table, k32.reshape(B, T, W),
                        v32.reshape(B, T, W), wts["rpb_tbl"], wts["lam"], wts["g_hn_b"], lam_init)
        o = o.reshape(N, W).astype(BF16)

    shift2, scale2, gate2 = split3(mods[3])
    x2, h16, rg = _proj_router(o, wts["w_o16"], x2, gate, shift2, scale2, wts["g_norm"][1, 1][None],
                               wts["w_router"], wts["b_router"], tm)
    x2 = _moe(h16, rg, x2, gate2, wts["wg16"][1], wts["wu16"][1], wts["wd16"][1], tm)

    H = W // LANES
    return (x2.reshape(B, T, D), new_state,
            k32.reshape(B, T, H, 2, B_DK), v32.reshape(B, T, H, B_DV))


def kernel(x_prompt, x_sample, c_prompt, c_sample, state_conv, state_C, state_n, state_m, cache_k, cache_v, page_table, w_ada, b_ada, g_norm, w_in_a, b_gate_a, w_conv_a, b_conv_a, g_hn_a, w_out_a, g_kv, w_ada_kv, b_ada_kv, w_kv, g_kn, w_q_b, g_qn_b, lam_b, g_hn_b, w_o_b, rpb, w_router, b_router, w_gate_e, w_up_e, w_down_e):
    Bp, Tp, D = x_prompt.shape
    Bs = x_sample.shape[0]
    inner = w_out_a.shape[1]
    heads_b = g_hn_b.shape[1]
    W = heads_b * B_DV

    n_c = Bp + Bs
    c_all = jnp.concatenate([c_prompt, c_sample, jnp.zeros((-n_c % SUBLANES, D), F32)], axis=0)
    mods = _ada(c_all, w_ada.reshape(-1, D, 3 * D), b_ada.reshape(-1, 1, 3 * D))
    mods_kv = _ada(c_all, w_ada_kv[None], b_ada_kv[None, None])[0]

    n_gate = 2 * A_HEADS
    group_of_lane = np.arange(W) // B_DK
    gmat = jnp.asarray((group_of_lane[:, None] == np.arange(LANES)[None, :]).astype(np.float32))
    wts = {
        "g_norm": g_norm,
        "w_in16": w_in_a[0][:, :4 * inner].astype(BF16),
        "w_gate": jnp.pad(w_in_a[0][:, 4 * inner:], ((0, 0), (0, LANES - n_gate))),
        "bg": jnp.pad(b_gate_a[0], (0, LANES - n_gate))[None, :],
        "bgt": b_gate_a[0][:, None],
        "w_conv": w_conv_a[0], "b_conv": b_conv_a[0][None, :],
        "g_hn_a": g_hn_a[0].reshape(1, inner),
        "w_out16": w_out_a[0].astype(BF16),
        "g_kv": g_kv[None, :],
        "w_kv16": w_kv.astype(BF16), "w_q16": w_q_b[0].astype(BF16),
        "gmat": gmat, "gmat_t": gmat.T,
        "g_kn": jnp.tile(g_kn.reshape(-1), heads_b)[None, :],
        "g_qn": jnp.tile(g_qn_b[0].reshape(-1), heads_b)[None, :],
        "lam": lam_b[0], "g_hn_b": g_hn_b[0].reshape(1, W),
        "w_o16": w_o_b[0].astype(BF16),
        "rpb_tbl": _bias_table(rpb),
        "w_router": jnp.pad(w_router, ((0, 0), (0, LANES - N_EXPERTS))),
        "b_router": b_router[:, None],
        "wg16": w_gate_e.astype(BF16), "wu16": w_up_e.astype(BF16), "wd16": w_down_e.astype(BF16),
    }

    y_p, st_p, k_p, v_p = _trunk(x_prompt, mods[:, :Bp], mods_kv[:Bp], None, None, wts)
    y_s, st_s, k_s, v_s = _trunk(x_sample, mods[:, Bp:n_c], mods_kv[Bp:n_c],
                                 (state_conv[0], state_C[0], state_n[0], state_m[0]),
                                 (cache_k, cache_v, page_table), wts)
    stack = lambda st: tuple(a[None] for a in st)
    return (y_p, y_s) + stack(st_p) + (k_p, v_p) + stack(st_s) + (k_s, v_s)
```

```python
import functools
import math

import numpy as np
import jax
import jax.numpy as jnp
from jax import lax
from jax.experimental import pallas as pl
from jax.experimental.pallas import tpu as pltpu

F32, BF16 = jnp.float32, jnp.bfloat16
HIGHEST = lax.Precision.HIGHEST
EPS = 1e-6

A_HEADS = 4
A_CONV = 4
B_DK = 64
B_DV = 128
N_EXPERTS = 16
N_GROUPS = 4
RPB_BUCKETS = 32
RPB_MAX_DIST = 128

LANES = 128
SUBLANES = 8
VMEM_LIMIT_BYTES = 56 * 1024 * 1024

ROW_TILE = 512
MLSTM_CHUNK = 256
ATTN_TQ = 512
ATTN_TK = 512


def _params(*sem):
    return pltpu.CompilerParams(dimension_semantics=sem, vmem_limit_bytes=VMEM_LIMIT_BYTES)


def _nt_dot(a, b):
    return lax.dot_general(a, b, (((1,), (1,)), ((), ())), preferred_element_type=F32)


def _tn_dot(a, b):
    return lax.dot_general(a, b, (((0,), (0,)), ((), ())), preferred_element_type=F32)


def _silu(x):
    return x * jax.nn.sigmoid(x)


def _rms_mod(x, g, scale, shift):
    y = x * lax.rsqrt(jnp.mean(x * x, axis=-1, keepdims=True) + EPS)
    return (y * g) * (1.0 + scale) + shift


def _dot_split(a, b16):
    hi = a.astype(BF16)
    lo = (a - hi.astype(F32)).astype(BF16)
    return (jnp.dot(hi, b16, preferred_element_type=F32) + jnp.dot(lo, b16, preferred_element_type=F32))


def _group_rms(x, gmat, gmat_t, g, group):
    ss = _dot_split(x * x, gmat)
    r = lax.rsqrt(ss * (1.0 / group) + EPS)
    rf = _dot_split(r, gmat_t)
    return x * rf * g


def _ada_kernel(c_ref, w_ref, b_ref, o_ref):
    a = _silu(c_ref[...])
    o_ref[0] = jnp.dot(a, w_ref[0], precision=HIGHEST, preferred_element_type=F32) + b_ref[0]


def _ada(c_all, w, b):
    S, D, Fo = w.shape
    R = c_all.shape[0]
    tn = 1024
    return pl.pallas_call(
        _ada_kernel,
        grid=(S, Fo // tn),
        in_specs=[pl.BlockSpec((R, D), lambda s, j: (0, 0)),
                  pl.BlockSpec((1, D, tn), lambda s, j: (s, 0, j)),
                  pl.BlockSpec((1, 1, tn), lambda s, j: (s, 0, j))],
        out_specs=pl.BlockSpec((1, R, tn), lambda s, j: (s, 0, j)),
        out_shape=jax.ShapeDtypeStruct((S, R, Fo), F32),
        compiler_params=_params("parallel", "parallel"),
        name="ada",
    )(c_all, w, b)


class _Mod:
    def __init__(self, m, T, tm):
        B, D = m.shape
        if T % tm == 0:
            self.arr, self.tiles_per_group = m[:, None, :], T // tm
        else:
            assert (B * T) % tm == 0
            self.arr, self.tiles_per_group = jnp.repeat(m, T, axis=0).reshape(-1, tm, D), 1

    def spec(self, grid_rank):
        R, D = self.arr.shape[1:]
        tpg = self.tiles_per_group
        if grid_rank == 1:
            return pl.BlockSpec((1, R, D), lambda i: (i // tpg, 0, 0))
        return pl.BlockSpec((1, R, D), lambda i, j: (i // tpg, 0, 0))


def _inproj_kernel(x_ref, sh_ref, sc_ref, g_ref, w_ref, wg_ref, o_ref, og_ref, h_sc):
    @pl.when(pl.program_id(1) == 0)
    def _():
        h = _rms_mod(x_ref[...], g_ref[...], sc_ref[0], sh_ref[0])
        h_sc[...] = h.astype(BF16)
        og_ref[...] = jnp.dot(h, wg_ref[...], precision=HIGHEST, preferred_element_type=F32)

    o_ref[...] = jnp.dot(h_sc[...], w_ref[...], preferred_element_type=F32)


def _inproj(x2, shift, scale, g, w16, wgate, tm):
    N, D = x2.shape
    Fo = w16.shape[1]
    tn = 2048
    return pl.pallas_call(
        _inproj_kernel,
        grid=(N // tm, Fo // tn),
        in_specs=[pl.BlockSpec((tm, D), lambda i, j: (i, 0)),
                  shift.spec(2), scale.spec(2),
                  pl.BlockSpec((1, D), lambda i, j: (0, 0)),
                  pl.BlockSpec((D, tn), lambda i, j: (0, j)),
                  pl.BlockSpec((D, LANES), lambda i, j: (0, 0))],
        out_specs=[pl.BlockSpec((tm, tn), lambda i, j: (i, j)),
                   pl.BlockSpec((tm, LANES), lambda i, j: (i, 0))],
        out_shape=[jax.ShapeDtypeStruct((N, Fo), F32), jax.ShapeDtypeStruct((N, LANES), F32)],
        scratch_shapes=[pltpu.VMEM((tm, D), BF16)],
        compiler_params=_params("parallel", "arbitrary"),
        name="mlstm_inproj",
    )(x2, shift.arr, scale.arr, g, w16, wgate)


def _mlstm_kernel(q_ref, k_ref, v_ref, o_ref, gt_ref, gtt_ref, bg_ref, bgt_ref, cinit_ref,
                  c0_ref, n0_ref, m0_ref, wconv_ref, bconv_ref, ghn_ref,
                  hs_ref, c_ref, n_ref, m_ref, tail_sc, *, L, dh, heads, t_valid):
    inner = heads * dh

    @pl.when(pl.program_id(1) == 0)
    def _():
        c_ref[...] = c0_ref[...]
        n_ref[...] = n0_ref[...]
        m_ref[...] = m0_ref[...]
        tail_sc[...] = cinit_ref[0]

    row8 = lax.broadcasted_iota(jnp.int32, (SUBLANES, dh), 0)

    def conv(x, tail, w, b):
        acc = b + x * w[A_CONV - 1:A_CONV]
        for s in range(1, A_CONV):
            xs = pltpu.roll(x, s, 0)
            top = jnp.where(row8 < s, pltpu.roll(tail, s, 0), xs[:SUBLANES])
            xs = top if L == SUBLANES else jnp.concatenate([top, xs[SUBLANES:]], axis=0)
            acc = acc + xs * w[A_CONV - 1 - s:A_CONV - s]
        return acc

    gt = gt_ref[0] + bg_ref[...]
    gtt = gtt_ref[0] + bgt_ref[...]
    ti = lax.broadcasted_iota(jnp.int32, (L, L), 0)
    si = lax.broadcasted_iota(jnp.int32, (L, L), 1)
    causal = si <= ti
    tcol = lax.broadcasted_iota(jnp.int32, (L, 1), 0)
    trow = lax.broadcasted_iota(jnp.int32, (1, L), 1)

    for h in range(heads):
        sl = slice(h * dh, (h + 1) * dh)
        slk = slice(inner + h * dh, inner + (h + 1) * dh)
        qh = _silu(conv(q_ref[0, :, sl], tail_sc[:, sl], wconv_ref[:, sl], bconv_ref[:, sl]))
        kh = _silu(conv(k_ref[0, :, sl], tail_sc[:, slk], wconv_ref[:, slk], bconv_ref[:, slk])) * (dh ** -0.5)
        vb = v_ref[0, :, sl].astype(BF16)

        ig_col = gt[:, h:h + 1]
        lf_col = jax.nn.log_sigmoid(gt[:, heads + h:heads + h + 1])
        ig_row = gtt[h:h + 1, :]
        lf_row = jax.nn.log_sigmoid(gtt[heads + h:heads + h + 1, :])
        if t_valid is not None:
            ig_col = jnp.where(tcol < t_valid, ig_col, -jnp.inf)
            lf_col = jnp.where(tcol < t_valid, lf_col, 0.0)
            ig_row = jnp.where(trow < t_valid, ig_row, -jnp.inf)
            lf_row = jnp.where(trow < t_valid, lf_row, 0.0)

        b_col = jnp.sum(jnp.where(causal, lf_row, 0.0), axis=1, keepdims=True)
        b_row = jnp.sum(jnp.where(ti <= si, lf_col, 0.0), axis=0, keepdims=True)
        dlog = jnp.where(causal, b_col - b_row + ig_row, -jnp.inf)
        g_col = b_col + m_ref[0, h]
        m_col = jnp.maximum(g_col, jnp.max(dlog, axis=1, keepdims=True))
        w_intra = jnp.exp(dlog - m_col)
        w_inter = jnp.exp(g_col - m_col)

        qb = qh.astype(BF16)
        kb = kh.astype(BF16)
        s = w_intra * _nt_dot(qb, kb)
        ch = c_ref[0, h]
        nh = n_ref[0, h]
        num = w_inter * _nt_dot(qb, ch.astype(BF16)) + jnp.dot(s.astype(BF16), vb, preferred_element_type=F32)
        den = w_inter * jnp.sum(qh * nh, axis=1, keepdims=True) + jnp.sum(s, axis=1, keepdims=True)
        hv = num / jnp.maximum(jnp.abs(den), jnp.exp(-m_col))

        m_end = m_col[L - 1:L]
        we_inter = jnp.exp(g_col[L - 1:L] - m_end)
        we_col = jnp.exp(b_col[L - 1:L] - b_col + ig_col - m_end)
        kw = kh * we_col
        c_ref[0, h] = we_inter * ch + _tn_dot(vb, kw.astype(BF16))
        n_ref[0, h] = we_inter * nh + jnp.sum(kw, axis=0, keepdims=True)
        m_ref[0, h] = m_end

        hn = hv * lax.rsqrt(jnp.mean(hv * hv, axis=1, keepdims=True) + EPS) * ghn_ref[:, sl]
        hs_ref[0, :, sl] = (jax.nn.sigmoid(o_ref[0, :, sl]) * hn).astype(hs_ref.dtype)

    tail_sc[:, :inner] = q_ref[0, L - SUBLANES:, :]
    tail_sc[:, inner:] = k_ref[0, L - SUBLANES:, :]


def _mlstm(proj, gates, gates_t, bg, bgt, conv_init, c0, n0, m0, wconv, bconv, ghn, L, t_valid):
    B, Tp, _ = proj.shape
    heads, dh = c0.shape[1], c0.shape[2]
    inner = heads * dh
    nc = Tp // L
    kern = functools.partial(_mlstm_kernel, L=L, dh=dh, heads=heads, t_valid=t_valid)
    col = lambda j: pl.BlockSpec((1, L, inner), lambda b, c: (b, c, j))
    full = lambda shape: pl.BlockSpec(shape, lambda b, c: (0,) * len(shape))
    per_b = lambda shape: pl.BlockSpec((1,) + shape, lambda b, c: (b,) + (0,) * len(shape))
    return pl.pallas_call(
        kern,
        grid=(B, nc),
        in_specs=[col(0), col(1), col(2), col(3),
                  pl.BlockSpec((1, L, LANES), lambda b, c: (b, c, 0)),
                  pl.BlockSpec((1, SUBLANES, L), lambda b, c: (b, 0, c)),
                  full((1, LANES)), full((SUBLANES, 1)),
                  per_b((SUBLANES, 2 * inner)),
                  per_b((heads, dh, dh)), per_b((heads, 1, dh)), per_b((heads, 1, 1)),
                  full((A_CONV, 2 * inner)), full((1, 2 * inner)), full((1, inner))],
        out_specs=[pl.BlockSpec((1, L, inner), lambda b, c: (b, c, 0)),
                   per_b((heads, dh, dh)), per_b((heads, 1, dh)), per_b((heads, 1, 1))],
        out_shape=[jax.ShapeDtypeStruct((B, Tp, inner), BF16),
                   jax.ShapeDtypeStruct((B, heads, dh, dh), F32),
                   jax.ShapeDtypeStruct((B, heads, 1, dh), F32),
                   jax.ShapeDtypeStruct((B, heads, 1, 1), F32)],
        scratch_shapes=[pltpu.VMEM((SUBLANES, 2 * inner), F32)],
        compiler_params=_params("parallel", "arbitrary"),
        name="mlstm",
    )(proj, proj, proj, proj, gates, gates_t, bg, bgt, conv_init, c0, n0, m0, wconv, bconv, ghn)


def _route(h, wr, br):
    tm = h.shape[0]
    per = N_EXPERTS // N_GROUPS
    logits = jnp.dot(h, wr, precision=HIGHEST, preferred_element_type=F32)
    lt = logits.T[:N_EXPERTS]
    s = jax.nn.sigmoid(lt)
    sel = s + br
    neg = jnp.full((1, tm), -jnp.inf, F32)
    izero = jnp.zeros((1, tm), jnp.int32)

    best_score = best_e1 = best_e2 = best_w1 = best_w2 = None
    for grp in range(N_GROUPS):
        rows = [sel[grp * per + j:grp * per + j + 1] for j in range(per)]
        srow = [s[grp * per + j:grp * per + j + 1] for j in range(per)]
        t1, i1, w1 = rows[0], izero, srow[0]
        for j in range(1, per):
            better = rows[j] > t1
            t1 = jnp.where(better, rows[j], t1)
            i1 = jnp.where(better, j, i1)
            w1 = jnp.where(better, srow[j], w1)
        t2, i2, w2 = neg, izero, srow[0]
        for j in range(per):
            better = jnp.where(i1 == j, neg, rows[j]) > t2
            t2 = jnp.where(better, rows[j], t2)
            i2 = jnp.where(better, j, i2)
            w2 = jnp.where(better, srow[j], w2)
        score = t1 + t2
        e1, e2 = i1 + grp * per, i2 + grp * per
        if grp == 0:
            best_score, best_e1, best_e2, best_w1, best_w2 = score, e1, e2, w1, w2
        else:
            better = score > best_score
            best_score = jnp.where(better, score, best_score)
            best_e1 = jnp.where(better, e1, best_e1)
            best_e2 = jnp.where(better, e2, best_e2)
            best_w1 = jnp.where(better, w1, best_w1)
            best_w2 = jnp.where(better, w2, best_w2)
    tot = best_w1 + best_w2
    eid = lax.broadcasted_iota(jnp.int32, (LANES, tm), 0)
    gt = jnp.where(eid == best_e1, best_w1 / tot, 0.0) + jnp.where(eid == best_e2, best_w2 / tot, 0.0)
    return gt.T


def _proj_router_kernel(a_ref, w_ref, x_ref, gate_ref, sh_ref, sc_ref, g_ref, wr_ref, br_ref,
                        xo_ref, h_ref, gates_ref):
    mix = jnp.dot(a_ref[...], w_ref[...], preferred_element_type=F32)
    x = x_ref[...] + gate_ref[0] * mix
    xo_ref[...] = x
    h = _rms_mod(x, g_ref[...], sc_ref[0], sh_ref[0])
    h_ref[...] = h.astype(BF16)
    gates_ref[...] = _route(h, wr_ref[...], br_ref[...])


def _proj_router(a16, w16, x2, gate, shift, scale, g, wr, br, tm):
    N, D = x2.shape
    K = a16.shape[1]
    row = lambda w: pl.BlockSpec((tm, w), lambda i: (i, 0))
    full = lambda shape: pl.BlockSpec(shape, lambda i: (0,) * len(shape))
    return pl.pallas_call(
        _proj_router_kernel,
        grid=(N // tm,),
        in_specs=[row(K), full((K, D)), row(D), gate.spec(1), shift.spec(1), scale.spec(1),
                  full((1, D)), full((D, LANES)), full((N_EXPERTS, 1))],
        out_specs=[row(D), row(D), row(LANES)],
        out_shape=[jax.ShapeDtypeStruct((N, D), F32), jax.ShapeDtypeStruct((N, D), BF16),
                   jax.ShapeDtypeStruct((N, LANES), F32)],
        compiler_params=_params("parallel"),
        name="proj_router",
    )(a16, w16, x2, gate.arr, shift.arr, scale.arr, g, wr, br)


def _moe_kernel(h_ref, gates_ref, x_ref, gate_ref, wg_ref, wu_ref, wd_ref, o_ref, acc_sc):
    e = pl.program_id(1)

    @pl.when(e == 0)
    def _():
        acc_sc[...] = jnp.zeros_like(acc_sc)

    h = h_ref[...]
    a = jnp.dot(h, wg_ref[0], preferred_element_type=F32)
    u = jnp.dot(h, wu_ref[0], preferred_element_type=F32)
    gates = gates_ref[...]
    lane = lax.broadcasted_iota(jnp.int32, gates.shape, 1)
    gcol = jnp.sum(jnp.where(lane == e, gates, 0.0), axis=1, keepdims=True)
    act = _silu(a) * u * gcol
    acc_sc[...] += jnp.dot(act.astype(BF16), wd_ref[0], preferred_element_type=F32)

    @pl.when(e == pl.num_programs(1) - 1)
    def _():
        o_ref[...] = x_ref[...] + gate_ref[0] * acc_sc[...]


def _moe(h16, gates, x2, gate, wg16, wu16, wd16, tm):
    N, D = x2.shape
    E, _, Fe = wg16.shape
    row = lambda w: pl.BlockSpec((tm, w), lambda i, e: (i, 0))
    return pl.pallas_call(
        _moe_kernel,
        grid=(N // tm, E),
        in_specs=[row(D), row(LANES), row(D), gate.spec(2),
                  pl.BlockSpec((1, D, Fe), lambda i, e: (e, 0, 0)),
                  pl.BlockSpec((1, D, Fe), lambda i, e: (e, 0, 0)),
                  pl.BlockSpec((1, Fe, D), lambda i, e: (e, 0, 0))],
        out_specs=row(D),
        out_shape=jax.ShapeDtypeStruct((N, D), F32),
        scratch_shapes=[pltpu.VMEM((tm, D), F32)],
        compiler_params=_params("parallel", "arbitrary"),
        name="moe",
    )(h16, gates, x2, gate.arr, wg16, wu16, wd16)


def _kvq_kernel(x_ref, shk_ref, sck_ref, gk_ref, shq_ref, scq_ref, gq_ref, wkv_ref, wq_ref,
                gmat_ref, gmatt_ref, gkn_ref, gqn_ref, k32_ref, v32_ref, q_ref, *maybe_attn_refs):
    x = x_ref[...]
    y = x * lax.rsqrt(jnp.mean(x * x, axis=-1, keepdims=True) + EPS)
    hk = ((y * gk_ref[...]) * (1.0 + sck_ref[0]) + shk_ref[0]).astype(BF16)
    hq = ((y * gq_ref[...]) * (1.0 + scq_ref[0]) + shq_ref[0]).astype(BF16)
    W = k32_ref.shape[1]
    kv = jnp.dot(hk, wkv_ref[...], preferred_element_type=F32)
    k = _group_rms(kv[:, :W], gmat_ref[...], gmatt_ref[...], gkn_ref[...], B_DK)
    v = kv[:, W:]
    k32_ref[...] = k
    v32_ref[...] = v
    q = jnp.dot(hq, wq_ref[...], preferred_element_type=F32)
    q = _group_rms(q, gmat_ref[...], gmatt_ref[...], gqn_ref[...], B_DK)
    q_ref[...] = (q * (B_DK ** -0.5)).astype(q_ref.dtype)
    if maybe_attn_refs:
        k16_ref, vt_ref = maybe_attn_refs
        k16_ref[...] = k.astype(BF16)
        vt_ref[0] = v.T.astype(BF16)


def _kvq(x2, shk, sck, gk, shq, scq, gq, wkv16, wq16, gmat, gmat_t, gkn, gqn, tm, seq_len):
    N, D = x2.shape
    W = wq16.shape[1]
    row = lambda w: pl.BlockSpec((tm, w), lambda i: (i, 0))
    full = lambda shape: pl.BlockSpec(shape, lambda i: (0,) * len(shape))
    out_specs = [row(W), row(W), row(W)]
    out_shape = [jax.ShapeDtypeStruct((N, W), F32), jax.ShapeDtypeStruct((N, W), F32)]
    if seq_len % tm == 0:
        tps = seq_len // tm
        out_specs += [row(W), pl.BlockSpec((1, W, tm), lambda i: (i // tps, 0, i % tps))]
        out_shape += [jax.ShapeDtypeStruct((N, W), BF16), jax.ShapeDtypeStruct((N, W), BF16),
                      jax.ShapeDtypeStruct((N // seq_len, W, seq_len), BF16)]
    else:
        out_shape += [jax.ShapeDtypeStruct((N, W), F32)]
    return pl.pallas_call(
        _kvq_kernel,
        grid=(N // tm,),
        in_specs=[row(D), shk.spec(1), sck.spec(1), full((1, D)), shq.spec(1), scq.spec(1), full((1, D)),
                  full((D, 2 * W)), full((D, W)), full((W, LANES)), full((LANES, W)),
                  full((1, W)), full((1, W))],
        out_specs=out_specs,
        out_shape=out_shape,
        compiler_params=_params("parallel"),
        name="kvq",
    )(x2, shk.arr, sck.arr, gk, shq.arr, scq.arr, gq, wkv16, wq16, gmat, gmat_t, gkn, gqn)


def _lambda(lam_ref, lam_init):
    lv = lam_ref[...]
    a = jnp.sum(lv[0:1] * lv[1:2], axis=1, keepdims=True)
    b = jnp.sum(lv[2:3] * lv[3:4], axis=1, keepdims=True)
    return jnp.exp(a) - jnp.exp(b) + lam_init


def _attn_kernel(qi_ref, kj_ref, ty_ref, fin_ref, q_ref, k_ref, vt_ref, bias_ref, lam_ref, ghn_ref, o_ref,
                 qm_sc, m_sc, l_sc, acc_sc, *, n_types, lam_init):
    step = pl.program_id(2)
    ty = ty_ref[step]

    @pl.when(kj_ref[step] == 0)
    def _():
        q = q_ref[0]
        lane = lax.broadcasted_iota(jnp.int32, q.shape, 1)
        zero = jnp.zeros_like(q)
        qm_sc[0] = jnp.where(lane < B_DK, q, zero)
        qm_sc[1] = jnp.where(lane >= B_DK, q, zero)
        m_sc[...] = jnp.full_like(m_sc, -jnp.inf)
        l_sc[...] = jnp.zeros_like(l_sc)
        acc_sc[...] = jnp.zeros_like(acc_sc)

    def update(adj):
        k = k_ref[0]
        vt = vt_ref[0]
        for c in range(2):
            s = _nt_dot(k, qm_sc[c])
            if adj is not None:
                s = s + adj
            m_old = m_sc[c]
            m_new = jnp.maximum(m_old, jnp.max(s, axis=0, keepdims=True))
            alpha = jnp.exp(m_old - m_new)
            p = jnp.exp(s - m_new)
            l_sc[c] = alpha * l_sc[c] + jnp.sum(p, axis=0, keepdims=True)
            acc_sc[c] = alpha * acc_sc[c] + jnp.dot(vt, p.astype(BF16), preferred_element_type=F32)
            m_sc[c] = m_new

    @pl.when(ty < 0)
    def _():
        update(None)

    for t in range(n_types):
        @pl.when(ty == t)
        def _(t=t):
            update(bias_ref[0, t])

    @pl.when(fin_ref[step] == 1)
    def _():
        lam = _lambda(lam_ref, lam_init)
        ot = acc_sc[0] / l_sc[0] - lam * (acc_sc[1] / l_sc[1])
        on = ot * lax.rsqrt(jnp.mean(ot * ot, axis=0, keepdims=True) + EPS) * ghn_ref[...] * (1.0 - lam_init)
        o_ref[0] = on.T.astype(o_ref.dtype)


def _attn_schedule(T, tq, tk):
    offsets = sorted({qi * tq - kj * tk for qi in range(T // tq) for kj in range(T // tk)
                      if qi * tq + tq - 1 >= kj * tk and qi * tq - kj * tk - (tk - 1) < RPB_MAX_DIST})
    qi_l, kj_l, ty_l, fin_l = [], [], [], []
    for qi in range(T // tq):
        kjs = [kj for kj in range(T // tk) if qi * tq + tq - 1 >= kj * tk]
        for kj in kjs:
            off = qi * tq - kj * tk
            qi_l.append(qi)
            kj_l.append(kj)
            ty_l.append(offsets.index(off) if off in offsets else -1)
            fin_l.append(int(kj == kjs[-1]))
    as_i32 = lambda v: jnp.asarray(np.asarray(v, np.int32))
    return offsets, as_i32(qi_l), as_i32(kj_l), as_i32(ty_l), as_i32(fin_l)


def _bias_table(rpb):
    n = jnp.arange(RPB_MAX_DIST, dtype=jnp.int32)
    max_exact = RPB_BUCKETS // 2
    nf = jnp.maximum(n, 1).astype(F32)
    large = max_exact + (jnp.log(nf / max_exact) / math.log(RPB_MAX_DIST / max_exact)
                         * (RPB_BUCKETS - max_exact)).astype(jnp.int32)
    bucket = jnp.where(n < max_exact, n, jnp.minimum(large, RPB_BUCKETS - 1))
    return (rpb[bucket] - rpb[RPB_BUCKETS - 1][None, :]).T.astype(F32)


def _bias_of_distance(tbl, dist):
    d = np.asarray(dist)
    idx = jnp.asarray(np.clip(d, 0, RPB_MAX_DIST - 1).astype(np.int32))
    vals = jnp.take(tbl, idx, axis=1)
    vals = jnp.where(jnp.asarray(d >= RPB_MAX_DIST), 0.0, vals)
    return jnp.where(jnp.asarray(d < 0), -jnp.inf, vals)


def _bias_tiles_t(tbl, offsets, tq, tk):
    period = tq + tk
    w = np.arange(period)
    u = np.where(w < tq, w, w - period)
    vext = _bias_of_distance(tbl, np.stack([off + u for off in offsets]))
    H, n_types = vext.shape[:2]
    flat = jnp.broadcast_to(vext[:, :, None, :], (H, n_types, tk, period)).reshape(H, n_types, tk * period)
    return flat[:, :, :tk * (period - 1)].reshape(H, n_types, tk, period - 1)[:, :, :, :tq]


def _attn_prompt(q16, k16, vt16, tbl, lam, ghn_col, lam_init, tq, tk):
    B, T, W = q16.shape
    H = W // LANES
    offsets, qi, kj, ty, fin = _attn_schedule(T, tq, tk)
    bias = _bias_tiles_t(tbl, offsets, tq, tk)
    kern = functools.partial(_attn_kernel, n_types=len(offsets), lam_init=lam_init)
    grid_spec = pltpu.PrefetchScalarGridSpec(
        num_scalar_prefetch=4,
        grid=(H, B, int(qi.shape[0])),
        in_specs=[pl.BlockSpec((1, tq, LANES), lambda h, b, s, qi, kj, ty, fin: (b, qi[s], h)),
                  pl.BlockSpec((1, tk, LANES), lambda h, b, s, qi, kj, ty, fin: (b, kj[s], h)),
                  pl.BlockSpec((1, LANES, tk), lambda h, b, s, qi, kj, ty, fin: (b, h, kj[s])),
                  pl.BlockSpec((1, len(offsets), tk, tq), lambda h, b, s, *_: (h, 0, 0, 0)),
                  pl.BlockSpec(lam.shape, lambda h, b, s, *_: (0, 0)),
                  pl.BlockSpec((LANES, 1), lambda h, b, s, *_: (h, 0))],
        out_specs=pl.BlockSpec((1, tq, LANES), lambda h, b, s, qi, kj, ty, fin: (b, qi[s], h)),
        scratch_shapes=[pltpu.VMEM((2, tq, LANES), BF16), pltpu.VMEM((2, 1, tq), F32),
                        pltpu.VMEM((2, 1, tq), F32), pltpu.VMEM((2, LANES, tq), F32)],
    )
    return pl.pallas_call(
        kern,
        grid_spec=grid_spec,
        out_shape=jax.ShapeDtypeStruct((B, T, W), BF16),
        compiler_params=_params("parallel", "parallel", "arbitrary"),
        name="attn_prompt",
    )(qi, kj, ty, fin, q16, k16, vt16, bias, lam, ghn_col)


PAGES_PER_STEP = 2


def _attn_paged_kernel(pt_ref, q_ref, *refs, heads, t_new, page, n_steps, lam_init):
    pps = PAGES_PER_STEP
    kc_refs, vc_refs = refs[:pps], refs[pps:2 * pps]
    kn_ref, vn_ref, blast_ref, bnew_ref, lam_ref, ghn_ref, o_ref, qa_sc, qb_sc, m_sc, l_sc, acc_sc = refs[2 * pps:]
    j = pl.program_id(1)
    R = SUBLANES
    hsl = lambda h: slice(h * R, (h + 1) * R)

    @pl.when(j == 0)
    def _():
        q = q_ref[0]
        first = lax.broadcasted_iota(jnp.int32, (R, B_DK), 0) < t_new
        for h in range(heads):
            qa_sc[hsl(h), :] = jnp.where(first, q[:, h * LANES:h * LANES + B_DK], 0.0)
            qb_sc[hsl(h), :] = jnp.where(first, 0.0, q[:, h * LANES + B_DK:(h + 1) * LANES])
        m_sc[...] = jnp.full_like(m_sc, -jnp.inf)
        l_sc[...] = jnp.zeros_like(l_sc)
        acc_sc[...] = jnp.zeros_like(acc_sc)

    qa = qa_sc[...].astype(BF16)
    qb = qb_sc[...].astype(BF16)

    def update(k_of, v_of, adj):
        s = jnp.concatenate([_nt_dot(qa[hsl(h)], k_of(h, 0)) + _nt_dot(qb[hsl(h)], k_of(h, 1))
                             for h in range(heads)], axis=0)
        if adj is not None:
            s = s + adj
        m_old = m_sc[...]
        m_new = jnp.maximum(m_old, jnp.max(s, axis=1, keepdims=True))
        alpha = jnp.exp(m_old - m_new)
        p = jnp.exp(s - m_new)
        l_sc[...] = alpha * l_sc[...] + jnp.sum(p, axis=1, keepdims=True)
        pb = p.astype(BF16)
        pv = jnp.concatenate([jnp.dot(pb[hsl(h)], v_of(h), preferred_element_type=F32) for h in range(heads)], axis=0)
        acc_sc[...] = alpha * acc_sc[...] + pv
        m_sc[...] = m_new

    for u in range(pps):
        kc, vc = kc_refs[u], vc_refs[u]
        k_of = lambda h, c, kc=kc: kc[0, pl.ds(2 * h + c, page, stride=2 * heads), :].astype(BF16)
        v_of = lambda h, vc=vc: vc[0, pl.ds(h, page, stride=heads), :].astype(BF16)
        adj = jnp.where(j == n_steps - 1, blast_ref[...], 0.0) if u == pps - 1 else None
        update(k_of, v_of, adj)

    @pl.when(j == n_steps - 1)
    def _():
        kn = kn_ref[0].astype(BF16)
        vn = vn_ref[0].astype(BF16)
        update(lambda h, c: kn[:, h * LANES + c * B_DK:h * LANES + (c + 1) * B_DK],
               lambda h: vn[:, h * LANES:(h + 1) * LANES], bnew_ref[...])
        lam = _lambda(lam_ref, lam_init)
        full = acc_sc[...] / l_sc[...]
        for h in range(heads):
            fh = full[hsl(h)]
            o = fh - lam * pltpu.roll(fh, R - t_new, 0)
            on = o * lax.rsqrt(jnp.mean(o * o, axis=1, keepdims=True) + EPS)
            o_ref[0, :, h * LANES:(h + 1) * LANES] = on * ghn_ref[:, h * LANES:(h + 1) * LANES] * (1.0 - lam_init)


def _attn_paged(q, cache_k, cache_v, page_table, k_new, v_new, tbl, lam, ghn, lam_init):
    B, t_new, W = q.shape
    H = W // LANES
    n_pool, page = cache_k.shape[:2]
    n_pages = page_table.shape[1]
    past = n_pages * page
    R = SUBLANES
    pps = PAGES_PER_STEP
    assert 2 * t_new == R and page >= RPB_MAX_DIST and n_pages % pps == 0
    n_steps = n_pages // pps
    pad = lambda a: jnp.concatenate([a, jnp.zeros((B, R - t_new, W), a.dtype)], axis=1)
    q8 = jnp.concatenate([q, q], axis=1)
    t = np.arange(R)[:, None] % t_new
    d_last = past + t - ((n_pages - 1) * page + np.arange(page)[None, :])
    c = np.arange(R)[None, :]
    d_new = np.where(c < t_new, t - c, -1)
    flat = lambda b: b.reshape(H * R, b.shape[-1])
    bias_last, bias_new = flat(_bias_of_distance(tbl, d_last)), flat(_bias_of_distance(tbl, d_new))
    kern = functools.partial(_attn_paged_kernel, heads=H, t_new=t_new, page=page, n_steps=n_steps, lam_init=lam_init)
    page_spec = lambda rows, width, u: pl.BlockSpec(
        (1, rows, width), lambda b, j, pt: (pt[b * n_pages + j * pps + u], 0, 0))
    per_b = pl.BlockSpec((1, R, W), lambda b, j, pt: (b, 0, 0))
    full = lambda a: pl.BlockSpec(a.shape, lambda b, j, pt: (0,) * a.ndim)
    grid_spec = pltpu.PrefetchScalarGridSpec(
        num_scalar_prefetch=1,
        grid=(B, n_steps),
        in_specs=[per_b] + [page_spec(page * H * 2, B_DK, u) for u in range(pps)]
                 + [page_spec(page * H, B_DV, u) for u in range(pps)]
                 + [per_b, per_b, full(bias_last), full(bias_new), full(lam), full(ghn)],
        out_specs=per_b,
        scratch_shapes=[pltpu.VMEM((H * R, B_DK), F32), pltpu.VMEM((H * R, B_DK), F32),
                        pltpu.VMEM((H * R, 1), F32), pltpu.VMEM((H * R, 1), F32), pltpu.VMEM((H * R, B_DV), F32)],
    )
    ck = cache_k.reshape(n_pool, page * H * 2, B_DK)
    cv = cache_v.reshape(n_pool, page * H, B_DV)
    out = pl.pallas_call(
        kern,
        grid_spec=grid_spec,
        out_shape=jax.ShapeDtypeStruct((B, R, W), F32),
        compiler_params=_params("parallel", "arbitrary"),
        name="attn_paged",
    )(page_table.reshape(-1), q8, *([ck] * pps), *([cv] * pps), pad(k_new), pad(v_new),
      bias_last, bias_new, lam, ghn)
    return out[:, :t_new]


def _trunk(x, mods, mods_kv, state, past, wts):
    B, T, D = x.shape
    N = B * T
    tm = ROW_TILE
    mod = lambda m: _Mod(m, T, tm)
    split3 = lambda m: (mod(m[:, :D]), mod(m[:, D:2 * D]), mod(m[:, 2 * D:]))
    x2 = x.reshape(N, D)

    heads = A_HEADS
    inner = wts["w_out16"].shape[0]
    dh = inner // heads
    shift, scale, gate = split3(mods[0])
    proj, gates = _inproj(x2, shift, scale, wts["g_norm"][0, 0][None], wts["w_in16"], wts["w_gate"], tm)
    proj = proj.reshape(B, T, 4 * inner)
    gates = gates.reshape(B, T, LANES)
    conv_new = proj[:, T - (A_CONV - 1):, :2 * inner]
    if state is None:
        L, t_valid = math.gcd(T, MLSTM_CHUNK), None
        conv_init = jnp.zeros((B, SUBLANES, 2 * inner), F32)
        c0 = jnp.zeros((B, heads, dh, dh), F32)
        n0 = jnp.zeros((B, heads, 1, dh), F32)
        m0 = jnp.zeros((B, heads, 1, 1), F32)
    else:
        conv_st, c_st, n_st, m_st = state
        assert T <= SUBLANES
        L, t_valid = SUBLANES, T
        rows = lambda a: jnp.concatenate([a, jnp.zeros((B, L - T, a.shape[2]), a.dtype)], axis=1)
        proj, gates = rows(proj), rows(gates)
        conv_init = jnp.concatenate([jnp.zeros((B, SUBLANES - (A_CONV - 1), 2 * inner), F32), conv_st], axis=1)
        c0, n0, m0 = c_st, n_st[:, :, None, :], m_st[:, :, None, None]
    gates_t = jnp.swapaxes(gates[:, :, :SUBLANES], 1, 2)
    hs, c1, n1, m1 = _mlstm(proj, gates, gates_t, wts["bg"], wts["bgt"], conv_init, c0, n0, m0,
                            wts["w_conv"], wts["b_conv"], wts["g_hn_a"], L, t_valid)
    hs = hs[:, :T].reshape(N, inner)
    new_state = (conv_new, c1, n1[:, :, 0, :], m1[:, :, 0, 0])

    shift2, scale2, gate2 = split3(mods[1])
    x2, h16, rg = _proj_router(hs, wts["w_out16"], x2, gate, shift2, scale2, wts["g_norm"][0, 1][None],
                               wts["w_router"], wts["b_router"], tm)
    x2 = _moe(h16, rg, x2, gate2, wts["wg16"][0], wts["wu16"][0], wts["wd16"][0], tm)

    shift_kv, scale_kv = mod(mods_kv[:, :D]), mod(mods_kv[:, D:])
    shift, scale, gate = split3(mods[2])
    kvq = _kvq(x2, shift_kv, scale_kv, wts["g_kv"], shift, scale, wts["g_norm"][1, 0][None],
               wts["w_kv16"], wts["w_q16"], wts["gmat"], wts["gmat_t"], wts["g_kn"], wts["g_qn"], tm, T)
    k32, v32, q = kvq[:3]
    W = k32.shape[1]
    lam_init = 0.8 - 0.6 * math.exp(-0.3 * 1)
    if past is None:
        k16, vt16 = kvq[3:]
        o = _attn_prompt(q.reshape(B, T, W), k16.reshape(B, T, W), vt16, wts["rpb_tbl"], wts["lam"],
                         wts["g_hn_b"].reshape(W, 1), lam_init, math.gcd(T, ATTN_TQ), math.gcd(T, ATTN_TK))
        o = o.reshape(N, W)
    else:
        cache_k, cache_v, page_table = past
        o = _attn_paged(q.reshape(B, T, W), cache_k, cache_v, page_table, k32.reshape(B, T, W),
                        v32.reshape(B, T, W), wts["rpb_tbl"], wts["lam"], wts["g_hn_b"], lam_init)
        o = o.reshape(N, W).astype(BF16)

    shift2, scale2, gate2 = split3(mods[3])
    x2, h16, rg = _proj_router(o, wts["w_o16"], x2, gate, shift2, scale2, wts["g_norm"][1, 1][None],
                               wts["w_router"], wts["b_router"], tm)
    x2 = _moe(h16, rg, x2, gate2, wts["wg16"][1], wts["wu16"][1], wts["wd16"][1], tm)

    H = W // LANES
    return (x2.reshape(B, T, D), new_state,
            k32.reshape(B, T, H, 2, B_DK), v32.reshape(B, T, H, B_DV))


def kernel(x_prompt, x_sample, c_prompt, c_sample, state_conv, state_C, state_n, state_m, cache_k, cache_v, page_table, w_ada, b_ada, g_norm, w_in_a, b_gate_a, w_conv_a, b_conv_a, g_hn_a, w_out_a, g_kv, w_ada_kv, b_ada_kv, w_kv, g_kn, w_q_b, g_qn_b, lam_b, g_hn_b, w_o_b, rpb, w_router, b_router, w_gate_e, w_up_e, w_down_e):
    Bp, Tp, D = x_prompt.shape
    Bs = x_sample.shape[0]
    inner = w_out_a.shape[1]
    heads_b = g_hn_b.shape[1]
    W = heads_b * B_DV

    n_c = Bp + Bs
    c_all = jnp.concatenate([c_prompt, c_sample, jnp.zeros((-n_c % SUBLANES, D), F32)], axis=0)
    mods = _ada(c_all, w_ada.reshape(-1, D, 3 * D), b_ada.reshape(-1, 1, 3 * D))
    mods_kv = _ada(c_all, w_ada_kv[None], b_ada_kv[None, None])[0]

    n_gate = 2 * A_HEADS
    group_of_lane = np.arange(W) // B_DK
    gmat = jnp.asarray((group_of_lane[:, None] == np.arange(LANES)[None, :]).astype(np.float32)).astype(BF16)
    wts = {
        "g_norm": g_norm,
        "w_in16": w_in_a[0][:, :4 * inner].astype(BF16),
        "w_gate": jnp.pad(w_in_a[0][:, 4 * inner:], ((0, 0), (0, LANES - n_gate))),
        "bg": jnp.pad(b_gate_a[0], (0, LANES - n_gate))[None, :],
        "bgt": b_gate_a[0][:, None],
        "w_conv": w_conv_a[0], "b_conv": b_conv_a[0][None, :],
        "g_hn_a": g_hn_a[0].reshape(1, inner),
        "w_out16": w_out_a[0].astype(BF16),
        "g_kv": g_kv[None, :],
        "w_kv16": w_kv.astype(BF16), "w_q16": w_q_b[0].astype(BF16),
        "gmat": gmat, "gmat_t": gmat.T,
        "g_kn": jnp.tile(g_kn.reshape(-1), heads_b)[None, :],
        "g_qn": jnp.tile(g_qn_b[0].reshape(-1), heads_b)[None, :],
        "lam": lam_b[0], "g_hn_b": g_hn_b[0].reshape(1, W),
        "w_o16": w_o_b[0].astype(BF16),
        "rpb_tbl": _bias_table(rpb),
        "w_router": jnp.pad(w_router, ((0, 0), (0, LANES - N_EXPERTS))),
        "b_router": b_router[:, None],
        "wg16": w_gate_e.astype(BF16), "wu16": w_up_e.astype(BF16), "wd16": w_down_e.astype(BF16),
    }

    y_p, st_p, k_p, v_p = _trunk(x_prompt, mods[:, :Bp], mods_kv[:Bp], None, None, wts)
    y_s, st_s, k_s, v_s = _trunk(x_sample, mods[:, Bp:n_c], mods_kv[Bp:n_c],
                                 (state_conv[0], state_C[0], state_n[0], state_m[0]),
                                 (cache_k, cache_v, page_table), wts)
    stack = lambda st: tuple(a[None] for a in st)
    return (y_p, y_s) + stack(st_p) + (k_p, v_p) + stack(st_s) + (k_s, v_s)
```

```python
import functools
import math

import numpy as np
import jax
import jax.numpy as jnp
from jax import lax
from jax.experimental import pallas as pl
from jax.experimental.pallas import tpu as pltpu

F32, BF16 = jnp.float32, jnp.bfloat16
HIGHEST = lax.Precision.HIGHEST
EPS = 1e-6

A_HEADS = 4
A_CONV = 4
B_DK = 64
B_DV = 128
N_EXPERTS = 16
N_GROUPS = 4
RPB_BUCKETS = 32
RPB_MAX_DIST = 128

LANES = 128
SUBLANES = 8
VMEM_LIMIT_BYTES = 56 * 1024 * 1024

ROW_TILE = 512
MLSTM_CHUNK = 256
ATTN_TQ = 512
ATTN_TK = 512


def _params(*sem):
    return pltpu.CompilerParams(dimension_semantics=sem, vmem_limit_bytes=VMEM_LIMIT_BYTES)


def _nt_dot(a, b):
    return lax.dot_general(a, b, (((1,), (1,)), ((), ())), preferred_element_type=F32)


def _tn_dot(a, b):
    return lax.dot_general(a, b, (((0,), (0,)), ((), ())), preferred_element_type=F32)


def _silu(x):
    return x * jax.nn.sigmoid(x)


def _rms_mod(x, g, scale, shift):
    y = x * lax.rsqrt(jnp.mean(x * x, axis=-1, keepdims=True) + EPS)
    return (y * g) * (1.0 + scale) + shift


def _dot_split(a, b16):
    hi = a.astype(BF16)
    lo = (a - hi.astype(F32)).astype(BF16)
    return (jnp.dot(hi, b16, preferred_element_type=F32) + jnp.dot(lo, b16, preferred_element_type=F32))


def _group_rms(x, gmat, gmat_t, g, group):
    ss = _dot_split(x * x, gmat)
    r = lax.rsqrt(ss * (1.0 / group) + EPS)
    rf = _dot_split(r, gmat_t)
    return x * rf * g


def _ada_kernel(c_ref, w_ref, b_ref, o_ref):
    a = _silu(c_ref[...])
    o_ref[0] = jnp.dot(a, w_ref[0], precision=HIGHEST, preferred_element_type=F32) + b_ref[0]


def _ada(c_all, w, b):
    S, D, Fo = w.shape
    R = c_all.shape[0]
    tn = 1024
    return pl.pallas_call(
        _ada_kernel,
        grid=(S, Fo // tn),
        in_specs=[pl.BlockSpec((R, D), lambda s, j: (0, 0)),
                  pl.BlockSpec((1, D, tn), lambda s, j: (s, 0, j)),
                  pl.BlockSpec((1, 1, tn), lambda s, j: (s, 0, j))],
        out_specs=pl.BlockSpec((1, R, tn), lambda s, j: (s, 0, j)),
        out_shape=jax.ShapeDtypeStruct((S, R, Fo), F32),
        compiler_params=_params("parallel", "parallel"),
        name="ada",
    )(c_all, w, b)


class _Mod:
    def __init__(self, m, T, tm):
        B, D = m.shape
        if T % tm == 0:
            self.arr, self.tiles_per_group = m[:, None, :], T // tm
        else:
            assert (B * T) % tm == 0
            self.arr, self.tiles_per_group = jnp.repeat(m, T, axis=0).reshape(-1, tm, D), 1

    def spec(self, grid_rank):
        R, D = self.arr.shape[1:]
        tpg = self.tiles_per_group
        if grid_rank == 1:
            return pl.BlockSpec((1, R, D), lambda i: (i // tpg, 0, 0))
        return pl.BlockSpec((1, R, D), lambda i, j: (i // tpg, 0, 0))


def _inproj_kernel(x_ref, sh_ref, sc_ref, g_ref, w_ref, wg_ref, o_ref, og_ref, h_sc):
    @pl.when(pl.program_id(1) == 0)
    def _():
        h = _rms_mod(x_ref[...], g_ref[...], sc_ref[0], sh_ref[0])
        h_sc[...] = h.astype(BF16)
        og_ref[...] = jnp.dot(h, wg_ref[...], precision=HIGHEST, preferred_element_type=F32)

    o_ref[...] = jnp.dot(h_sc[...], w_ref[...], preferred_element_type=F32)


def _inproj(x2, shift, scale, g, w16, wgate, tm):
    N, D = x2.shape
    Fo = w16.shape[1]
    tn = 2048
    return pl.pallas_call(
        _inproj_kernel,
        grid=(N // tm, Fo // tn),
        in_specs=[pl.BlockSpec((tm, D), lambda i, j: (i, 0)),
                  shift.spec(2), scale.spec(2),
                  pl.BlockSpec((1, D), lambda i, j: (0, 0)),
                  pl.BlockSpec((D, tn), lambda i, j: (0, j)),
                  pl.BlockSpec((D, LANES), lambda i, j: (0, 0))],
        out_specs=[pl.BlockSpec((tm, tn), lambda i, j: (i, j)),
                   pl.BlockSpec((tm, LANES), lambda i, j: (i, 0))],
        out_shape=[jax.ShapeDtypeStruct((N, Fo), F32), jax.ShapeDtypeStruct((N, LANES), F32)],
        scratch_shapes=[pltpu.VMEM((tm, D), BF16)],
        compiler_params=_params("parallel", "arbitrary"),
        name="mlstm_inproj",
    )(x2, shift.arr, scale.arr, g, w16, wgate)


def _mlstm_kernel(q_ref, k_ref, v_ref, o_ref, gt_ref, gtt_ref, bg_ref, bgt_ref, cinit_ref,
                  c0_ref, n0_ref, m0_ref, wconv_ref, bconv_ref, ghn_ref,
                  hs_ref, c_ref, n_ref, m_ref, tail_sc, *, L, dh, heads, t_valid):
    inner = heads * dh

    @pl.when(pl.program_id(1) == 0)
    def _():
        c_ref[...] = c0_ref[...]
        n_ref[...] = n0_ref[...]
        m_ref[...] = m0_ref[...]
        tail_sc[...] = cinit_ref[0]

    row8 = lax.broadcasted_iota(jnp.int32, (SUBLANES, dh), 0)

    def conv(x, tail, w, b):
        acc = b + x * w[A_CONV - 1:A_CONV]
        for s in range(1, A_CONV):
            xs = pltpu.roll(x, s, 0)
            top = jnp.where(row8 < s, pltpu.roll(tail, s, 0), xs[:SUBLANES])
            xs = top if L == SUBLANES else jnp.concatenate([top, xs[SUBLANES:]], axis=0)
            acc = acc + xs * w[A_CONV - 1 - s:A_CONV - s]
        return acc

    gt = gt_ref[0] + bg_ref[...]
    gtt = gtt_ref[0] + bgt_ref[...]
    ti = lax.broadcasted_iota(jnp.int32, (L, L), 0)
    si = lax.broadcasted_iota(jnp.int32, (L, L), 1)
    causal = si <= ti
    tcol = lax.broadcasted_iota(jnp.int32, (L, 1), 0)
    trow = lax.broadcasted_iota(jnp.int32, (1, L), 1)

    for h in range(heads):
        sl = slice(h * dh, (h + 1) * dh)
        slk = slice(inner + h * dh, inner + (h + 1) * dh)
        qh = _silu(conv(q_ref[0, :, sl], tail_sc[:, sl], wconv_ref[:, sl], bconv_ref[:, sl]))
        kh = _silu(conv(k_ref[0, :, sl], tail_sc[:, slk], wconv_ref[:, slk], bconv_ref[:, slk])) * (dh ** -0.5)
        vb = v_ref[0, :, sl].astype(BF16)

        ig_col = gt[:, h:h + 1]
        lf_col = jax.nn.log_sigmoid(gt[:, heads + h:heads + h + 1])
        ig_row = gtt[h:h + 1, :]
        lf_row = jax.nn.log_sigmoid(gtt[heads + h:heads + h + 1, :])
        if t_valid is not None:
            ig_col = jnp.where(tcol < t_valid, ig_col, -jnp.inf)
            lf_col = jnp.where(tcol < t_valid, lf_col, 0.0)
            ig_row = jnp.where(trow < t_valid, ig_row, -jnp.inf)
            lf_row = jnp.where(trow < t_valid, lf_row, 0.0)

        b_col = jnp.sum(jnp.where(causal, lf_row, 0.0), axis=1, keepdims=True)
        b_row = jnp.sum(jnp.where(ti <= si, lf_col, 0.0), axis=0, keepdims=True)
        dlog = jnp.where(causal, b_col - b_row + ig_row, -jnp.inf)
        g_col = b_col + m_ref[0, h]
        m_col = jnp.maximum(g_col, jnp.max(dlog, axis=1, keepdims=True))
        w_intra = jnp.exp(dlog - m_col)
        w_inter = jnp.exp(g_col - m_col)

        qb = qh.astype(BF16)
        kb = kh.astype(BF16)
        s = w_intra * _nt_dot(qb, kb)
        ch = c_ref[0, h]
        nh = n_ref[0, h]
        num = w_inter * _nt_dot(qb, ch.astype(BF16)) + jnp.dot(s.astype(BF16), vb, preferred_element_type=F32)
        den = w_inter * jnp.sum(qh * nh, axis=1, keepdims=True) + jnp.sum(s, axis=1, keepdims=True)
        hv = num / jnp.maximum(jnp.abs(den), jnp.exp(-m_col))

        m_end = m_col[L - 1:L]
        we_inter = jnp.exp(g_col[L - 1:L] - m_end)
        we_col = jnp.exp(b_col[L - 1:L] - b_col + ig_col - m_end)
        kw = kh * we_col
        c_ref[0, h] = we_inter * ch + _tn_dot(vb, kw.astype(BF16))
        n_ref[0, h] = we_inter * nh + jnp.sum(kw, axis=0, keepdims=True)
        m_ref[0, h] = m_end

        hn = hv * lax.rsqrt(jnp.mean(hv * hv, axis=1, keepdims=True) + EPS) * ghn_ref[:, sl]
        hs_ref[0, :, sl] = (jax.nn.sigmoid(o_ref[0, :, sl]) * hn).astype(hs_ref.dtype)

    tail_sc[:, :inner] = q_ref[0, L - SUBLANES:, :]
    tail_sc[:, inner:] = k_ref[0, L - SUBLANES:, :]


def _mlstm(proj, gates, gates_t, bg, bgt, conv_init, c0, n0, m0, wconv, bconv, ghn, L, t_valid):
    B, Tp, _ = proj.shape
    heads, dh = c0.shape[1], c0.shape[2]
    inner = heads * dh
    nc = Tp // L
    kern = functools.partial(_mlstm_kernel, L=L, dh=dh, heads=heads, t_valid=t_valid)
    col = lambda j: pl.BlockSpec((1, L, inner), lambda b, c: (b, c, j))
    full = lambda shape: pl.BlockSpec(shape, lambda b, c: (0,) * len(shape))
    per_b = lambda shape: pl.BlockSpec((1,) + shape, lambda b, c: (b,) + (0,) * len(shape))
    return pl.pallas_call(
        kern,
        grid=(B, nc),
        in_specs=[col(0), col(1), col(2), col(3),
                  pl.BlockSpec((1, L, LANES), lambda b, c: (b, c, 0)),
                  pl.BlockSpec((1, SUBLANES, L), lambda b, c: (b, 0, c)),
                  full((1, LANES)), full((SUBLANES, 1)),
                  per_b((SUBLANES, 2 * inner)),
                  per_b((heads, dh, dh)), per_b((heads, 1, dh)), per_b((heads, 1, 1)),
                  full((A_CONV, 2 * inner)), full((1, 2 * inner)), full((1, inner))],
        out_specs=[pl.BlockSpec((1, L, inner), lambda b, c: (b, c, 0)),
                   per_b((heads, dh, dh)), per_b((heads, 1, dh)), per_b((heads, 1, 1))],
        out_shape=[jax.ShapeDtypeStruct((B, Tp, inner), BF16),
                   jax.ShapeDtypeStruct((B, heads, dh, dh), F32),
                   jax.ShapeDtypeStruct((B, heads, 1, dh), F32),
                   jax.ShapeDtypeStruct((B, heads, 1, 1), F32)],
        scratch_shapes=[pltpu.VMEM((SUBLANES, 2 * inner), F32)],
        compiler_params=_params("parallel", "arbitrary"),
        name="mlstm",
    )(proj, proj, proj, proj, gates, gates_t, bg, bgt, conv_init, c0, n0, m0, wconv, bconv, ghn)


def _route(h, wr, br):
    tm = h.shape[0]
    per = N_EXPERTS // N_GROUPS
    logits = jnp.dot(h, wr, precision=HIGHEST, preferred_element_type=F32)
    lt = logits.T[:N_EXPERTS]
    s = jax.nn.sigmoid(lt)
    sel = s + br
    neg = jnp.full((1, tm), -jnp.inf, F32)
    izero = jnp.zeros((1, tm), jnp.int32)

    best_score = best_e1 = best_e2 = best_w1 = best_w2 = None
    for grp in range(N_GROUPS):
        rows = [sel[grp * per + j:grp * per + j + 1] for j in range(per)]
        srow = [s[grp * per + j:grp * per + j + 1] for j in range(per)]
        t1, i1, w1 = rows[0], izero, srow[0]
        for j in range(1, per):
            better = rows[j] > t1
            t1 = jnp.where(better, rows[j], t1)
            i1 = jnp.where(better, j, i1)
            w1 = jnp.where(better, srow[j], w1)
        t2, i2, w2 = neg, izero, srow[0]
        for j in range(per):
            better = jnp.where(i1 == j, neg, rows[j]) > t2
            t2 = jnp.where(better, rows[j], t2)
            i2 = jnp.where(better, j, i2)
            w2 = jnp.where(better, srow[j], w2)
        score = t1 + t2
        e1, e2 = i1 + grp * per, i2 + grp * per
        if grp == 0:
            best_score, best_e1, best_e2, best_w1, best_w2 = score, e1, e2, w1, w2
        else:
            better = score > best_score
            best_score = jnp.where(better, score, best_score)
            best_e1 = jnp.where(better, e1, best_e1)
            best_e2 = jnp.where(better, e2, best_e2)
            best_w1 = jnp.where(better, w1, best_w1)
            best_w2 = jnp.where(better, w2, best_w2)
    tot = best_w1 + best_w2
    eid = lax.broadcasted_iota(jnp.int32, (LANES, tm), 0)
    gt = jnp.where(eid == best_e1, best_w1 / tot, 0.0) + jnp.where(eid == best_e2, best_w2 / tot, 0.0)
    return gt.T


def _proj_router_kernel(a_ref, w_ref, x_ref, gate_ref, sh_ref, sc_ref, g_ref, wr_ref, br_ref,
                        xo_ref, h_ref, gates_ref):
    mix = jnp.dot(a_ref[...], w_ref[...], preferred_element_type=F32)
    x = x_ref[...] + gate_ref[0] * mix
    xo_ref[...] = x
    h = _rms_mod(x, g_ref[...], sc_ref[0], sh_ref[0])
    h_ref[...] = h.astype(BF16)
    gates_ref[...] = _route(h, wr_ref[...], br_ref[...])


def _proj_router(a16, w16, x2, gate, shift, scale, g, wr, br, tm):
    N, D = x2.shape
    K = a16.shape[1]
    row = lambda w: pl.BlockSpec((tm, w), lambda i: (i, 0))
    full = lambda shape: pl.BlockSpec(shape, lambda i: (0,) * len(shape))
    return pl.pallas_call(
        _proj_router_kernel,
        grid=(N // tm,),
        in_specs=[row(K), full((K, D)), row(D), gate.spec(1), shift.spec(1), scale.spec(1),
                  full((1, D)), full((D, LANES)), full((N_EXPERTS, 1))],
        out_specs=[row(D), row(D), row(LANES)],
        out_shape=[jax.ShapeDtypeStruct((N, D), F32), jax.ShapeDtypeStruct((N, D), BF16),
                   jax.ShapeDtypeStruct((N, LANES), F32)],
        compiler_params=_params("parallel"),
        name="proj_router",
    )(a16, w16, x2, gate.arr, shift.arr, scale.arr, g, wr, br)


def _moe_kernel(h_ref, gates_ref, x_ref, gate_ref, wg_ref, wu_ref, wd_ref, o_ref, acc_sc):
    e = pl.program_id(1)

    @pl.when(e == 0)
    def _():
        acc_sc[...] = jnp.zeros_like(acc_sc)

    h = h_ref[...]
    a = jnp.dot(h, wg_ref[0], preferred_element_type=F32)
    u = jnp.dot(h, wu_ref[0], preferred_element_type=F32)
    gates = gates_ref[...]
    lane = lax.broadcasted_iota(jnp.int32, gates.shape, 1)
    gcol = jnp.sum(jnp.where(lane == e, gates, 0.0), axis=1, keepdims=True)
    act = _silu(a) * u * gcol
    acc_sc[...] += jnp.dot(act.astype(BF16), wd_ref[0], preferred_element_type=F32)

    @pl.when(e == pl.num_programs(1) - 1)
    def _():
        o_ref[...] = x_ref[...] + gate_ref[0] * acc_sc[...]


def _moe(h16, gates, x2, gate, wg16, wu16, wd16, tm):
    N, D = x2.shape
    E, _, Fe = wg16.shape
    row = lambda w: pl.BlockSpec((tm, w), lambda i, e: (i, 0))
    return pl.pallas_call(
        _moe_kernel,
        grid=(N // tm, E),
        in_specs=[row(D), row(LANES), row(D), gate.spec(2),
                  pl.BlockSpec((1, D, Fe), lambda i, e: (e, 0, 0)),
                  pl.BlockSpec((1, D, Fe), lambda i, e: (e, 0, 0)),
                  pl.BlockSpec((1, Fe, D), lambda i, e: (e, 0, 0))],
        out_specs=row(D),
        out_shape=jax.ShapeDtypeStruct((N, D), F32),
        scratch_shapes=[pltpu.VMEM((tm, D), F32)],
        compiler_params=_params("parallel", "arbitrary"),
        name="moe",
    )(h16, gates, x2, gate.arr, wg16, wu16, wd16)


def _kvq_kernel(x_ref, shk_ref, sck_ref, gk_ref, shq_ref, scq_ref, gq_ref, wkv_ref, wq_ref,
                gmat_ref, gmatt_ref, gkn_ref, gqn_ref, k_ref, v32_ref, q_ref, *maybe_attn_refs):
    x = x_ref[...]
    y = x * lax.rsqrt(jnp.mean(x * x, axis=-1, keepdims=True) + EPS)
    hk = ((y * gk_ref[...]) * (1.0 + sck_ref[0]) + shk_ref[0]).astype(BF16)
    hq = ((y * gq_ref[...]) * (1.0 + scq_ref[0]) + shq_ref[0]).astype(BF16)
    W = v32_ref.shape[1]
    kv = jnp.dot(hk, wkv_ref[...], preferred_element_type=F32)
    k = _group_rms(kv[:, :W], gmat_ref[...], gmatt_ref[...], gkn_ref[...], B_DK)
    v = kv[:, W:]
    v32_ref[...] = v
    q = jnp.dot(hq, wq_ref[...], preferred_element_type=F32)
    q = _group_rms(q, gmat_ref[...], gmatt_ref[...], gqn_ref[...], B_DK)
    q_ref[...] = (q * (B_DK ** -0.5)).astype(q_ref.dtype)
    if maybe_attn_refs:
        k16_ref, vt_ref = maybe_attn_refs
        k_ref[0] = k.T
        k16_ref[...] = k.astype(BF16)
        vt_ref[0] = v.T.astype(BF16)
    else:
        k_ref[...] = k


def _kvq(x2, shk, sck, gk, shq, scq, gq, wkv16, wq16, gmat, gmat_t, gkn, gqn, tm, seq_len):
    N, D = x2.shape
    W = wq16.shape[1]
    row = lambda w: pl.BlockSpec((tm, w), lambda i: (i, 0))
    full = lambda shape: pl.BlockSpec(shape, lambda i: (0,) * len(shape))
    if seq_len % tm == 0:
        tps = seq_len // tm
        col = pl.BlockSpec((1, W, tm), lambda i: (i // tps, 0, i % tps))
        out_specs = [col, row(W), row(W), row(W), col]
        out_shape = [jax.ShapeDtypeStruct((N // seq_len, W, seq_len), F32), jax.ShapeDtypeStruct((N, W), F32),
                     jax.ShapeDtypeStruct((N, W), BF16), jax.ShapeDtypeStruct((N, W), BF16),
                     jax.ShapeDtypeStruct((N // seq_len, W, seq_len), BF16)]
    else:
        out_specs = [row(W), row(W), row(W)]
        out_shape = [jax.ShapeDtypeStruct((N, W), F32)] * 3
    return pl.pallas_call(
        _kvq_kernel,
        grid=(N // tm,),
        in_specs=[row(D), shk.spec(1), sck.spec(1), full((1, D)), shq.spec(1), scq.spec(1), full((1, D)),
                  full((D, 2 * W)), full((D, W)), full((W, LANES)), full((LANES, W)),
                  full((1, W)), full((1, W))],
        out_specs=out_specs,
        out_shape=out_shape,
        compiler_params=_params("parallel"),
        name="kvq",
    )(x2, shk.arr, sck.arr, gk, shq.arr, scq.arr, gq, wkv16, wq16, gmat, gmat_t, gkn, gqn)


def _lambda(lam_ref, lam_init):
    lv = lam_ref[...]
    a = jnp.sum(lv[0:1] * lv[1:2], axis=1, keepdims=True)
    b = jnp.sum(lv[2:3] * lv[3:4], axis=1, keepdims=True)
    return jnp.exp(a) - jnp.exp(b) + lam_init


def _attn_kernel(qi_ref, kj_ref, ty_ref, fin_ref, q_ref, k_ref, vt_ref, bias_ref, lam_ref, ghn_ref, o_ref,
                 qm_sc, m_sc, l_sc, acc_sc, *, n_types, lam_init):
    step = pl.program_id(2)
    ty = ty_ref[step]

    @pl.when(kj_ref[step] == 0)
    def _():
        q = q_ref[0]
        lane = lax.broadcasted_iota(jnp.int32, q.shape, 1)
        zero = jnp.zeros_like(q)
        qm_sc[0] = jnp.where(lane < B_DK, q, zero)
        qm_sc[1] = jnp.where(lane >= B_DK, q, zero)
        m_sc[...] = jnp.full_like(m_sc, -jnp.inf)
        l_sc[...] = jnp.zeros_like(l_sc)
        acc_sc[...] = jnp.zeros_like(acc_sc)

    def update(adj):
        k = k_ref[0]
        vt = vt_ref[0]
        for c in range(2):
            s = _nt_dot(k, qm_sc[c])
            if adj is not None:
                s = s + adj
            m_old = m_sc[c]
            m_new = jnp.maximum(m_old, jnp.max(s, axis=0, keepdims=True))
            alpha = jnp.exp(m_old - m_new)
            p = jnp.exp(s - m_new)
            l_sc[c] = alpha * l_sc[c] + jnp.sum(p, axis=0, keepdims=True)
            acc_sc[c] = alpha * acc_sc[c] + jnp.dot(vt, p.astype(BF16), preferred_element_type=F32)
            m_sc[c] = m_new

    @pl.when(ty < 0)
    def _():
        update(None)

    for t in range(n_types):
        @pl.when(ty == t)
        def _(t=t):
            update(bias_ref[0, t])

    @pl.when(fin_ref[step] == 1)
    def _():
        lam = _lambda(lam_ref, lam_init)
        ot = acc_sc[0] / l_sc[0] - lam * (acc_sc[1] / l_sc[1])
        on = ot * lax.rsqrt(jnp.mean(ot * ot, axis=0, keepdims=True) + EPS) * ghn_ref[...] * (1.0 - lam_init)
        o_ref[0] = on.T.astype(o_ref.dtype)


def _attn_schedule(T, tq, tk):
    offsets = sorted({qi * tq - kj * tk for qi in range(T // tq) for kj in range(T // tk)
                      if qi * tq + tq - 1 >= kj * tk and qi * tq - kj * tk - (tk - 1) < RPB_MAX_DIST})
    qi_l, kj_l, ty_l, fin_l = [], [], [], []
    for qi in range(T // tq):
        kjs = [kj for kj in range(T // tk) if qi * tq + tq - 1 >= kj * tk]
        for kj in kjs:
            off = qi * tq - kj * tk
            qi_l.append(qi)
            kj_l.append(kj)
            ty_l.append(offsets.index(off) if off in offsets else -1)
            fin_l.append(int(kj == kjs[-1]))
    as_i32 = lambda v: jnp.asarray(np.asarray(v, np.int32))
    return offsets, as_i32(qi_l), as_i32(kj_l), as_i32(ty_l), as_i32(fin_l)


def _bias_table(rpb):
    n = jnp.arange(RPB_MAX_DIST, dtype=jnp.int32)
    max_exact = RPB_BUCKETS // 2
    nf = jnp.maximum(n, 1).astype(F32)
    large = max_exact + (jnp.log(nf / max_exact) / math.log(RPB_MAX_DIST / max_exact)
                         * (RPB_BUCKETS - max_exact)).astype(jnp.int32)
    bucket = jnp.where(n < max_exact, n, jnp.minimum(large, RPB_BUCKETS - 1))
    return (rpb[bucket] - rpb[RPB_BUCKETS - 1][None, :]).T.astype(F32)


def _bias_of_distance(tbl, dist):
    d = np.asarray(dist)
    idx = jnp.asarray(np.clip(d, 0, RPB_MAX_DIST - 1).astype(np.int32))
    vals = jnp.take(tbl, idx, axis=1)
    vals = jnp.where(jnp.asarray(d >= RPB_MAX_DIST), 0.0, vals)
    return jnp.where(jnp.asarray(d < 0), -jnp.inf, vals)


def _bias_tiles_t(tbl, offsets, tq, tk):
    period = tq + tk
    w = np.arange(period)
    u = np.where(w < tq, w, w - period)
    vext = _bias_of_distance(tbl, np.stack([off + u for off in offsets]))
    H, n_types = vext.shape[:2]
    flat = jnp.broadcast_to(vext[:, :, None, :], (H, n_types, tk, period)).reshape(H, n_types, tk * period)
    return flat[:, :, :tk * (period - 1)].reshape(H, n_types, tk, period - 1)[:, :, :, :tq]


def _attn_prompt(q16, k16, vt16, tbl, lam, ghn_col, lam_init, tq, tk):
    B, T, W = q16.shape
    H = W // LANES
    offsets, qi, kj, ty, fin = _attn_schedule(T, tq, tk)
    bias = _bias_tiles_t(tbl, offsets, tq, tk)
    kern = functools.partial(_attn_kernel, n_types=len(offsets), lam_init=lam_init)
    grid_spec = pltpu.PrefetchScalarGridSpec(
        num_scalar_prefetch=4,
        grid=(H, B, int(qi.shape[0])),
        in_specs=[pl.BlockSpec((1, tq, LANES), lambda h, b, s, qi, kj, ty, fin: (b, qi[s], h)),
                  pl.BlockSpec((1, tk, LANES), lambda h, b, s, qi, kj, ty, fin: (b, kj[s], h)),
                  pl.BlockSpec((1, LANES, tk), lambda h, b, s, qi, kj, ty, fin: (b, h, kj[s])),
                  pl.BlockSpec((1, len(offsets), tk, tq), lambda h, b, s, *_: (h, 0, 0, 0)),
                  pl.BlockSpec(lam.shape, lambda h, b, s, *_: (0, 0)),
                  pl.BlockSpec((LANES, 1), lambda h, b, s, *_: (h, 0))],
        out_specs=pl.BlockSpec((1, tq, LANES), lambda h, b, s, qi, kj, ty, fin: (b, qi[s], h)),
        scratch_shapes=[pltpu.VMEM((2, tq, LANES), BF16), pltpu.VMEM((2, 1, tq), F32),
                        pltpu.VMEM((2, 1, tq), F32), pltpu.VMEM((2, LANES, tq), F32)],
    )
    return pl.pallas_call(
        kern,
        grid_spec=grid_spec,
        out_shape=jax.ShapeDtypeStruct((B, T, W), BF16),
        compiler_params=_params("parallel", "parallel", "arbitrary"),
        name="attn_prompt",
    )(qi, kj, ty, fin, q16, k16, vt16, bias, lam, ghn_col)


PAGES_PER_STEP = 4


def _attn_paged_kernel(pt_ref, q_ref, *refs, heads, t_new, page, n_steps, lam_init):
    pps = PAGES_PER_STEP
    kc_refs, vc_refs = refs[:pps], refs[pps:2 * pps]
    kn_ref, vn_ref, blast_ref, bnew_ref, lam_ref, ghn_ref, o_ref, qm_sc, m_sc, l_sc, acc_sc = refs[2 * pps:]
    j = pl.program_id(1)
    R = SUBLANES
    hsl = lambda h: slice(h * R, (h + 1) * R)
    lsl = lambda h: slice(h * LANES, (h + 1) * LANES)

    @pl.when(j == 0)
    def _():
        q = q_ref[0]
        row = lax.broadcasted_iota(jnp.int32, (R, LANES), 0)
        lane = lax.broadcasted_iota(jnp.int32, (R, LANES), 1)
        keep = (row < t_new) == (lane < B_DK)
        for h in range(heads):
            qm_sc[hsl(h), :] = jnp.where(keep, q[:, lsl(h)], 0.0)
        m_sc[...] = jnp.full_like(m_sc, -jnp.inf)
        l_sc[...] = jnp.zeros_like(l_sc)
        acc_sc[...] = jnp.zeros_like(acc_sc)

    qm = qm_sc[...].astype(BF16)

    def update(s, pv_of):
        m_old = m_sc[...]
        m_new = jnp.maximum(m_old, jnp.max(s, axis=1, keepdims=True))
        alpha = jnp.exp(m_old - m_new)
        p = jnp.exp(s - m_new)
        l_sc[...] = alpha * l_sc[...] + jnp.sum(p, axis=1, keepdims=True)
        pb = p.astype(BF16)
        acc_sc[...] = alpha * acc_sc[...] + jnp.concatenate([pv_of(h, pb[hsl(h)]) for h in range(heads)], axis=0)
        m_sc[...] = m_new

    s = jnp.concatenate(
        [jnp.concatenate([jnp.dot(qm[hsl(h)], kc[0, lsl(h), :].astype(BF16), preferred_element_type=F32)
                          for h in range(heads)], axis=0) for kc in kc_refs], axis=1)
    s = s + jnp.where(j == n_steps - 1, blast_ref[...], 0.0)

    def pv_cached(h, ph):
        parts = [jnp.dot(ph[:, u * page:(u + 1) * page], vc[0, pl.ds(h, page, stride=heads), :].astype(BF16),
                         preferred_element_type=F32) for u, vc in enumerate(vc_refs)]
        return functools.reduce(lambda a, b: a + b, parts)

    update(s, pv_cached)

    @pl.when(j == n_steps - 1)
    def _():
        kn = kn_ref[0].astype(BF16)
        vn = vn_ref[0].astype(BF16)
        s_new = jnp.concatenate([_nt_dot(qm[hsl(h)], kn[:, lsl(h)]) for h in range(heads)], axis=0) + bnew_ref[...]
        update(s_new, lambda h, ph: jnp.dot(ph, vn[:, lsl(h)], preferred_element_type=F32))
        lam = _lambda(lam_ref, lam_init)
        full = acc_sc[...] / l_sc[...]
        for h in range(heads):
            fh = full[hsl(h)]
            o = fh - lam * pltpu.roll(fh, R - t_new, 0)
            on = o * lax.rsqrt(jnp.mean(o * o, axis=1, keepdims=True) + EPS)
            o_ref[0, :, h * LANES:(h + 1) * LANES] = on * ghn_ref[:, h * LANES:(h + 1) * LANES] * (1.0 - lam_init)


def _attn_paged(q, cache_k, cache_v, page_table, k_new, v_new, tbl, lam, ghn, lam_init):
    B, t_new, W = q.shape
    H = W // LANES
    n_pool, page = cache_k.shape[:2]
    n_pages = page_table.shape[1]
    past = n_pages * page
    R = SUBLANES
    pps = PAGES_PER_STEP
    assert 2 * t_new == R and page >= RPB_MAX_DIST and n_pages % pps == 0
    n_steps = n_pages // pps
    pad = lambda a: jnp.concatenate([a, jnp.zeros((B, R - t_new, W), a.dtype)], axis=1)
    q8 = jnp.concatenate([q, q], axis=1)
    t = np.arange(R)[:, None] % t_new
    d_last = past + t - ((n_pages - 1) * page + np.arange(page)[None, :])
    c = np.arange(R)[None, :]
    d_new = np.where(c < t_new, t - c, -1)
    flat = lambda b: b.reshape(H * R, b.shape[-1])
    bias_last = jnp.pad(flat(_bias_of_distance(tbl, d_last)), ((0, 0), ((pps - 1) * page, 0)))
    bias_new = flat(_bias_of_distance(tbl, d_new))
    kern = functools.partial(_attn_paged_kernel, heads=H, t_new=t_new, page=page, n_steps=n_steps, lam_init=lam_init)
    page_spec = lambda rows, width, u: pl.BlockSpec(
        (1, rows, width), lambda b, j, pt: (pt[b * n_pages + j * pps + u], 0, 0))
    per_b = pl.BlockSpec((1, R, W), lambda b, j, pt: (b, 0, 0))
    full = lambda a: pl.BlockSpec(a.shape, lambda b, j, pt: (0,) * a.ndim)
    grid_spec = pltpu.PrefetchScalarGridSpec(
        num_scalar_prefetch=1,
        grid=(B, n_steps),
        in_specs=[per_b] + [page_spec(W, page, u) for u in range(pps)]
                 + [page_spec(page * H, B_DV, u) for u in range(pps)]
                 + [per_b, per_b, full(bias_last), full(bias_new), full(lam), full(ghn)],
        out_specs=per_b,
        scratch_shapes=[pltpu.VMEM((H * R, LANES), F32), pltpu.VMEM((H * R, 1), F32),
                        pltpu.VMEM((H * R, 1), F32), pltpu.VMEM((H * R, B_DV), F32)],
    )
    ck = jnp.transpose(cache_k, (0, 2, 3, 4, 1)).reshape(n_pool, W, page)
    cv = cache_v.reshape(n_pool, page * H, B_DV)
    out = pl.pallas_call(
        kern,
        grid_spec=grid_spec,
        out_shape=jax.ShapeDtypeStruct((B, R, W), F32),
        compiler_params=_params("parallel", "arbitrary"),
        name="attn_paged",
    )(page_table.reshape(-1), q8, *([ck] * pps), *([cv] * pps), pad(k_new), pad(v_new),
      bias_last, bias_new, lam, ghn)
    return out[:, :t_new]


def _trunk(x, mods, mods_kv, state, past, wts):
    B, T, D = x.shape
    N = B * T
    tm = ROW_TILE
    mod = lambda m: _Mod(m, T, tm)
    split3 = lambda m: (mod(m[:, :D]), mod(m[:, D:2 * D]), mod(m[:, 2 * D:]))
    x2 = x.reshape(N, D)

    heads = A_HEADS
    inner = wts["w_out16"].shape[0]
    dh = inner // heads
    shift, scale, gate = split3(mods[0])
    proj, gates = _inproj(x2, shift, scale, wts["g_norm"][0, 0][None], wts["w_in16"], wts["w_gate"], tm)
    proj = proj.reshape(B, T, 4 * inner)
    gates = gates.reshape(B, T, LANES)
    conv_new = proj[:, T - (A_CONV - 1):, :2 * inner]
    if state is None:
        L, t_valid = math.gcd(T, MLSTM_CHUNK), None
        conv_init = jnp.zeros((B, SUBLANES, 2 * inner), F32)
        c0 = jnp.zeros((B, heads, dh, dh), F32)
        n0 = jnp.zeros((B, heads, 1, dh), F32)
        m0 = jnp.zeros((B, heads, 1, 1), F32)
    else:
        conv_st, c_st, n_st, m_st = state
        assert T <= SUBLANES
        L, t_valid = SUBLANES, T
        rows = lambda a: jnp.concatenate([a, jnp.zeros((B, L - T, a.shape[2]), a.dtype)], axis=1)
        proj, gates = rows(proj), rows(gates)
        conv_init = jnp.concatenate([jnp.zeros((B, SUBLANES - (A_CONV - 1), 2 * inner), F32), conv_st], axis=1)
        c0, n0, m0 = c_st, n_st[:, :, None, :], m_st[:, :, None, None]
    gates_t = jnp.swapaxes(gates[:, :, :SUBLANES], 1, 2)
    hs, c1, n1, m1 = _mlstm(proj, gates, gates_t, wts["bg"], wts["bgt"], conv_init, c0, n0, m0,
                            wts["w_conv"], wts["b_conv"], wts["g_hn_a"], L, t_valid)
    hs = hs[:, :T].reshape(N, inner)
    new_state = (conv_new, c1, n1[:, :, 0, :], m1[:, :, 0, 0])

    shift2, scale2, gate2 = split3(mods[1])
    x2, h16, rg = _proj_router(hs, wts["w_out16"], x2, gate, shift2, scale2, wts["g_norm"][0, 1][None],
                               wts["w_router"], wts["b_router"], tm)
    x2 = _moe(h16, rg, x2, gate2, wts["wg16"][0], wts["wu16"][0], wts["wd16"][0], tm)

    shift_kv, scale_kv = mod(mods_kv[:, :D]), mod(mods_kv[:, D:])
    shift, scale, gate = split3(mods[2])
    kvq = _kvq(x2, shift_kv, scale_kv, wts["g_kv"], shift, scale, wts["g_norm"][1, 0][None],
               wts["w_kv16"], wts["w_q16"], wts["gmat"], wts["gmat_t"], wts["g_kn"], wts["g_qn"], tm, T)
    v32 = kvq[1]
    W = v32.shape[1]
    H = W // LANES
    lam_init = 0.8 - 0.6 * math.exp(-0.3 * 1)
    if past is None:
        kt32, _, q, k16, vt16 = kvq
        o = _attn_prompt(q.reshape(B, T, W), k16.reshape(B, T, W), vt16, wts["rpb_tbl"], wts["lam"],
                         wts["g_hn_b"].reshape(W, 1), lam_init, math.gcd(T, ATTN_TQ), math.gcd(T, ATTN_TK))
        o = o.reshape(N, W)
        k_out = jnp.transpose(kt32.reshape(B, H, 2, B_DK, T), (0, 4, 1, 2, 3))
    else:
        k32, _, q = kvq
        k_out = k32.reshape(B, T, H, 2, B_DK)
        cache_k, cache_v, page_table = past
        o = _attn_paged(q.reshape(B, T, W), cache_k, cache_v, page_table, k32.reshape(B, T, W),
                        v32.reshape(B, T, W), wts["rpb_tbl"], wts["lam"], wts["g_hn_b"], lam_init)
        o = o.reshape(N, W).astype(BF16)

    shift2, scale2, gate2 = split3(mods[3])
    x2, h16, rg = _proj_router(o, wts["w_o16"], x2, gate, shift2, scale2, wts["g_norm"][1, 1][None],
                               wts["w_router"], wts["b_router"], tm)
    x2 = _moe(h16, rg, x2, gate2, wts["wg16"][1], wts["wu16"][1], wts["wd16"][1], tm)

    return x2.reshape(B, T, D), new_state, k_out, v32.reshape(B, T, H, B_DV)


def kernel(x_prompt, x_sample, c_prompt, c_sample, state_conv, state_C, state_n, state_m, cache_k, cache_v, page_table, w_ada, b_ada, g_norm, w_in_a, b_gate_a, w_conv_a, b_conv_a, g_hn_a, w_out_a, g_kv, w_ada_kv, b_ada_kv, w_kv, g_kn, w_q_b, g_qn_b, lam_b, g_hn_b, w_o_b, rpb, w_router, b_router, w_gate_e, w_up_e, w_down_e):
    Bp, Tp, D = x_prompt.shape
    Bs = x_sample.shape[0]
    inner = w_out_a.shape[1]
    heads_b = g_hn_b.shape[1]
    W = heads_b * B_DV

    n_c = Bp + Bs
    c_all = jnp.concatenate([c_prompt, c_sample, jnp.zeros((-n_c % SUBLANES, D), F32)], axis=0)
    mods = _ada(c_all, w_ada.reshape(-1, D, 3 * D), b_ada.reshape(-1, 1, 3 * D))
    mods_kv = _ada(c_all, w_ada_kv[None], b_ada_kv[None, None])[0]

    n_gate = 2 * A_HEADS
    group_of_lane = np.arange(W) // B_DK
    gmat = jnp.asarray((group_of_lane[:, None] == np.arange(LANES)[None, :]).astype(np.float32)).astype(BF16)
    wts = {
        "g_norm": g_norm,
        "w_in16": w_in_a[0][:, :4 * inner].astype(BF16),
        "w_gate": jnp.pad(w_in_a[0][:, 4 * inner:], ((0, 0), (0, LANES - n_gate))),
        "bg": jnp.pad(b_gate_a[0], (0, LANES - n_gate))[None, :],
        "bgt": b_gate_a[0][:, None],
        "w_conv": w_conv_a[0], "b_conv": b_conv_a[0][None, :],
        "g_hn_a": g_hn_a[0].reshape(1, inner),
        "w_out16": w_out_a[0].astype(BF16),
        "g_kv": g_kv[None, :],
        "w_kv16": w_kv.astype(BF16), "w_q16": w_q_b[0].astype(BF16),
        "gmat": gmat, "gmat_t": gmat.T,
        "g_kn": jnp.tile(g_kn.reshape(-1), heads_b)[None, :],
        "g_qn": jnp.tile(g_qn_b[0].reshape(-1), heads_b)[None, :],
        "lam": lam_b[0], "g_hn_b": g_hn_b[0].reshape(1, W),
        "w_o16": w_o_b[0].astype(BF16),
        "rpb_tbl": _bias_table(rpb),
        "w_router": jnp.pad(w_router, ((0, 0), (0, LANES - N_EXPERTS))),
        "b_router": b_router[:, None],
        "wg16": w_gate_e.astype(BF16), "wu16": w_up_e.astype(BF16), "wd16": w_down_e.astype(BF16),
    }

    y_p, st_p, k_p, v_p = _trunk(x_prompt, mods[:, :Bp], mods_kv[:Bp], None, None, wts)
    y_s, st_s, k_s, v_s = _trunk(x_sample, mods[:, Bp:n_c], mods_kv[Bp:n_c],
                                 (state_conv[0], state_C[0], state_n[0], state_m[0]),
                                 (cache_k, cache_v, page_table), wts)
    stack = lambda st: tuple(a[None] for a in st)
    return (y_p, y_s) + stack(st_p) + (k_p, v_p) + stack(st_s) + (k_s, v_s)
```

```python
import functools
import math

import numpy as np
import jax
import jax.numpy as jnp
from jax import lax
from jax.experimental import pallas as pl
from jax.experimental.pallas import tpu as pltpu

F32, BF16 = jnp.float32, jnp.bfloat16
HIGHEST = lax.Precision.HIGHEST
EPS = 1e-6

A_HEADS = 4
A_CONV = 4
B_DK = 64
B_DV = 128
N_EXPERTS = 16
N_GROUPS = 4
RPB_BUCKETS = 32
RPB_MAX_DIST = 128

LANES = 128
SUBLANES = 8
VMEM_LIMIT_BYTES = 56 * 1024 * 1024

ROW_TILE = 512
MLSTM_CHUNK = 256
ATTN_TQ = 512
ATTN_TK = 512


def _params(*sem):
    return pltpu.CompilerParams(dimension_semantics=sem, vmem_limit_bytes=VMEM_LIMIT_BYTES)


def _nt_dot(a, b):
    return lax.dot_general(a, b, (((1,), (1,)), ((), ())), preferred_element_type=F32)


def _tn_dot(a, b):
    return lax.dot_general(a, b, (((0,), (0,)), ((), ())), preferred_element_type=F32)


def _silu(x):
    return x * jax.nn.sigmoid(x)


def _rms_mod(x, g, scale, shift):
    y = x * lax.rsqrt(jnp.mean(x * x, axis=-1, keepdims=True) + EPS)
    return (y * g) * (1.0 + scale) + shift


def _dot_split(a, b16):
    hi = a.astype(BF16)
    lo = (a - hi.astype(F32)).astype(BF16)
    return (jnp.dot(hi, b16, preferred_element_type=F32) + jnp.dot(lo, b16, preferred_element_type=F32))


def _group_rms(x, gmat, gmat_t, g, group):
    ss = _dot_split(x * x, gmat)
    r = lax.rsqrt(ss * (1.0 / group) + EPS)
    rf = _dot_split(r, gmat_t)
    return x * rf * g


def _ada_kernel(c_ref, w_ref, b_ref, o_ref):
    a = _silu(c_ref[...])
    o_ref[0] = jnp.dot(a, w_ref[0], precision=HIGHEST, preferred_element_type=F32) + b_ref[0]


def _ada(c_all, w, b):
    S, D, Fo = w.shape
    R = c_all.shape[0]
    tn = 1024
    return pl.pallas_call(
        _ada_kernel,
        grid=(S, Fo // tn),
        in_specs=[pl.BlockSpec((R, D), lambda s, j: (0, 0)),
                  pl.BlockSpec((1, D, tn), lambda s, j: (s, 0, j)),
                  pl.BlockSpec((1, 1, tn), lambda s, j: (s, 0, j))],
        out_specs=pl.BlockSpec((1, R, tn), lambda s, j: (s, 0, j)),
        out_shape=jax.ShapeDtypeStruct((S, R, Fo), F32),
        compiler_params=_params("parallel", "parallel"),
        name="ada",
    )(c_all, w, b)


class _Mod:
    def __init__(self, m, T, tm):
        B, D = m.shape
        if T % tm == 0:
            self.arr, self.tiles_per_group = m[:, None, :], T // tm
        else:
            assert (B * T) % tm == 0
            self.arr, self.tiles_per_group = jnp.repeat(m, T, axis=0).reshape(-1, tm, D), 1

    def spec(self, grid_rank):
        R, D = self.arr.shape[1:]
        tpg = self.tiles_per_group
        if grid_rank == 1:
            return pl.BlockSpec((1, R, D), lambda i: (i // tpg, 0, 0))
        return pl.BlockSpec((1, R, D), lambda i, j: (i // tpg, 0, 0))


def _inproj_kernel(x_ref, sh_ref, sc_ref, g_ref, w_ref, wg_ref, o_ref, og_ref, h_sc):
    @pl.when(pl.program_id(1) == 0)
    def _():
        h = _rms_mod(x_ref[...], g_ref[...], sc_ref[0], sh_ref[0])
        h_sc[...] = h.astype(BF16)
        og_ref[...] = jnp.dot(h, wg_ref[...], precision=HIGHEST, preferred_element_type=F32)

    o_ref[...] = jnp.dot(h_sc[...], w_ref[...], preferred_element_type=F32)


def _inproj(x2, shift, scale, g, w16, wgate, tm):
    N, D = x2.shape
    Fo = w16.shape[1]
    tn = 2048
    return pl.pallas_call(
        _inproj_kernel,
        grid=(N // tm, Fo // tn),
        in_specs=[pl.BlockSpec((tm, D), lambda i, j: (i, 0)),
                  shift.spec(2), scale.spec(2),
                  pl.BlockSpec((1, D), lambda i, j: (0, 0)),
                  pl.BlockSpec((D, tn), lambda i, j: (0, j)),
                  pl.BlockSpec((D, LANES), lambda i, j: (0, 0))],
        out_specs=[pl.BlockSpec((tm, tn), lambda i, j: (i, j)),
                   pl.BlockSpec((tm, LANES), lambda i, j: (i, 0))],
        out_shape=[jax.ShapeDtypeStruct((N, Fo), F32), jax.ShapeDtypeStruct((N, LANES), F32)],
        scratch_shapes=[pltpu.VMEM((tm, D), BF16)],
        compiler_params=_params("parallel", "arbitrary"),
        name="mlstm_inproj",
    )(x2, shift.arr, scale.arr, g, w16, wgate)


def _mlstm_kernel(q_ref, k_ref, v_ref, o_ref, gt_ref, gtt_ref, bg_ref, bgt_ref, cinit_ref,
                  c0_ref, n0_ref, m0_ref, wconv_ref, bconv_ref, ghn_ref,
                  hs_ref, c_ref, n_ref, m_ref, tail_sc, *, L, dh, heads, t_valid):
    inner = heads * dh

    @pl.when(pl.program_id(1) == 0)
    def _():
        c_ref[...] = c0_ref[...]
        n_ref[...] = n0_ref[...]
        m_ref[...] = m0_ref[...]
        tail_sc[...] = cinit_ref[0]

    row8 = lax.broadcasted_iota(jnp.int32, (SUBLANES, dh), 0)

    def conv(x, tail, w, b):
        acc = b + x * w[A_CONV - 1:A_CONV]
        for s in range(1, A_CONV):
            xs = pltpu.roll(x, s, 0)
            top = jnp.where(row8 < s, pltpu.roll(tail, s, 0), xs[:SUBLANES])
            xs = top if L == SUBLANES else jnp.concatenate([top, xs[SUBLANES:]], axis=0)
            acc = acc + xs * w[A_CONV - 1 - s:A_CONV - s]
        return acc

    gt = gt_ref[0] + bg_ref[...]
    gtt = gtt_ref[0] + bgt_ref[...]
    ti = lax.broadcasted_iota(jnp.int32, (L, L), 0)
    si = lax.broadcasted_iota(jnp.int32, (L, L), 1)
    causal = si <= ti
    tcol = lax.broadcasted_iota(jnp.int32, (L, 1), 0)
    trow = lax.broadcasted_iota(jnp.int32, (1, L), 1)

    for h in range(heads):
        sl = slice(h * dh, (h + 1) * dh)
        slk = slice(inner + h * dh, inner + (h + 1) * dh)
        qh = _silu(conv(q_ref[0, :, sl], tail_sc[:, sl], wconv_ref[:, sl], bconv_ref[:, sl]))
        kh = _silu(conv(k_ref[0, :, sl], tail_sc[:, slk], wconv_ref[:, slk], bconv_ref[:, slk])) * (dh ** -0.5)
        vb = v_ref[0, :, sl].astype(BF16)

        ig_col = gt[:, h:h + 1]
        lf_col = jax.nn.log_sigmoid(gt[:, heads + h:heads + h + 1])
        ig_row = gtt[h:h + 1, :]
        lf_row = jax.nn.log_sigmoid(gtt[heads + h:heads + h + 1, :])
        if t_valid is not None:
            ig_col = jnp.where(tcol < t_valid, ig_col, -jnp.inf)
            lf_col = jnp.where(tcol < t_valid, lf_col, 0.0)
            ig_row = jnp.where(trow < t_valid, ig_row, -jnp.inf)
            lf_row = jnp.where(trow < t_valid, lf_row, 0.0)

        b_col = jnp.sum(jnp.where(causal, lf_row, 0.0), axis=1, keepdims=True)
        b_row = jnp.sum(jnp.where(ti <= si, lf_col, 0.0), axis=0, keepdims=True)
        dlog = jnp.where(causal, b_col - b_row + ig_row, -jnp.inf)
        g_col = b_col + m_ref[0, h]
        m_col = jnp.maximum(g_col, jnp.max(dlog, axis=1, keepdims=True))
        w_intra = jnp.exp(dlog - m_col)
        w_inter = jnp.exp(g_col - m_col)

        qb = qh.astype(BF16)
        kb = kh.astype(BF16)
        s = w_intra * _nt_dot(qb, kb)
        ch = c_ref[0, h]
        nh = n_ref[0, h]
        num = w_inter * _nt_dot(qb, ch.astype(BF16)) + jnp.dot(s.astype(BF16), vb, preferred_element_type=F32)
        den = w_inter * jnp.sum(qh * nh, axis=1, keepdims=True) + jnp.sum(s, axis=1, keepdims=True)
        hv = num / jnp.maximum(jnp.abs(den), jnp.exp(-m_col))

        m_end = m_col[L - 1:L]
        we_inter = jnp.exp(g_col[L - 1:L] - m_end)
        we_col = jnp.exp(b_col[L - 1:L] - b_col + ig_col - m_end)
        kw = kh * we_col
        c_ref[0, h] = we_inter * ch + _tn_dot(vb, kw.astype(BF16))
        n_ref[0, h] = we_inter * nh + jnp.sum(kw, axis=0, keepdims=True)
        m_ref[0, h] = m_end

        hn = hv * lax.rsqrt(jnp.mean(hv * hv, axis=1, keepdims=True) + EPS) * ghn_ref[:, sl]
        hs_ref[0, :, sl] = (jax.nn.sigmoid(o_ref[0, :, sl]) * hn).astype(hs_ref.dtype)

    tail_sc[:, :inner] = q_ref[0, L - SUBLANES:, :]
    tail_sc[:, inner:] = k_ref[0, L - SUBLANES:, :]


def _mlstm(proj, gates, gates_t, bg, bgt, conv_init, c0, n0, m0, wconv, bconv, ghn, L, t_valid):
    B, Tp, _ = proj.shape
    heads, dh = c0.shape[1], c0.shape[2]
    inner = heads * dh
    nc = Tp // L
    kern = functools.partial(_mlstm_kernel, L=L, dh=dh, heads=heads, t_valid=t_valid)
    col = lambda j: pl.BlockSpec((1, L, inner), lambda b, c: (b, c, j))
    full = lambda shape: pl.BlockSpec(shape, lambda b, c: (0,) * len(shape))
    per_b = lambda shape: pl.BlockSpec((1,) + shape, lambda b, c: (b,) + (0,) * len(shape))
    return pl.pallas_call(
        kern,
        grid=(B, nc),
        in_specs=[col(0), col(1), col(2), col(3),
                  pl.BlockSpec((1, L, LANES), lambda b, c: (b, c, 0)),
                  pl.BlockSpec((1, SUBLANES, L), lambda b, c: (b, 0, c)),
                  full((1, LANES)), full((SUBLANES, 1)),
                  per_b((SUBLANES, 2 * inner)),
                  per_b((heads, dh, dh)), per_b((heads, 1, dh)), per_b((heads, 1, 1)),
                  full((A_CONV, 2 * inner)), full((1, 2 * inner)), full((1, inner))],
        out_specs=[pl.BlockSpec((1, L, inner), lambda b, c: (b, c, 0)),
                   per_b((heads, dh, dh)), per_b((heads, 1, dh)), per_b((heads, 1, 1))],
        out_shape=[jax.ShapeDtypeStruct((B, Tp, inner), BF16),
                   jax.ShapeDtypeStruct((B, heads, dh, dh), F32),
                   jax.ShapeDtypeStruct((B, heads, 1, dh), F32),
                   jax.ShapeDtypeStruct((B, heads, 1, 1), F32)],
        scratch_shapes=[pltpu.VMEM((SUBLANES, 2 * inner), F32)],
        compiler_params=_params("parallel", "arbitrary"),
        name="mlstm",
    )(proj, proj, proj, proj, gates, gates_t, bg, bgt, conv_init, c0, n0, m0, wconv, bconv, ghn)


ROUTE_ROWS = SUBLANES


def _route(h, wr, br):
    tm = h.shape[0]
    per = N_EXPERTS // N_GROUPS
    logits = jnp.dot(h, wr, precision=HIGHEST, preferred_element_type=F32)
    lt = logits.T[:N_EXPERTS]
    s = jax.nn.sigmoid(lt)
    sel = s + br
    neg = jnp.full((1, tm), -jnp.inf, F32)
    izero = jnp.zeros((1, tm), jnp.int32)

    best_score = best_e1 = best_e2 = best_w1 = best_w2 = None
    for grp in range(N_GROUPS):
        rows = [sel[grp * per + j:grp * per + j + 1] for j in range(per)]
        srow = [s[grp * per + j:grp * per + j + 1] for j in range(per)]
        t1, i1, w1 = rows[0], izero, srow[0]
        for j in range(1, per):
            better = rows[j] > t1
            t1 = jnp.where(better, rows[j], t1)
            i1 = jnp.where(better, j, i1)
            w1 = jnp.where(better, srow[j], w1)
        t2, i2, w2 = neg, izero, srow[0]
        for j in range(per):
            better = jnp.where(i1 == j, neg, rows[j]) > t2
            t2 = jnp.where(better, rows[j], t2)
            i2 = jnp.where(better, j, i2)
            w2 = jnp.where(better, srow[j], w2)
        score = t1 + t2
        e1, e2 = i1 + grp * per, i2 + grp * per
        if grp == 0:
            best_score, best_e1, best_e2, best_w1, best_w2 = score, e1, e2, w1, w2
        else:
            better = score > best_score
            best_score = jnp.where(better, score, best_score)
            best_e1 = jnp.where(better, e1, best_e1)
            best_e2 = jnp.where(better, e2, best_e2)
            best_w1 = jnp.where(better, w1, best_w1)
            best_w2 = jnp.where(better, w2, best_w2)
    tot = best_w1 + best_w2
    zero = jnp.zeros((ROUTE_ROWS - 4, tm), F32)
    return jnp.concatenate([best_e1.astype(F32), best_e2.astype(F32), best_w1 / tot, best_w2 / tot, zero], axis=0)


def _proj_router_kernel(a_ref, w_ref, x_ref, gate_ref, sh_ref, sc_ref, g_ref, wr_ref, br_ref,
                        xo_ref, h_ref, route_ref):
    mix = jnp.dot(a_ref[...], w_ref[...], preferred_element_type=F32)
    x = x_ref[...] + gate_ref[0] * mix
    xo_ref[...] = x
    h = _rms_mod(x, g_ref[...], sc_ref[0], sh_ref[0])
    h_ref[...] = h.astype(BF16)
    route_ref[...] = _route(h, wr_ref[...], br_ref[...])


def _proj_router(a16, w16, x2, gate, shift, scale, g, wr, br, tm):
    N, D = x2.shape
    K = a16.shape[1]
    row = lambda w: pl.BlockSpec((tm, w), lambda i: (i, 0))
    full = lambda shape: pl.BlockSpec(shape, lambda i: (0,) * len(shape))
    return pl.pallas_call(
        _proj_router_kernel,
        grid=(N // tm,),
        in_specs=[row(K), full((K, D)), row(D), gate.spec(1), shift.spec(1), scale.spec(1),
                  full((1, D)), full((D, LANES)), full((N_EXPERTS, 1))],
        out_specs=[row(D), row(D), pl.BlockSpec((ROUTE_ROWS, tm), lambda i: (0, i))],
        out_shape=[jax.ShapeDtypeStruct((N, D), F32), jax.ShapeDtypeStruct((N, D), BF16),
                   jax.ShapeDtypeStruct((ROUTE_ROWS, N), F32)],
        compiler_params=_params("parallel"),
        name="proj_router",
    )(a16, w16, x2, gate.arr, shift.arr, scale.arr, g, wr, br)


MOE_BLOCK = 1024
MOE_WINDOW = 256
MOE_CHUNK = 256
SEG_ALIGN = 16


def _moe_kernel(route_ref, h_ref, x_ref, gate_ref, tri_ref, wg_ref, wu_ref, wd_ref, o_ref,
                xs_sc, ys_sc, gs_sc, tok_sc, seg_sc, *, n_experts, n_sorted):
    e = pl.program_id(1)
    tb = h_ref.shape[0]
    S, CH, RW = n_sorted, MOE_CHUNK, MOE_WINDOW
    one_hot = lambda a, b: jnp.where(a, 1.0, jnp.where(b, 1.0, 0.0))

    @pl.when(e == 0)
    def _dispatch():
        route = route_ref[...]
        e1, e2, w1, w2 = route[0:1], route[1:2], route[2:3], route[3:4]
        eid = lax.broadcasted_iota(jnp.int32, (n_experts, tb), 0).astype(F32)
        hit1, hit2 = eid == e1, eid == e2
        routed = one_hot(hit1, hit2)
        earlier = jnp.dot(routed.astype(BF16), tri_ref[...], preferred_element_type=F32)
        count = jnp.sum(routed, axis=1, keepdims=True)
        padded = jnp.floor((count + (SEG_ALIGN - 1)) * (1.0 / SEG_ALIGN)) * SEG_ALIGN
        below = jnp.where(lax.broadcasted_iota(jnp.int32, (n_experts, n_experts), 1)
                          < lax.broadcasted_iota(jnp.int32, (n_experts, n_experts), 0), 1.0, 0.0)
        start = jnp.dot(below, jnp.broadcast_to(padded, (n_experts, LANES)), precision=HIGHEST,
                        preferred_element_type=F32)[:, :1]
        pos = start + earlier
        pos1 = jnp.sum(jnp.where(hit1, pos, 0.0), axis=0, keepdims=True)
        pos2 = jnp.sum(jnp.where(hit2, pos, 0.0), axis=0, keepdims=True)
        for ex in range(n_experts):
            seg_sc[0, ex] = start[ex, 0].astype(jnp.int32)
            seg_sc[1, ex] = (start[ex, 0] + count[ex, 0]).astype(jnp.int32)
        tok_sc[...] = jnp.concatenate([pos1, pos2, jnp.zeros((LANES - 2, tb), F32)], axis=0).T

        def gather(c, carry):
            r0 = pl.multiple_of(c * CH, CH)
            row = (lax.broadcasted_iota(jnp.int32, (CH, tb), 0) + r0).astype(F32)
            is1, is2 = row == pos1, row == pos2
            xs_sc[pl.ds(r0, CH), :] = jnp.dot(one_hot(is1, is2).astype(BF16), h_ref[...],
                                              preferred_element_type=F32).astype(BF16)
            gs_sc[pl.ds(r0, CH), :] = jnp.sum(jnp.where(is1, w1, jnp.where(is2, w2, 0.0)), axis=1, keepdims=True)
            return carry

        lax.fori_loop(0, S // CH, gather, 0)
        xs_sc[S:, :] = jnp.zeros((RW, xs_sc.shape[1]), BF16)
        gs_sc[S:, :] = jnp.zeros((RW, 1), F32)
        ys_sc[...] = jnp.zeros_like(ys_sc)

    seg_start, seg_end = seg_sc[0, e], seg_sc[1, e]

    def window(w, carry):
        r0 = pl.multiple_of(seg_start + w * RW, SEG_ALIGN)
        rows = xs_sc[pl.ds(r0, RW), :]
        a = jnp.dot(rows, wg_ref[0], preferred_element_type=F32)
        u = jnp.dot(rows, wu_ref[0], preferred_element_type=F32)
        mine = lax.broadcasted_iota(jnp.int32, (RW, 1), 0) + r0 < seg_end
        act = jnp.where(mine, _silu(a) * u * gs_sc[pl.ds(r0, RW), :], 0.0)
        ys_sc[pl.ds(r0, RW), :] += jnp.dot(act.astype(BF16), wd_ref[0], preferred_element_type=F32)
        return carry

    lax.fori_loop(0, (seg_end - seg_start + (RW - 1)) // RW, window, 0)

    @pl.when(e == n_experts - 1)
    def _combine():
        def to_bf16(c, carry):
            r0 = pl.multiple_of(c * CH, CH)
            xs_sc[pl.ds(r0, CH), :] = ys_sc[pl.ds(r0, CH), :].astype(BF16)
            return carry

        lax.fori_loop(0, S // CH, to_bf16, 0)

        def scatter(c, carry):
            t0 = pl.multiple_of(c * CH, CH)
            rec = tok_sc[pl.ds(t0, CH), :]
            col = lax.broadcasted_iota(jnp.int32, (CH, S), 1).astype(F32)
            sel = one_hot(col == rec[:, 0:1], col == rec[:, 1:2]).astype(BF16)
            y = jnp.dot(sel, xs_sc[0:S, :], preferred_element_type=F32)
            g = gate_ref[0] if gate_ref.shape[1] == 1 else gate_ref[0, pl.ds(t0, CH), :]
            o_ref[pl.ds(t0, CH), :] = x_ref[pl.ds(t0, CH), :] + g * y
            return carry

        lax.fori_loop(0, tb // CH, scatter, 0)


def _moe(h16, route, x2, gate, wg16, wu16, wd16, tb):
    N, D = x2.shape
    E, _, Fe = wg16.shape
    n_sorted = -(-(2 * tb + E * SEG_ALIGN) // MOE_CHUNK) * MOE_CHUNK
    tri = jnp.asarray(np.triu(np.ones((tb, tb), np.float32), 1), dtype=BF16)
    row = lambda w: pl.BlockSpec((tb, w), lambda i, e: (i, 0))
    kern = functools.partial(_moe_kernel, n_experts=E, n_sorted=n_sorted)
    return pl.pallas_call(
        kern,
        grid=(N // tb, E),
        in_specs=[pl.BlockSpec((ROUTE_ROWS, tb), lambda i, e: (0, i)), row(D), row(D), gate.spec(2),
                  pl.BlockSpec((tb, tb), lambda i, e: (0, 0)),
                  pl.BlockSpec((1, D, Fe), lambda i, e: (e, 0, 0)),
                  pl.BlockSpec((1, D, Fe), lambda i, e: (e, 0, 0)),
                  pl.BlockSpec((1, Fe, D), lambda i, e: (e, 0, 0))],
        out_specs=row(D),
        out_shape=jax.ShapeDtypeStruct((N, D), F32),
        scratch_shapes=[pltpu.VMEM((n_sorted + MOE_WINDOW, D), BF16), pltpu.VMEM((n_sorted + MOE_WINDOW, D), F32),
                        pltpu.VMEM((n_sorted + MOE_WINDOW, 1), F32), pltpu.VMEM((tb, LANES), F32),
                        pltpu.SMEM((2, E), jnp.int32)],
        compiler_params=_params("parallel", "arbitrary"),
        name="moe",
    )(route, h16, x2, gate.arr, tri, wg16, wu16, wd16)


def _kvq_kernel(x_ref, shk_ref, sck_ref, gk_ref, shq_ref, scq_ref, gq_ref, wkv_ref, wq_ref,
                gmat_ref, gmatt_ref, gkn_ref, gqn_ref, k_ref, v32_ref, q_ref, *maybe_attn_refs):
    x = x_ref[...]
    y = x * lax.rsqrt(jnp.mean(x * x, axis=-1, keepdims=True) + EPS)
    hk = ((y * gk_ref[...]) * (1.0 + sck_ref[0]) + shk_ref[0]).astype(BF16)
    hq = ((y * gq_ref[...]) * (1.0 + scq_ref[0]) + shq_ref[0]).astype(BF16)
    W = v32_ref.shape[1]
    kv = jnp.dot(hk, wkv_ref[...], preferred_element_type=F32)
    k = _group_rms(kv[:, :W], gmat_ref[...], gmatt_ref[...], gkn_ref[...], B_DK)
    v = kv[:, W:]
    v32_ref[...] = v
    q = jnp.dot(hq, wq_ref[...], preferred_element_type=F32)
    q = _group_rms(q, gmat_ref[...], gmatt_ref[...], gqn_ref[...], B_DK)
    q_ref[...] = (q * (B_DK ** -0.5)).astype(q_ref.dtype)
    if maybe_attn_refs:
        k16_ref, vt_ref = maybe_attn_refs
        k_ref[0] = k.T
        k16_ref[...] = k.astype(BF16)
        vt_ref[0] = v.T.astype(BF16)
    else:
        k_ref[...] = k


def _kvq(x2, shk, sck, gk, shq, scq, gq, wkv16, wq16, gmat, gmat_t, gkn, gqn, tm, seq_len):
    N, D = x2.shape
    W = wq16.shape[1]
    row = lambda w: pl.BlockSpec((tm, w), lambda i: (i, 0))
    full = lambda shape: pl.BlockSpec(shape, lambda i: (0,) * len(shape))
    if seq_len % tm == 0:
        tps = seq_len // tm
        col = pl.BlockSpec((1, W, tm), lambda i: (i // tps, 0, i % tps))
        out_specs = [col, row(W), row(W), row(W), col]
        out_shape = [jax.ShapeDtypeStruct((N // seq_len, W, seq_len), F32), jax.ShapeDtypeStruct((N, W), F32),
                     jax.ShapeDtypeStruct((N, W), BF16), jax.ShapeDtypeStruct((N, W), BF16),
                     jax.ShapeDtypeStruct((N // seq_len, W, seq_len), BF16)]
    else:
        out_specs = [row(W), row(W), row(W)]
        out_shape = [jax.ShapeDtypeStruct((N, W), F32)] * 3
    return pl.pallas_call(
        _kvq_kernel,
        grid=(N // tm,),
        in_specs=[row(D), shk.spec(1), sck.spec(1), full((1, D)), shq.spec(1), scq.spec(1), full((1, D)),
                  full((D, 2 * W)), full((D, W)), full((W, LANES)), full((LANES, W)),
                  full((1, W)), full((1, W))],
        out_specs=out_specs,
        out_shape=out_shape,
        compiler_params=_params("parallel"),
        name="kvq",
    )(x2, shk.arr, sck.arr, gk, shq.arr, scq.arr, gq, wkv16, wq16, gmat, gmat_t, gkn, gqn)


def _lambda(lam_ref, lam_init):
    lv = lam_ref[...]
    a = jnp.sum(lv[0:1] * lv[1:2], axis=1, keepdims=True)
    b = jnp.sum(lv[2:3] * lv[3:4], axis=1, keepdims=True)
    return jnp.exp(a) - jnp.exp(b) + lam_init


ATTN_HEADS_PER_STEP = 4
ONES_ROWS = SUBLANES


def _attn_kernel(qi_ref, kj_ref, ty_ref, fin_ref, q_ref, k_ref, vt_ref, bias_ref, lam_ref, ghn_ref, o_ref,
                 qm_sc, m_sc, acc_sc, *, n_types, lam_init):
    step = pl.program_id(2)
    ty = ty_ref[step]
    tq = q_ref.shape[1]
    heads = ATTN_HEADS_PER_STEP
    hl = lambda hh: slice(hh * LANES, (hh + 1) * LANES)

    @pl.when(kj_ref[step] == 0)
    def _():
        q = q_ref[0]
        lane = lax.broadcasted_iota(jnp.int32, (tq, LANES), 1)
        zero = jnp.zeros((tq, LANES), BF16)
        for hh in range(heads):
            qh = q[:, hl(hh)]
            qm_sc[hh, 0:tq, :] = jnp.where(lane < B_DK, qh, zero)
            qm_sc[hh, tq:2 * tq, :] = jnp.where(lane >= B_DK, qh, zero)
        m_sc[...] = jnp.full_like(m_sc, -jnp.inf)
        acc_sc[...] = jnp.zeros_like(acc_sc)

    def update(adj_of):
        ones = jnp.ones((ONES_ROWS, k_ref.shape[1]), BF16)
        scores = [_nt_dot(k_ref[0, :, hl(hh)], qm_sc[hh]) for hh in range(heads)]
        for hh, s in enumerate(scores):
            if adj_of is not None:
                adj = adj_of(hh)
                s = s + jnp.concatenate([adj, adj], axis=1)
            m_old = m_sc[hh]
            m_new = jnp.maximum(m_old, jnp.max(s, axis=0, keepdims=True))
            alpha = jnp.exp(m_old - m_new)
            p = jnp.exp(s - m_new).astype(BF16)
            vt1 = jnp.concatenate([vt_ref[0, hl(hh), :], ones], axis=0)
            acc_sc[hh] = alpha * acc_sc[hh] + jnp.dot(vt1, p, preferred_element_type=F32)
            m_sc[hh] = m_new

    @pl.when(ty < 0)
    def _():
        update(None)

    for t in range(n_types):
        @pl.when(ty == t)
        def _(t=t):
            update(lambda hh, t=t: bias_ref[hh, t])

    @pl.when(fin_ref[step] == 1)
    def _():
        lam = _lambda(lam_ref, lam_init)
        for hh in range(heads):
            acc = acc_sc[hh]
            both = acc[0:B_DV] / acc[B_DV:B_DV + 1]
            ot = both[:, 0:tq] - lam * both[:, tq:2 * tq]
            on = ot * lax.rsqrt(jnp.mean(ot * ot, axis=0, keepdims=True) + EPS) * ghn_ref[hl(hh), :] * (1.0 - lam_init)
            o_ref[0, :, hl(hh)] = on.T.astype(o_ref.dtype)


def _attn_schedule(T, tq, tk):
    offsets = sorted({qi * tq - kj * tk for qi in range(T // tq) for kj in range(T // tk)
                      if qi * tq + tq - 1 >= kj * tk and qi * tq - kj * tk - (tk - 1) < RPB_MAX_DIST})
    qi_l, kj_l, ty_l, fin_l = [], [], [], []
    for qi in range(T // tq):
        kjs = [kj for kj in range(T // tk) if qi * tq + tq - 1 >= kj * tk]
        for kj in kjs:
            off = qi * tq - kj * tk
            qi_l.append(qi)
            kj_l.append(kj)
            ty_l.append(offsets.index(off) if off in offsets else -1)
            fin_l.append(int(kj == kjs[-1]))
    as_i32 = lambda v: jnp.asarray(np.asarray(v, np.int32))
    return offsets, as_i32(qi_l), as_i32(kj_l), as_i32(ty_l), as_i32(fin_l)


def _bias_table(rpb):
    n = jnp.arange(RPB_MAX_DIST, dtype=jnp.int32)
    max_exact = RPB_BUCKETS // 2
    nf = jnp.maximum(n, 1).astype(F32)
    large = max_exact + (jnp.log(nf / max_exact) / math.log(RPB_MAX_DIST / max_exact)
                         * (RPB_BUCKETS - max_exact)).astype(jnp.int32)
    bucket = jnp.where(n < max_exact, n, jnp.minimum(large, RPB_BUCKETS - 1))
    return (rpb[bucket] - rpb[RPB_BUCKETS - 1][None, :]).T.astype(F32)


def _bias_of_distance(tbl, dist):
    d = np.asarray(dist)
    idx = jnp.asarray(np.clip(d, 0, RPB_MAX_DIST - 1).astype(np.int32))
    vals = jnp.take(tbl, idx, axis=1)
    vals = jnp.where(jnp.asarray(d >= RPB_MAX_DIST), 0.0, vals)
    return jnp.where(jnp.asarray(d < 0), -jnp.inf, vals)


def _bias_tiles_t(tbl, offsets, tq, tk):
    period = tq + tk
    w = np.arange(period)
    u = np.where(w < tq, w, w - period)
    vext = _bias_of_distance(tbl, np.stack([off + u for off in offsets]))
    H, n_types = vext.shape[:2]
    flat = jnp.broadcast_to(vext[:, :, None, :], (H, n_types, tk, period)).reshape(H, n_types, tk * period)
    return flat[:, :, :tk * (period - 1)].reshape(H, n_types, tk, period - 1)[:, :, :, :tq]


def _attn_prompt(q16, k16, vt16, tbl, lam, ghn_col, lam_init, tq, tk):
    B, T, W = q16.shape
    H = W // LANES
    hps = ATTN_HEADS_PER_STEP
    hw = hps * LANES
    assert H % hps == 0
    offsets, qi, kj, ty, fin = _attn_schedule(T, tq, tk)
    bias = _bias_tiles_t(tbl, offsets, tq, tk)
    kern = functools.partial(_attn_kernel, n_types=len(offsets), lam_init=lam_init)
    grid_spec = pltpu.PrefetchScalarGridSpec(
        num_scalar_prefetch=4,
        grid=(H // hps, B, int(qi.shape[0])),
        in_specs=[pl.BlockSpec((1, tq, hw), lambda h, b, s, qi, kj, ty, fin: (b, qi[s], h)),
                  pl.BlockSpec((1, tk, hw), lambda h, b, s, qi, kj, ty, fin: (b, kj[s], h)),
                  pl.BlockSpec((1, hw, tk), lambda h, b, s, qi, kj, ty, fin: (b, h, kj[s])),
                  pl.BlockSpec((hps, len(offsets), tk, tq), lambda h, b, s, *_: (h, 0, 0, 0)),
                  pl.BlockSpec(lam.shape, lambda h, b, s, *_: (0, 0)),
                  pl.BlockSpec((hw, 1), lambda h, b, s, *_: (h, 0))],
        out_specs=pl.BlockSpec((1, tq, hw), lambda h, b, s, qi, kj, ty, fin: (b, qi[s], h)),
        scratch_shapes=[pltpu.VMEM((hps, 2 * tq, LANES), BF16), pltpu.VMEM((hps, 1, 2 * tq), F32),
                        pltpu.VMEM((hps, B_DV + ONES_ROWS, 2 * tq), F32)],
    )
    return pl.pallas_call(
        kern,
        grid_spec=grid_spec,
        out_shape=jax.ShapeDtypeStruct((B, T, W), BF16),
        compiler_params=_params("parallel", "parallel", "arbitrary"),
        name="attn_prompt",
    )(qi, kj, ty, fin, q16, k16, vt16, bias, lam, ghn_col)


PAGES_PER_STEP = 4


def _attn_paged_kernel(pt_ref, q_ref, *refs, heads, t_new, page, n_steps, lam_init):
    pps = PAGES_PER_STEP
    kc_refs, vc_refs = refs[:pps], refs[pps:2 * pps]
    kn_ref, vn_ref, blast_ref, bnew_ref, lam_ref, ghn_ref, o_ref, qm_sc, m_sc, l_sc, acc_sc = refs[2 * pps:]
    j = pl.program_id(1)
    R = SUBLANES
    hsl = lambda h: slice(h * R, (h + 1) * R)
    lsl = lambda h: slice(h * LANES, (h + 1) * LANES)

    @pl.when(j == 0)
    def _():
        q = q_ref[0]
        row = lax.broadcasted_iota(jnp.int32, (R, LANES), 0)
        lane = lax.broadcasted_iota(jnp.int32, (R, LANES), 1)
        keep = (row < t_new) == (lane < B_DK)
        for h in range(heads):
            qm_sc[hsl(h), :] = jnp.where(keep, q[:, lsl(h)], 0.0)
        m_sc[...] = jnp.full_like(m_sc, -jnp.inf)
        l_sc[...] = jnp.zeros_like(l_sc)
        acc_sc[...] = jnp.zeros_like(acc_sc)

    qm = qm_sc[...].astype(BF16)

    def update(s, pv_of):
        m_old = m_sc[...]
        m_new = jnp.maximum(m_old, jnp.max(s, axis=1, keepdims=True))
        alpha = jnp.exp(m_old - m_new)
        p = jnp.exp(s - m_new)
        l_sc[...] = alpha * l_sc[...] + jnp.sum(p, axis=1, keepdims=True)
        pb = p.astype(BF16)
        acc_sc[...] = alpha * acc_sc[...] + jnp.concatenate([pv_of(h, pb[hsl(h)]) for h in range(heads)], axis=0)
        m_sc[...] = m_new

    s = jnp.concatenate(
        [jnp.concatenate([jnp.dot(qm[hsl(h)], kc[0, lsl(h), :].astype(BF16), preferred_element_type=F32)
                          for h in range(heads)], axis=0) for kc in kc_refs], axis=1)
    s = s + jnp.where(j == n_steps - 1, blast_ref[...], 0.0)

    def pv_cached(h, ph):
        parts = [jnp.dot(ph[:, u * page:(u + 1) * page], vc[0, pl.ds(h, page, stride=heads), :].astype(BF16),
                         preferred_element_type=F32) for u, vc in enumerate(vc_refs)]
        return functools.reduce(lambda a, b: a + b, parts)

    update(s, pv_cached)

    @pl.when(j == n_steps - 1)
    def _():
        kn = kn_ref[0].astype(BF16)
        vn = vn_ref[0].astype(BF16)
        s_new = jnp.concatenate([_nt_dot(qm[hsl(h)], kn[:, lsl(h)]) for h in range(heads)], axis=0) + bnew_ref[...]
        update(s_new, lambda h, ph: jnp.dot(ph, vn[:, lsl(h)], preferred_element_type=F32))
        lam = _lambda(lam_ref, lam_init)
        full = acc_sc[...] / l_sc[...]
        for h in range(heads):
            fh = full[hsl(h)]
            o = fh - lam * pltpu.roll(fh, R - t_new, 0)
            on = o * lax.rsqrt(jnp.mean(o * o, axis=1, keepdims=True) + EPS)
            o_ref[0, :, h * LANES:(h + 1) * LANES] = on * ghn_ref[:, h * LANES:(h + 1) * LANES] * (1.0 - lam_init)


def _attn_paged(q, cache_k, cache_v, page_table, k_new, v_new, tbl, lam, ghn, lam_init):
    B, t_new, W = q.shape
    H = W // LANES
    n_pool, page = cache_k.shape[:2]
    n_pages = page_table.shape[1]
    past = n_pages * page
    R = SUBLANES
    pps = PAGES_PER_STEP
    assert 2 * t_new == R and page >= RPB_MAX_DIST and n_pages % pps == 0
    n_steps = n_pages // pps
    pad = lambda a: jnp.concatenate([a, jnp.zeros((B, R - t_new, W), a.dtype)], axis=1)
    q8 = jnp.concatenate([q, q], axis=1)
    t = np.arange(R)[:, None] % t_new
    d_last = past + t - ((n_pages - 1) * page + np.arange(page)[None, :])
    c = np.arange(R)[None, :]
    d_new = np.where(c < t_new, t - c, -1)
    flat = lambda b: b.reshape(H * R, b.shape[-1])
    bias_last = jnp.pad(flat(_bias_of_distance(tbl, d_last)), ((0, 0), ((pps - 1) * page, 0)))
    bias_new = flat(_bias_of_distance(tbl, d_new))
    kern = functools.partial(_attn_paged_kernel, heads=H, t_new=t_new, page=page, n_steps=n_steps, lam_init=lam_init)
    page_spec = lambda rows, width, u: pl.BlockSpec(
        (1, rows, width), lambda b, j, pt: (pt[b * n_pages + j * pps + u], 0, 0))
    per_b = pl.BlockSpec((1, R, W), lambda b, j, pt: (b, 0, 0))
    full = lambda a: pl.BlockSpec(a.shape, lambda b, j, pt: (0,) * a.ndim)
    grid_spec = pltpu.PrefetchScalarGridSpec(
        num_scalar_prefetch=1,
        grid=(B, n_steps),
        in_specs=[per_b] + [page_spec(W, page, u) for u in range(pps)]
                 + [page_spec(page * H, B_DV, u) for u in range(pps)]
                 + [per_b, per_b, full(bias_last), full(bias_new), full(lam), full(ghn)],
        out_specs=per_b,
        scratch_shapes=[pltpu.VMEM((H * R, LANES), F32), pltpu.VMEM((H * R, 1), F32),
                        pltpu.VMEM((H * R, 1), F32), pltpu.VMEM((H * R, B_DV), F32)],
    )
    ck = jnp.transpose(cache_k, (0, 2, 3, 4, 1)).reshape(n_pool, W, page)
    cv = cache_v.reshape(n_pool, page * H, B_DV)
    out = pl.pallas_call(
        kern,
        grid_spec=grid_spec,
        out_shape=jax.ShapeDtypeStruct((B, R, W), F32),
        compiler_params=_params("parallel", "arbitrary"),
        name="attn_paged",
    )(page_table.reshape(-1), q8, *([ck] * pps), *([cv] * pps), pad(k_new), pad(v_new),
      bias_last, bias_new, lam, ghn)
    return out[:, :t_new]


def _trunk(x, mods, mods_kv, state, past, wts):
    B, T, D = x.shape
    N = B * T
    tm = ROW_TILE
    tb = min(MOE_BLOCK, N)
    mod = lambda m: _Mod(m, T, tm)
    split3 = lambda m: (mod(m[:, :D]), mod(m[:, D:2 * D]), mod(m[:, 2 * D:]))
    x2 = x.reshape(N, D)

    heads = A_HEADS
    inner = wts["w_out16"].shape[0]
    dh = inner // heads
    shift, scale, gate = split3(mods[0])
    proj, gates = _inproj(x2, shift, scale, wts["g_norm"][0, 0][None], wts["w_in16"], wts["w_gate"], tm)
    proj = proj.reshape(B, T, 4 * inner)
    gates = gates.reshape(B, T, LANES)
    conv_new = proj[:, T - (A_CONV - 1):, :2 * inner]
    if state is None:
        L, t_valid = math.gcd(T, MLSTM_CHUNK), None
        conv_init = jnp.zeros((B, SUBLANES, 2 * inner), F32)
        c0 = jnp.zeros((B, heads, dh, dh), F32)
        n0 = jnp.zeros((B, heads, 1, dh), F32)
        m0 = jnp.zeros((B, heads, 1, 1), F32)
    else:
        conv_st, c_st, n_st, m_st = state
        assert T <= SUBLANES
        L, t_valid = SUBLANES, T
        rows = lambda a: jnp.concatenate([a, jnp.zeros((B, L - T, a.shape[2]), a.dtype)], axis=1)
        proj, gates = rows(proj), rows(gates)
        conv_init = jnp.concatenate([jnp.zeros((B, SUBLANES - (A_CONV - 1), 2 * inner), F32), conv_st], axis=1)
        c0, n0, m0 = c_st, n_st[:, :, None, :], m_st[:, :, None, None]
    gates_t = jnp.swapaxes(gates[:, :, :SUBLANES], 1, 2)
    hs, c1, n1, m1 = _mlstm(proj, gates, gates_t, wts["bg"], wts["bgt"], conv_init, c0, n0, m0,
                            wts["w_conv"], wts["b_conv"], wts["g_hn_a"], L, t_valid)
    hs = hs[:, :T].reshape(N, inner)
    new_state = (conv_new, c1, n1[:, :, 0, :], m1[:, :, 0, 0])

    shift2, scale2, gate2 = split3(mods[1])
    x2, h16, rg = _proj_router(hs, wts["w_out16"], x2, gate, shift2, scale2, wts["g_norm"][0, 1][None],
                               wts["w_router"], wts["b_router"], tm)
    x2 = _moe(h16, rg, x2, _Mod(mods[1][:, 2 * D:], T, tb), wts["wg16"][0], wts["wu16"][0], wts["wd16"][0], tb)

    shift_kv, scale_kv = mod(mods_kv[:, :D]), mod(mods_kv[:, D:])
    shift, scale, gate = split3(mods[2])
    kvq = _kvq(x2, shift_kv, scale_kv, wts["g_kv"], shift, scale, wts["g_norm"][1, 0][None],
               wts["w_kv16"], wts["w_q16"], wts["gmat"], wts["gmat_t"], wts["g_kn"], wts["g_qn"], tm, T)
    v32 = kvq[1]
    W = v32.shape[1]
    H = W // LANES
    lam_init = 0.8 - 0.6 * math.exp(-0.3 * 1)
    if past is None:
        kt32, _, q, k16, vt16 = kvq
        o = _attn_prompt(q.reshape(B, T, W), k16.reshape(B, T, W), vt16, wts["rpb_tbl"], wts["lam"],
                         wts["g_hn_b"].reshape(W, 1), lam_init, math.gcd(T, ATTN_TQ), math.gcd(T, ATTN_TK))
        o = o.reshape(N, W)
        k_out = jnp.transpose(kt32.reshape(B, H, 2, B_DK, T), (0, 4, 1, 2, 3))
    else:
        k32, _, q = kvq
        k_out = k32.reshape(B, T, H, 2, B_DK)
        cache_k, cache_v, page_table = past
        o = _attn_paged(q.reshape(B, T, W), cache_k, cache_v, page_table, k32.reshape(B, T, W),
                        v32.reshape(B, T, W), wts["rpb_tbl"], wts["lam"], wts["g_hn_b"], lam_init)
        o = o.reshape(N, W).astype(BF16)

    shift2, scale2, gate2 = split3(mods[3])
    x2, h16, rg = _proj_router(o, wts["w_o16"], x2, gate, shift2, scale2, wts["g_norm"][1, 1][None],
                               wts["w_router"], wts["b_router"], tm)
    x2 = _moe(h16, rg, x2, _Mod(mods[3][:, 2 * D:], T, tb), wts["wg16"][1], wts["wu16"][1], wts["wd16"][1], tb)

    return x2.reshape(B, T, D), new_state, k_out, v32.reshape(B, T, H, B_DV)


def kernel(x_prompt, x_sample, c_prompt, c_sample, state_conv, state_C, state_n, state_m, cache_k, cache_v, page_table, w_ada, b_ada, g_norm, w_in_a, b_gate_a, w_conv_a, b_conv_a, g_hn_a, w_out_a, g_kv, w_ada_kv, b_ada_kv, w_kv, g_kn, w_q_b, g_qn_b, lam_b, g_hn_b, w_o_b, rpb, w_router, b_router, w_gate_e, w_up_e, w_down_e):
    Bp, Tp, D = x_prompt.shape
    Bs = x_sample.shape[0]
    inner = w_out_a.shape[1]
    heads_b = g_hn_b.shape[1]
    W = heads_b * B_DV

    n_c = Bp + Bs
    c_all = jnp.concatenate([c_prompt, c_sample, jnp.zeros((-n_c % SUBLANES, D), F32)], axis=0)
    mods = _ada(c_all, w_ada.reshape(-1, D, 3 * D), b_ada.reshape(-1, 1, 3 * D))
    mods_kv = _ada(c_all, w_ada_kv[None], b_ada_kv[None, None])[0]

    n_gate = 2 * A_HEADS
    group_of_lane = np.arange(W) // B_DK
    gmat = jnp.asarray((group_of_lane[:, None] == np.arange(LANES)[None, :]).astype(np.float32)).astype(BF16)
    wts = {
        "g_norm": g_norm,
        "w_in16": w_in_a[0][:, :4 * inner].astype(BF16),
        "w_gate": jnp.pad(w_in_a[0][:, 4 * inner:], ((0, 0), (0, LANES - n_gate))),
        "bg": jnp.pad(b_gate_a[0], (0, LANES - n_gate))[None, :],
        "bgt": b_gate_a[0][:, None],
        "w_conv": w_conv_a[0], "b_conv": b_conv_a[0][None, :],
        "g_hn_a": g_hn_a[0].reshape(1, inner),
        "w_out16": w_out_a[0].astype(BF16),
        "g_kv": g_kv[None, :],
        "w_kv16": w_kv.astype(BF16), "w_q16": w_q_b[0].astype(BF16),
        "gmat": gmat, "gmat_t": gmat.T,
        "g_kn": jnp.tile(g_kn.reshape(-1), heads_b)[None, :],
        "g_qn": jnp.tile(g_qn_b[0].reshape(-1), heads_b)[None, :],
        "lam": lam_b[0], "g_hn_b": g_hn_b[0].reshape(1, W),
        "w_o16": w_o_b[0].astype(BF16),
        "rpb_tbl": _bias_table(rpb),
        "w_router": jnp.pad(w_router, ((0, 0), (0, LANES - N_EXPERTS))),
        "b_router": b_router[:, None],
        "wg16": w_gate_e.astype(BF16), "wu16": w_up_e.astype(BF16), "wd16": w_down_e.astype(BF16),
    }

    y_p, st_p, k_p, v_p = _trunk(x_prompt, mods[:, :Bp], mods_kv[:Bp], None, None, wts)
    y_s, st_s, k_s, v_s = _trunk(x_sample, mods[:, Bp:n_c], mods_kv[Bp:n_c],
                                 (state_conv[0], state_C[0], state_n[0], state_m[0]),
                                 (cache_k, cache_v, page_table), wts)
    stack = lambda st: tuple(a[None] for a in st)
    return (y_p, y_s) + stack(st_p) + (k_p, v_p) + stack(st_s) + (k_s, v_s)
```

```python
import functools
import math

import numpy as np
import jax
import jax.numpy as jnp
from jax import lax
from jax.experimental import pallas as pl
from jax.experimental.pallas import tpu as pltpu

F32, BF16 = jnp.float32, jnp.bfloat16
HIGHEST = lax.Precision.HIGHEST
EPS = 1e-6

A_HEADS = 4
A_CONV = 4
B_DK = 64
B_DV = 128
N_EXPERTS = 16
N_GROUPS = 4
RPB_BUCKETS = 32
RPB_MAX_DIST = 128

LANES = 128
SUBLANES = 8
VMEM_LIMIT_BYTES = 56 * 1024 * 1024

ROW_TILE = 512
INPROJ_TILE = 1024
MLSTM_CHUNK = 256
ATTN_TQ = 512
ATTN_TK = 512


def _params(*sem):
    return pltpu.CompilerParams(dimension_semantics=sem, vmem_limit_bytes=VMEM_LIMIT_BYTES)


def _nt_dot(a, b):
    return lax.dot_general(a, b, (((1,), (1,)), ((), ())), preferred_element_type=F32)


def _tn_dot(a, b):
    return lax.dot_general(a, b, (((0,), (0,)), ((), ())), preferred_element_type=F32)


def _silu(x):
    return x * jax.nn.sigmoid(x)


def _rms_mod(x, g, scale, shift):
    y = x * lax.rsqrt(jnp.mean(x * x, axis=-1, keepdims=True) + EPS)
    return (y * g) * (1.0 + scale) + shift


def _dot_split(a, b16):
    hi = a.astype(BF16)
    lo = (a - hi.astype(F32)).astype(BF16)
    return (jnp.dot(hi, b16, preferred_element_type=F32) + jnp.dot(lo, b16, preferred_element_type=F32))


def _group_rms(x, gmat, gmat_t, g, group):
    ss = _dot_split(x * x, gmat)
    r = lax.rsqrt(ss * (1.0 / group) + EPS)
    rf = _dot_split(r, gmat_t)
    return x * rf * g


def _ada_kernel(c_ref, w_ref, b_ref, o_ref):
    a = _silu(c_ref[...])
    o_ref[0] = jnp.dot(a, w_ref[0], precision=HIGHEST, preferred_element_type=F32) + b_ref[0]


def _ada(c_all, w, b):
    S, D, Fo = w.shape
    R = c_all.shape[0]
    tn = 1024
    return pl.pallas_call(
        _ada_kernel,
        grid=(S, Fo // tn),
        in_specs=[pl.BlockSpec((R, D), lambda s, j: (0, 0)),
                  pl.BlockSpec((1, D, tn), lambda s, j: (s, 0, j)),
                  pl.BlockSpec((1, 1, tn), lambda s, j: (s, 0, j))],
        out_specs=pl.BlockSpec((1, R, tn), lambda s, j: (s, 0, j)),
        out_shape=jax.ShapeDtypeStruct((S, R, Fo), F32),
        compiler_params=_params("parallel", "parallel"),
        name="ada",
    )(c_all, w, b)


class _Mod:
    def __init__(self, m, T, tm):
        B, D = m.shape
        if T % tm == 0:
            self.arr, self.tiles_per_group = m[:, None, :], T // tm
        else:
            assert (B * T) % tm == 0
            self.arr, self.tiles_per_group = jnp.repeat(m, T, axis=0).reshape(-1, tm, D), 1

    def spec(self, grid_rank):
        R, D = self.arr.shape[1:]
        tpg = self.tiles_per_group
        if grid_rank == 1:
            return pl.BlockSpec((1, R, D), lambda i: (i // tpg, 0, 0))
        return pl.BlockSpec((1, R, D), lambda i, j: (i // tpg, 0, 0))


def _inproj_kernel(x_ref, sh_ref, sc_ref, g_ref, w_ref, wg_ref, o_ref, og_ref, h_sc):
    @pl.when(pl.program_id(1) == 0)
    def _():
        h = _rms_mod(x_ref[...], g_ref[...], sc_ref[0], sh_ref[0])
        h_sc[...] = h.astype(BF16)
        og_ref[...] = jnp.dot(h, wg_ref[...], precision=HIGHEST, preferred_element_type=F32)

    o_ref[...] = jnp.dot(h_sc[...], w_ref[...], preferred_element_type=F32)


def _inproj(x2, shift, scale, g, w16, wgate, tm):
    N, D = x2.shape
    Fo = w16.shape[1]
    tn = 2048
    return pl.pallas_call(
        _inproj_kernel,
        grid=(N // tm, Fo // tn),
        in_specs=[pl.BlockSpec((tm, D), lambda i, j: (i, 0)),
                  shift.spec(2), scale.spec(2),
                  pl.BlockSpec((1, D), lambda i, j: (0, 0)),
                  pl.BlockSpec((D, tn), lambda i, j: (0, j)),
                  pl.BlockSpec((D, LANES), lambda i, j: (0, 0))],
        out_specs=[pl.BlockSpec((tm, tn), lambda i, j: (i, j)),
                   pl.BlockSpec((tm, LANES), lambda i, j: (i, 0))],
        out_shape=[jax.ShapeDtypeStruct((N, Fo), F32), jax.ShapeDtypeStruct((N, LANES), F32)],
        scratch_shapes=[pltpu.VMEM((tm, D), BF16)],
        compiler_params=_params("parallel", "arbitrary"),
        name="mlstm_inproj",
    )(x2, shift.arr, scale.arr, g, w16, wgate)


def _mlstm_kernel(q_ref, k_ref, v_ref, o_ref, gt_ref, gtt_ref, bg_ref, bgt_ref, cinit_ref,
                  c0_ref, n0_ref, m0_ref, wconv_ref, bconv_ref, ghn_ref,
                  hs_ref, c_ref, n_ref, m_ref, tail_sc, *, L, dh, heads, t_valid):
    inner = heads * dh

    @pl.when(pl.program_id(1) == 0)
    def _():
        c_ref[...] = c0_ref[...]
        n_ref[...] = n0_ref[...]
        m_ref[...] = m0_ref[...]
        tail_sc[...] = cinit_ref[0]

    row8 = lax.broadcasted_iota(jnp.int32, (SUBLANES, dh), 0)

    def conv(x, tail, w, b):
        acc = b + x * w[A_CONV - 1:A_CONV]
        for s in range(1, A_CONV):
            xs = pltpu.roll(x, s, 0)
            top = jnp.where(row8 < s, pltpu.roll(tail, s, 0), xs[:SUBLANES])
            xs = top if L == SUBLANES else jnp.concatenate([top, xs[SUBLANES:]], axis=0)
            acc = acc + xs * w[A_CONV - 1 - s:A_CONV - s]
        return acc

    gt = gt_ref[0] + bg_ref[...]
    gtt = gtt_ref[0] + bgt_ref[...]
    ti = lax.broadcasted_iota(jnp.int32, (L, L), 0)
    si = lax.broadcasted_iota(jnp.int32, (L, L), 1)
    causal = si <= ti
    tcol = lax.broadcasted_iota(jnp.int32, (L, 1), 0)
    trow = lax.broadcasted_iota(jnp.int32, (1, L), 1)

    for h in range(heads):
        sl = slice(h * dh, (h + 1) * dh)
        slk = slice(inner + h * dh, inner + (h + 1) * dh)
        qh = _silu(conv(q_ref[0, :, sl], tail_sc[:, sl], wconv_ref[:, sl], bconv_ref[:, sl]))
        kh = _silu(conv(k_ref[0, :, sl], tail_sc[:, slk], wconv_ref[:, slk], bconv_ref[:, slk])) * (dh ** -0.5)
        vb = v_ref[0, :, sl].astype(BF16)

        ig_col = gt[:, h:h + 1]
        lf_col = jax.nn.log_sigmoid(gt[:, heads + h:heads + h + 1])
        ig_row = gtt[h:h + 1, :]
        lf_row = jax.nn.log_sigmoid(gtt[heads + h:heads + h + 1, :])
        if t_valid is not None:
            ig_col = jnp.where(tcol < t_valid, ig_col, -jnp.inf)
            lf_col = jnp.where(tcol < t_valid, lf_col, 0.0)
            ig_row = jnp.where(trow < t_valid, ig_row, -jnp.inf)
            lf_row = jnp.where(trow < t_valid, lf_row, 0.0)

        b_col = jnp.sum(jnp.where(causal, lf_row, 0.0), axis=1, keepdims=True)
        b_row = jnp.sum(jnp.where(ti <= si, lf_col, 0.0), axis=0, keepdims=True)
        dlog = jnp.where(causal, b_col - b_row + ig_row, -jnp.inf)
        g_col = b_col + m_ref[0, h]
        m_col = jnp.maximum(g_col, jnp.max(dlog, axis=1, keepdims=True))
        w_intra = jnp.exp(dlog - m_col)
        w_inter = jnp.exp(g_col - m_col)

        qb = qh.astype(BF16)
        kb = kh.astype(BF16)
        s = w_intra * _nt_dot(qb, kb)
        ch = c_ref[0, h]
        nh = n_ref[0, h]
        num = w_inter * _nt_dot(qb, ch.astype(BF16)) + jnp.dot(s.astype(BF16), vb, preferred_element_type=F32)
        den = w_inter * jnp.sum(qh * nh, axis=1, keepdims=True) + jnp.sum(s, axis=1, keepdims=True)
        hv = num / jnp.maximum(jnp.abs(den), jnp.exp(-m_col))

        m_end = m_col[L - 1:L]
        we_inter = jnp.exp(g_col[L - 1:L] - m_end)
        we_col = jnp.exp(b_col[L - 1:L] - b_col + ig_col - m_end)
        kw = kh * we_col
        c_ref[0, h] = we_inter * ch + _tn_dot(vb, kw.astype(BF16))
        n_ref[0, h] = we_inter * nh + jnp.sum(kw, axis=0, keepdims=True)
        m_ref[0, h] = m_end

        hn = hv * lax.rsqrt(jnp.mean(hv * hv, axis=1, keepdims=True) + EPS) * ghn_ref[:, sl]
        hs_ref[0, :, sl] = (jax.nn.sigmoid(o_ref[0, :, sl]) * hn).astype(hs_ref.dtype)

    tail_sc[:, :inner] = q_ref[0, L - SUBLANES:, :]
    tail_sc[:, inner:] = k_ref[0, L - SUBLANES:, :]


def _mlstm(proj, gates, gates_t, bg, bgt, conv_init, c0, n0, m0, wconv, bconv, ghn, L, t_valid):
    B, Tp, _ = proj.shape
    heads, dh = c0.shape[1], c0.shape[2]
    inner = heads * dh
    nc = Tp // L
    kern = functools.partial(_mlstm_kernel, L=L, dh=dh, heads=heads, t_valid=t_valid)
    col = lambda j: pl.BlockSpec((1, L, inner), lambda b, c: (b, c, j))
    full = lambda shape: pl.BlockSpec(shape, lambda b, c: (0,) * len(shape))
    per_b = lambda shape: pl.BlockSpec((1,) + shape, lambda b, c: (b,) + (0,) * len(shape))
    return pl.pallas_call(
        kern,
        grid=(B, nc),
        in_specs=[col(0), col(1), col(2), col(3),
                  pl.BlockSpec((1, L, LANES), lambda b, c: (b, c, 0)),
                  pl.BlockSpec((1, SUBLANES, L), lambda b, c: (b, 0, c)),
                  full((1, LANES)), full((SUBLANES, 1)),
                  per_b((SUBLANES, 2 * inner)),
                  per_b((heads, dh, dh)), per_b((heads, 1, dh)), per_b((heads, 1, 1)),
                  full((A_CONV, 2 * inner)), full((1, 2 * inner)), full((1, inner))],
        out_specs=[pl.BlockSpec((1, L, inner), lambda b, c: (b, c, 0)),
                   per_b((heads, dh, dh)), per_b((heads, 1, dh)), per_b((heads, 1, 1))],
        out_shape=[jax.ShapeDtypeStruct((B, Tp, inner), BF16),
                   jax.ShapeDtypeStruct((B, heads, dh, dh), F32),
                   jax.ShapeDtypeStruct((B, heads, 1, dh), F32),
                   jax.ShapeDtypeStruct((B, heads, 1, 1), F32)],
        scratch_shapes=[pltpu.VMEM((SUBLANES, 2 * inner), F32)],
        compiler_params=_params("parallel", "arbitrary"),
        name="mlstm",
    )(proj, proj, proj, proj, gates, gates_t, bg, bgt, conv_init, c0, n0, m0, wconv, bconv, ghn)


ROUTE_ROWS = SUBLANES


def _route(h, wr, br):
    tm = h.shape[0]
    per = N_EXPERTS // N_GROUPS
    logits = jnp.dot(h, wr, precision=HIGHEST, preferred_element_type=F32)
    lt = logits.T[:N_EXPERTS]
    s = jax.nn.sigmoid(lt)
    sel = s + br
    neg = jnp.full((1, tm), -jnp.inf, F32)
    izero = jnp.zeros((1, tm), jnp.int32)

    best_score = best_e1 = best_e2 = best_w1 = best_w2 = None
    for grp in range(N_GROUPS):
        rows = [sel[grp * per + j:grp * per + j + 1] for j in range(per)]
        srow = [s[grp * per + j:grp * per + j + 1] for j in range(per)]
        t1, i1, w1 = rows[0], izero, srow[0]
        for j in range(1, per):
            better = rows[j] > t1
            t1 = jnp.where(better, rows[j], t1)
            i1 = jnp.where(better, j, i1)
            w1 = jnp.where(better, srow[j], w1)
        t2, i2, w2 = neg, izero, srow[0]
        for j in range(per):
            better = jnp.where(i1 == j, neg, rows[j]) > t2
            t2 = jnp.where(better, rows[j], t2)
            i2 = jnp.where(better, j, i2)
            w2 = jnp.where(better, srow[j], w2)
        score = t1 + t2
        e1, e2 = i1 + grp * per, i2 + grp * per
        if grp == 0:
            best_score, best_e1, best_e2, best_w1, best_w2 = score, e1, e2, w1, w2
        else:
            better = score > best_score
            best_score = jnp.where(better, score, best_score)
            best_e1 = jnp.where(better, e1, best_e1)
            best_e2 = jnp.where(better, e2, best_e2)
            best_w1 = jnp.where(better, w1, best_w1)
            best_w2 = jnp.where(better, w2, best_w2)
    tot = best_w1 + best_w2
    zero = jnp.zeros((ROUTE_ROWS - 4, tm), F32)
    return jnp.concatenate([best_e1.astype(F32), best_e2.astype(F32), best_w1 / tot, best_w2 / tot, zero], axis=0)


def _proj_router_kernel(a_ref, w_ref, x_ref, gate_ref, sh_ref, sc_ref, g_ref, wr_ref, br_ref,
                        xo_ref, h_ref, route_ref):
    mix = jnp.dot(a_ref[...], w_ref[...], preferred_element_type=F32)
    x = x_ref[...] + gate_ref[0] * mix
    xo_ref[...] = x
    h = _rms_mod(x, g_ref[...], sc_ref[0], sh_ref[0])
    h_ref[...] = h.astype(BF16)
    route_ref[...] = _route(h, wr_ref[...], br_ref[...])


def _proj_router(a16, w16, x2, gate, shift, scale, g, wr, br, tm):
    N, D = x2.shape
    K = a16.shape[1]
    row = lambda w: pl.BlockSpec((tm, w), lambda i: (i, 0))
    full = lambda shape: pl.BlockSpec(shape, lambda i: (0,) * len(shape))
    return pl.pallas_call(
        _proj_router_kernel,
        grid=(N // tm,),
        in_specs=[row(K), full((K, D)), row(D), gate.spec(1), shift.spec(1), scale.spec(1),
                  full((1, D)), full((D, LANES)), full((N_EXPERTS, 1))],
        out_specs=[row(D), row(D), pl.BlockSpec((ROUTE_ROWS, tm), lambda i: (0, i))],
        out_shape=[jax.ShapeDtypeStruct((N, D), F32), jax.ShapeDtypeStruct((N, D), BF16),
                   jax.ShapeDtypeStruct((ROUTE_ROWS, N), F32)],
        compiler_params=_params("parallel"),
        name="proj_router",
    )(a16, w16, x2, gate.arr, shift.arr, scale.arr, g, wr, br)


MOE_BLOCK = 1024
MOE_WINDOW = 256
MOE_CHUNK = 256
SEG_ALIGN = 16


def _moe_kernel(route_ref, h_ref, x_ref, gate_ref, tri_ref, wg_ref, wu_ref, wd_ref, o_ref,
                xs_sc, ys_sc, gs_sc, tok_sc, seg_sc, *, n_experts, n_sorted):
    e = pl.program_id(1)
    tb = h_ref.shape[0]
    S, CH, RW = n_sorted, MOE_CHUNK, MOE_WINDOW
    one_hot = lambda a, b: jnp.where(a, 1.0, jnp.where(b, 1.0, 0.0))

    @pl.when(e == 0)
    def _dispatch():
        route = route_ref[...]
        e1, e2, w1, w2 = route[0:1], route[1:2], route[2:3], route[3:4]
        eid = lax.broadcasted_iota(jnp.int32, (n_experts, tb), 0).astype(F32)
        hit1, hit2 = eid == e1, eid == e2
        routed = one_hot(hit1, hit2)
        earlier = jnp.dot(routed.astype(BF16), tri_ref[...], preferred_element_type=F32)
        count = jnp.sum(routed, axis=1, keepdims=True)
        padded = jnp.floor((count + (SEG_ALIGN - 1)) * (1.0 / SEG_ALIGN)) * SEG_ALIGN
        below = jnp.where(lax.broadcasted_iota(jnp.int32, (n_experts, n_experts), 1)
                          < lax.broadcasted_iota(jnp.int32, (n_experts, n_experts), 0), 1.0, 0.0)
        start = jnp.dot(below, jnp.broadcast_to(padded, (n_experts, LANES)), precision=HIGHEST,
                        preferred_element_type=F32)[:, :1]
        pos = start + earlier
        pos1 = jnp.sum(jnp.where(hit1, pos, 0.0), axis=0, keepdims=True)
        pos2 = jnp.sum(jnp.where(hit2, pos, 0.0), axis=0, keepdims=True)
        for ex in range(n_experts):
            seg_sc[0, ex] = start[ex, 0].astype(jnp.int32)
            seg_sc[1, ex] = (start[ex, 0] + count[ex, 0]).astype(jnp.int32)
        tok_sc[...] = jnp.concatenate([pos1, pos2, jnp.zeros((LANES - 2, tb), F32)], axis=0).T

        def gather(c, carry):
            r0 = pl.multiple_of(c * CH, CH)
            row = (lax.broadcasted_iota(jnp.int32, (CH, tb), 0) + r0).astype(F32)
            is1, is2 = row == pos1, row == pos2
            xs_sc[pl.ds(r0, CH), :] = jnp.dot(one_hot(is1, is2).astype(BF16), h_ref[...],
                                              preferred_element_type=F32).astype(BF16)
            gs_sc[pl.ds(r0, CH), :] = jnp.sum(jnp.where(is1, w1, jnp.where(is2, w2, 0.0)), axis=1, keepdims=True)
            return carry

        lax.fori_loop(0, S // CH, gather, 0)
        xs_sc[S:, :] = jnp.zeros((RW, xs_sc.shape[1]), BF16)
        gs_sc[S:, :] = jnp.zeros((RW, 1), F32)
        ys_sc[...] = jnp.zeros_like(ys_sc)

    seg_start, seg_end = seg_sc[0, e], seg_sc[1, e]

    def window(w, carry):
        r0 = pl.multiple_of(seg_start + w * RW, SEG_ALIGN)
        rows = xs_sc[pl.ds(r0, RW), :]
        a = jnp.dot(rows, wg_ref[0], preferred_element_type=F32)
        u = jnp.dot(rows, wu_ref[0], preferred_element_type=F32)
        mine = lax.broadcasted_iota(jnp.int32, (RW, 1), 0) + r0 < seg_end
        act = jnp.where(mine, _silu(a) * u * gs_sc[pl.ds(r0, RW), :], 0.0)
        ys_sc[pl.ds(r0, RW), :] += jnp.dot(act.astype(BF16), wd_ref[0], preferred_element_type=F32)
        return carry

    lax.fori_loop(0, (seg_end - seg_start + (RW - 1)) // RW, window, 0)

    @pl.when(e == n_experts - 1)
    def _combine():
        def to_bf16(c, carry):
            r0 = pl.multiple_of(c * CH, CH)
            xs_sc[pl.ds(r0, CH), :] = ys_sc[pl.ds(r0, CH), :].astype(BF16)
            return carry

        lax.fori_loop(0, S // CH, to_bf16, 0)

        def scatter(c, carry):
            t0 = pl.multiple_of(c * CH, CH)
            rec = tok_sc[pl.ds(t0, CH), :]
            col = lax.broadcasted_iota(jnp.int32, (CH, S), 1).astype(F32)
            sel = one_hot(col == rec[:, 0:1], col == rec[:, 1:2]).astype(BF16)
            y = jnp.dot(sel, xs_sc[0:S, :], preferred_element_type=F32)
            g = gate_ref[0] if gate_ref.shape[1] == 1 else gate_ref[0, pl.ds(t0, CH), :]
            o_ref[pl.ds(t0, CH), :] = x_ref[pl.ds(t0, CH), :] + g * y
            return carry

        lax.fori_loop(0, tb // CH, scatter, 0)


def _moe(h16, route, x2, gate, wg16, wu16, wd16, layer, tb):
    N, D = x2.shape
    _, E, _, Fe = wg16.shape
    n_sorted = -(-(2 * tb + E * SEG_ALIGN) // MOE_CHUNK) * MOE_CHUNK
    tri = jnp.asarray(np.triu(np.ones((tb, tb), np.float32), 1), dtype=BF16)
    row = lambda w: pl.BlockSpec((tb, w), lambda i, e: (i, 0))
    kern = functools.partial(_moe_kernel, n_experts=E, n_sorted=n_sorted)
    return pl.pallas_call(
        kern,
        grid=(N // tb, E),
        in_specs=[pl.BlockSpec((ROUTE_ROWS, tb), lambda i, e: (0, i)), row(D), row(D), gate.spec(2),
                  pl.BlockSpec((tb, tb), lambda i, e: (0, 0)),
                  pl.BlockSpec((None, 1, D, Fe), lambda i, e: (layer, e, 0, 0)),
                  pl.BlockSpec((None, 1, D, Fe), lambda i, e: (layer, e, 0, 0)),
                  pl.BlockSpec((None, 1, Fe, D), lambda i, e: (layer, e, 0, 0))],
        out_specs=row(D),
        out_shape=jax.ShapeDtypeStruct((N, D), F32),
        scratch_shapes=[pltpu.VMEM((n_sorted + MOE_WINDOW, D), BF16), pltpu.VMEM((n_sorted + MOE_WINDOW, D), F32),
                        pltpu.VMEM((n_sorted + MOE_WINDOW, 1), F32), pltpu.VMEM((tb, LANES), F32),
                        pltpu.SMEM((2, E), jnp.int32)],
        compiler_params=_params("parallel", "arbitrary"),
        name="moe",
    )(route, h16, x2, gate.arr, tri, wg16, wu16, wd16)


def _kvq_kernel(x_ref, shk_ref, sck_ref, gk_ref, shq_ref, scq_ref, gq_ref, wkv_ref, wq_ref,
                gmat_ref, gmatt_ref, gkn_ref, gqn_ref, k_ref, v32_ref, q_ref, *maybe_attn_refs):
    x = x_ref[...]
    y = x * lax.rsqrt(jnp.mean(x * x, axis=-1, keepdims=True) + EPS)
    hk = ((y * gk_ref[...]) * (1.0 + sck_ref[0]) + shk_ref[0]).astype(BF16)
    hq = ((y * gq_ref[...]) * (1.0 + scq_ref[0]) + shq_ref[0]).astype(BF16)
    W = v32_ref.shape[1]
    kv = jnp.dot(hk, wkv_ref[...], preferred_element_type=F32)
    k = _group_rms(kv[:, :W], gmat_ref[...], gmatt_ref[...], gkn_ref[...], B_DK)
    v = kv[:, W:]
    v32_ref[...] = v
    q = jnp.dot(hq, wq_ref[...], preferred_element_type=F32)
    q = _group_rms(q, gmat_ref[...], gmatt_ref[...], gqn_ref[...], B_DK)
    q_ref[...] = (q * (B_DK ** -0.5)).astype(q_ref.dtype)
    if maybe_attn_refs:
        k16_ref, vt_ref = maybe_attn_refs
        k_ref[0] = k.T
        k16_ref[...] = k.astype(BF16)
        vt_ref[0] = v.T.astype(BF16)
    else:
        k_ref[...] = k


def _kvq(x2, shk, sck, gk, shq, scq, gq, wkv16, wq16, gmat, gmat_t, gkn, gqn, tm, seq_len):
    N, D = x2.shape
    W = wq16.shape[1]
    row = lambda w: pl.BlockSpec((tm, w), lambda i: (i, 0))
    full = lambda shape: pl.BlockSpec(shape, lambda i: (0,) * len(shape))
    if seq_len % tm == 0:
        tps = seq_len // tm
        col = pl.BlockSpec((1, W, tm), lambda i: (i // tps, 0, i % tps))
        out_specs = [col, row(W), row(W), row(W), col]
        out_shape = [jax.ShapeDtypeStruct((N // seq_len, W, seq_len), F32), jax.ShapeDtypeStruct((N, W), F32),
                     jax.ShapeDtypeStruct((N, W), BF16), jax.ShapeDtypeStruct((N, W), BF16),
                     jax.ShapeDtypeStruct((N // seq_len, W, seq_len), BF16)]
    else:
        out_specs = [row(W), row(W), row(W)]
        out_shape = [jax.ShapeDtypeStruct((N, W), F32)] * 3
    return pl.pallas_call(
        _kvq_kernel,
        grid=(N // tm,),
        in_specs=[row(D), shk.spec(1), sck.spec(1), full((1, D)), shq.spec(1), scq.spec(1), full((1, D)),
                  full((D, 2 * W)), full((D, W)), full((W, LANES)), full((LANES, W)),
                  full((1, W)), full((1, W))],
        out_specs=out_specs,
        out_shape=out_shape,
        compiler_params=_params("parallel"),
        name="kvq",
    )(x2, shk.arr, sck.arr, gk, shq.arr, scq.arr, gq, wkv16, wq16, gmat, gmat_t, gkn, gqn)


def _lambda(lam_ref, lam_init):
    lv = lam_ref[...]
    a = jnp.sum(lv[0:1] * lv[1:2], axis=1, keepdims=True)
    b = jnp.sum(lv[2:3] * lv[3:4], axis=1, keepdims=True)
    return jnp.exp(a) - jnp.exp(b) + lam_init


ATTN_HEADS_PER_STEP = 4
ONES_ROWS = SUBLANES


def _attn_kernel(qi_ref, kj_ref, ty_ref, fin_ref, q_ref, k_ref, vt_ref, bias_ref, lam_ref, ghn_ref, o_ref,
                 qm_sc, m_sc, acc_sc, *, n_types, lam_init):
    step = pl.program_id(2)
    ty = ty_ref[step]
    tq = q_ref.shape[1]
    heads = ATTN_HEADS_PER_STEP
    hl = lambda hh: slice(hh * LANES, (hh + 1) * LANES)

    @pl.when(kj_ref[step] == 0)
    def _():
        q = q_ref[0]
        lane = lax.broadcasted_iota(jnp.int32, (tq, LANES), 1)
        zero = jnp.zeros((tq, LANES), BF16)
        for hh in range(heads):
            qh = q[:, hl(hh)]
            qm_sc[hh, 0:tq, :] = jnp.where(lane < B_DK, qh, zero)
            qm_sc[hh, tq:2 * tq, :] = jnp.where(lane >= B_DK, qh, zero)
        m_sc[...] = jnp.full_like(m_sc, -jnp.inf)
        acc_sc[...] = jnp.zeros_like(acc_sc)

    def update(adj_of):
        ones = jnp.ones((ONES_ROWS, k_ref.shape[1]), BF16)

        def scores(hh):
            return _nt_dot(k_ref[0, :, hl(hh)], qm_sc[hh])

        def softmax(hh, s):
            if adj_of is not None:
                adj = adj_of(hh)
                s = s + jnp.concatenate([adj, adj], axis=1)
            m_old = m_sc[hh]
            m_new = jnp.maximum(m_old, jnp.max(s, axis=0, keepdims=True))
            m_sc[hh] = m_new
            return jnp.exp(m_old - m_new), jnp.exp(s - m_new).astype(BF16)

        def weighted_values(hh, alpha, p):
            vt1 = jnp.concatenate([vt_ref[0, hl(hh), :], ones], axis=0)
            acc_sc[hh] = alpha * acc_sc[hh] + jnp.dot(vt1, p, preferred_element_type=F32)

        s_of, ap_of = {}, {}
        for t in range(heads + 3):
            if t - 3 >= 0:
                weighted_values(t - 3, *ap_of.pop(t - 3))
            if 0 <= t - 2 < heads:
                ap_of[t - 2] = softmax(t - 2, s_of.pop(t - 2))
            if t < heads:
                s_of[t] = scores(t)

    @pl.when(ty < 0)
    def _():
        update(None)

    for t in range(n_types):
        @pl.when(ty == t)
        def _(t=t):
            update(lambda hh, t=t: bias_ref[hh, t])

    @pl.when(fin_ref[step] == 1)
    def _():
        lam = _lambda(lam_ref, lam_init)
        for hh in range(heads):
            acc = acc_sc[hh]
            both = acc[0:B_DV] / acc[B_DV:B_DV + 1]
            ot = both[:, 0:tq] - lam * both[:, tq:2 * tq]
            on = ot * lax.rsqrt(jnp.mean(ot * ot, axis=0, keepdims=True) + EPS) * ghn_ref[hl(hh), :] * (1.0 - lam_init)
            o_ref[0, :, hl(hh)] = on.T.astype(o_ref.dtype)


def _attn_schedule(T, tq, tk):
    offsets = sorted({qi * tq - kj * tk for qi in range(T // tq) for kj in range(T // tk)
                      if qi * tq + tq - 1 >= kj * tk and qi * tq - kj * tk - (tk - 1) < RPB_MAX_DIST})
    qi_l, kj_l, ty_l, fin_l = [], [], [], []
    for qi in range(T // tq):
        kjs = [kj for kj in range(T // tk) if qi * tq + tq - 1 >= kj * tk]
        for kj in kjs:
            off = qi * tq - kj * tk
            qi_l.append(qi)
            kj_l.append(kj)
            ty_l.append(offsets.index(off) if off in offsets else -1)
            fin_l.append(int(kj == kjs[-1]))
    as_i32 = lambda v: jnp.asarray(np.asarray(v, np.int32))
    return offsets, as_i32(qi_l), as_i32(kj_l), as_i32(ty_l), as_i32(fin_l)


def _bias_table(rpb):
    n = jnp.arange(RPB_MAX_DIST, dtype=jnp.int32)
    max_exact = RPB_BUCKETS // 2
    nf = jnp.maximum(n, 1).astype(F32)
    large = max_exact + (jnp.log(nf / max_exact) / math.log(RPB_MAX_DIST / max_exact)
                         * (RPB_BUCKETS - max_exact)).astype(jnp.int32)
    bucket = jnp.where(n < max_exact, n, jnp.minimum(large, RPB_BUCKETS - 1))
    return (rpb[bucket] - rpb[RPB_BUCKETS - 1][None, :]).T.astype(F32)


def _bias_of_distance(tbl, dist):
    d = np.asarray(dist)
    idx = jnp.asarray(np.clip(d, 0, RPB_MAX_DIST - 1).astype(np.int32))
    vals = jnp.take(tbl, idx, axis=1)
    vals = jnp.where(jnp.asarray(d >= RPB_MAX_DIST), 0.0, vals)
    return jnp.where(jnp.asarray(d < 0), -jnp.inf, vals)


def _bias_tiles_t(tbl, offsets, tq, tk):
    period = tq + tk
    w = np.arange(period)
    u = np.where(w < tq, w, w - period)
    vext = _bias_of_distance(tbl, np.stack([off + u for off in offsets]))
    H, n_types = vext.shape[:2]
    flat = jnp.broadcast_to(vext[:, :, None, :], (H, n_types, tk, period)).reshape(H, n_types, tk * period)
    return flat[:, :, :tk * (period - 1)].reshape(H, n_types, tk, period - 1)[:, :, :, :tq]


def _attn_prompt(q16, k16, vt16, tbl, lam, ghn_col, lam_init, tq, tk):
    B, T, W = q16.shape
    H = W // LANES
    hps = ATTN_HEADS_PER_STEP
    hw = hps * LANES
    assert H % hps == 0
    offsets, qi, kj, ty, fin = _attn_schedule(T, tq, tk)
    bias = _bias_tiles_t(tbl, offsets, tq, tk)
    kern = functools.partial(_attn_kernel, n_types=len(offsets), lam_init=lam_init)
    grid_spec = pltpu.PrefetchScalarGridSpec(
        num_scalar_prefetch=4,
        grid=(H // hps, B, int(qi.shape[0])),
        in_specs=[pl.BlockSpec((1, tq, hw), lambda h, b, s, qi, kj, ty, fin: (b, qi[s], h)),
                  pl.BlockSpec((1, tk, hw), lambda h, b, s, qi, kj, ty, fin: (b, kj[s], h)),
                  pl.BlockSpec((1, hw, tk), lambda h, b, s, qi, kj, ty, fin: (b, h, kj[s])),
                  pl.BlockSpec((hps, len(offsets), tk, tq), lambda h, b, s, *_: (h, 0, 0, 0)),
                  pl.BlockSpec(lam.shape, lambda h, b, s, *_: (0, 0)),
                  pl.BlockSpec((hw, 1), lambda h, b, s, *_: (h, 0))],
        out_specs=pl.BlockSpec((1, tq, hw), lambda h, b, s, qi, kj, ty, fin: (b, qi[s], h)),
        scratch_shapes=[pltpu.VMEM((hps, 2 * tq, LANES), BF16), pltpu.VMEM((hps, 1, 2 * tq), F32),
                        pltpu.VMEM((hps, B_DV + ONES_ROWS, 2 * tq), F32)],
    )
    return pl.pallas_call(
        kern,
        grid_spec=grid_spec,
        out_shape=jax.ShapeDtypeStruct((B, T, W), BF16),
        compiler_params=_params("parallel", "parallel", "arbitrary"),
        name="attn_prompt",
    )(qi, kj, ty, fin, q16, k16, vt16, bias, lam, ghn_col)


PAGES_PER_STEP = 8


def _attn_paged_kernel(pt_ref, q_ref, *refs, heads, t_new, page, n_steps, lam_init):
    pps = PAGES_PER_STEP
    kc_refs, vc_refs = refs[:pps], refs[pps:2 * pps]
    kn_ref, vn_ref, blast_ref, bnew_ref, lam_ref, ghn_ref, o_ref, qm_sc, m_sc, l_sc, acc_sc = refs[2 * pps:]
    j = pl.program_id(1)
    R = SUBLANES
    hsl = lambda h: slice(h * R, (h + 1) * R)
    lsl = lambda h: slice(h * LANES, (h + 1) * LANES)

    @pl.when(j == 0)
    def _():
        q = q_ref[0]
        row = lax.broadcasted_iota(jnp.int32, (R, LANES), 0)
        lane = lax.broadcasted_iota(jnp.int32, (R, LANES), 1)
        keep = (row < t_new) == (lane < B_DK)
        for h in range(heads):
            qm_sc[hsl(h), :] = jnp.where(keep, q[:, lsl(h)], 0.0)
        m_sc[...] = jnp.full_like(m_sc, -jnp.inf)
        l_sc[...] = jnp.zeros_like(l_sc)
        acc_sc[...] = jnp.zeros_like(acc_sc)

    qm = qm_sc[...].astype(BF16)

    def update(s, pv_of):
        m_old = m_sc[...]
        m_new = jnp.maximum(m_old, jnp.max(s, axis=1, keepdims=True))
        alpha = jnp.exp(m_old - m_new)
        p = jnp.exp(s - m_new)
        l_sc[...] = alpha * l_sc[...] + jnp.sum(p, axis=1, keepdims=True)
        pb = p.astype(BF16)
        acc_sc[...] = alpha * acc_sc[...] + jnp.concatenate([pv_of(h, pb[hsl(h)]) for h in range(heads)], axis=0)
        m_sc[...] = m_new

    s = jnp.concatenate(
        [jnp.concatenate([jnp.dot(qm[hsl(h)], kc[0, lsl(h), :].astype(BF16), preferred_element_type=F32)
                          for h in range(heads)], axis=0) for kc in kc_refs], axis=1)
    s = s + jnp.where(j == n_steps - 1, blast_ref[...], 0.0)

    def pv_cached(h, ph):
        parts = [jnp.dot(ph[:, u * page:(u + 1) * page], vc[0, pl.ds(h, page, stride=heads), :].astype(BF16),
                         preferred_element_type=F32) for u, vc in enumerate(vc_refs)]
        return functools.reduce(lambda a, b: a + b, parts)

    update(s, pv_cached)

    @pl.when(j == n_steps - 1)
    def _():
        kn = kn_ref[0].astype(BF16)
        vn = vn_ref[0].astype(BF16)
        s_new = jnp.concatenate([_nt_dot(qm[hsl(h)], kn[:, lsl(h)]) for h in range(heads)], axis=0) + bnew_ref[...]
        update(s_new, lambda h, ph: jnp.dot(ph, vn[:, lsl(h)], preferred_element_type=F32))
        lam = _lambda(lam_ref, lam_init)
        full = acc_sc[...] / l_sc[...]
        for h in range(heads):
            fh = full[hsl(h)]
            o = fh - lam * pltpu.roll(fh, R - t_new, 0)
            on = o * lax.rsqrt(jnp.mean(o * o, axis=1, keepdims=True) + EPS)
            o_ref[0, :, h * LANES:(h + 1) * LANES] = on * ghn_ref[:, h * LANES:(h + 1) * LANES] * (1.0 - lam_init)


def _attn_paged(q, cache_k, cache_v, page_table, k_new, v_new, tbl, lam, ghn, lam_init):
    B, t_new, W = q.shape
    H = W // LANES
    n_pool, page = cache_k.shape[:2]
    n_pages = page_table.shape[1]
    past = n_pages * page
    R = SUBLANES
    pps = PAGES_PER_STEP
    assert 2 * t_new == R and page >= RPB_MAX_DIST and n_pages % pps == 0
    n_steps = n_pages // pps
    pad = lambda a: jnp.concatenate([a, jnp.zeros((B, R - t_new, W), a.dtype)], axis=1)
    q8 = jnp.concatenate([q, q], axis=1)
    t = np.arange(R)[:, None] % t_new
    d_last = past + t - ((n_pages - 1) * page + np.arange(page)[None, :])
    c = np.arange(R)[None, :]
    d_new = np.where(c < t_new, t - c, -1)
    flat = lambda b: b.reshape(H * R, b.shape[-1])
    bias_last = jnp.pad(flat(_bias_of_distance(tbl, d_last)), ((0, 0), ((pps - 1) * page, 0)))
    bias_new = flat(_bias_of_distance(tbl, d_new))
    kern = functools.partial(_attn_paged_kernel, heads=H, t_new=t_new, page=page, n_steps=n_steps, lam_init=lam_init)
    page_spec = lambda rows, width, u: pl.BlockSpec(
        (1, rows, width), lambda b, j, pt: (pt[b * n_pages + j * pps + u], 0, 0))
    per_b = pl.BlockSpec((1, R, W), lambda b, j, pt: (b, 0, 0))
    full = lambda a: pl.BlockSpec(a.shape, lambda b, j, pt: (0,) * a.ndim)
    grid_spec = pltpu.PrefetchScalarGridSpec(
        num_scalar_prefetch=1,
        grid=(B, n_steps),
        in_specs=[per_b] + [page_spec(W, page, u) for u in range(pps)]
                 + [page_spec(page * H, B_DV, u) for u in range(pps)]
                 + [per_b, per_b, full(bias_last), full(bias_new), full(lam), full(ghn)],
        out_specs=per_b,
        scratch_shapes=[pltpu.VMEM((H * R, LANES), F32), pltpu.VMEM((H * R, 1), F32),
                        pltpu.VMEM((H * R, 1), F32), pltpu.VMEM((H * R, B_DV), F32)],
    )
    ck = jnp.transpose(cache_k, (0, 2, 3, 4, 1)).reshape(n_pool, W, page)
    cv = cache_v.reshape(n_pool, page * H, B_DV)
    out = pl.pallas_call(
        kern,
        grid_spec=grid_spec,
        out_shape=jax.ShapeDtypeStruct((B, R, W), F32),
        compiler_params=_params("parallel", "arbitrary"),
        name="attn_paged",
    )(page_table.reshape(-1), q8, *([ck] * pps), *([cv] * pps), pad(k_new), pad(v_new),
      bias_last, bias_new, lam, ghn)
    return out[:, :t_new]


def _trunk(x, mods, mods_kv, state, past, wts):
    B, T, D = x.shape
    N = B * T
    tm = ROW_TILE
    tb = min(MOE_BLOCK, N)
    mod = lambda m: _Mod(m, T, tm)
    split3 = lambda m: (mod(m[:, :D]), mod(m[:, D:2 * D]), mod(m[:, 2 * D:]))
    x2 = x.reshape(N, D)

    heads = A_HEADS
    inner = wts["w_out16"].shape[0]
    dh = inner // heads
    gate = mod(mods[0][:, 2 * D:])
    if state is None:
        t_rows, x_in = T, x2
    else:
        assert T <= SUBLANES
        t_rows = SUBLANES
        x_in = jnp.concatenate([x, jnp.zeros((B, t_rows - T, D), x.dtype)], axis=1).reshape(B * t_rows, D)
    t_in = min(INPROJ_TILE, B * t_rows)
    proj, gates = _inproj(x_in, _Mod(mods[0][:, :D], t_rows, t_in), _Mod(mods[0][:, D:2 * D], t_rows, t_in),
                          wts["g_norm"][0, 0][None], wts["w_in16"], wts["w_gate"], t_in)
    proj = proj.reshape(B, t_rows, 4 * inner)
    gates = gates.reshape(B, t_rows, LANES)
    conv_new = proj[:, T - (A_CONV - 1):T, :2 * inner]
    if state is None:
        L, t_valid = math.gcd(T, MLSTM_CHUNK), None
        conv_init = jnp.zeros((B, SUBLANES, 2 * inner), F32)
        c0 = jnp.zeros((B, heads, dh, dh), F32)
        n0 = jnp.zeros((B, heads, 1, dh), F32)
        m0 = jnp.zeros((B, heads, 1, 1), F32)
    else:
        conv_st, c_st, n_st, m_st = state
        L, t_valid = SUBLANES, T
        conv_init = jnp.concatenate([jnp.zeros((B, SUBLANES - (A_CONV - 1), 2 * inner), F32), conv_st], axis=1)
        c0, n0, m0 = c_st, n_st[:, :, None, :], m_st[:, :, None, None]
    gates_t = jnp.swapaxes(gates[:, :, :SUBLANES], 1, 2)
    hs, c1, n1, m1 = _mlstm(proj, gates, gates_t, wts["bg"], wts["bgt"], conv_init, c0, n0, m0,
                            wts["w_conv"], wts["b_conv"], wts["g_hn_a"], L, t_valid)
    hs = hs[:, :T].reshape(N, inner)
    new_state = (conv_new, c1, n1[:, :, 0, :], m1[:, :, 0, 0])

    shift2, scale2, gate2 = split3(mods[1])
    x2, h16, rg = _proj_router(hs, wts["w_out16"], x2, gate, shift2, scale2, wts["g_norm"][0, 1][None],
                               wts["w_router"], wts["b_router"], tm)
    x2 = _moe(h16, rg, x2, _Mod(mods[1][:, 2 * D:], T, tb), wts["wg16"], wts["wu16"], wts["wd16"], 0, tb)

    shift_kv, scale_kv = mod(mods_kv[:, :D]), mod(mods_kv[:, D:])
    shift, scale, gate = split3(mods[2])
    kvq = _kvq(x2, shift_kv, scale_kv, wts["g_kv"], shift, scale, wts["g_norm"][1, 0][None],
               wts["w_kv16"], wts["w_q16"], wts["gmat"], wts["gmat_t"], wts["g_kn"], wts["g_qn"], tm, T)
    v32 = kvq[1]
    W = v32.shape[1]
    H = W // LANES
    lam_init = 0.8 - 0.6 * math.exp(-0.3 * 1)
    if past is None:
        kt32, _, q, k16, vt16 = kvq
        o = _attn_prompt(q.reshape(B, T, W), k16.reshape(B, T, W), vt16, wts["rpb_tbl"], wts["lam"],
                         wts["g_hn_b"].reshape(W, 1), lam_init, math.gcd(T, ATTN_TQ), math.gcd(T, ATTN_TK))
        o = o.reshape(N, W)
        k_out = jnp.transpose(kt32.reshape(B, H, 2, B_DK, T), (0, 4, 1, 2, 3))
    else:
        k32, _, q = kvq
        k_out = k32.reshape(B, T, H, 2, B_DK)
        cache_k, cache_v, page_table = past
        o = _attn_paged(q.reshape(B, T, W), cache_k, cache_v, page_table, k32.reshape(B, T, W),
                        v32.reshape(B, T, W), wts["rpb_tbl"], wts["lam"], wts["g_hn_b"], lam_init)
        o = o.reshape(N, W).astype(BF16)

    shift2, scale2, gate2 = split3(mods[3])
    x2, h16, rg = _proj_router(o, wts["w_o16"], x2, gate, shift2, scale2, wts["g_norm"][1, 1][None],
                               wts["w_router"], wts["b_router"], tm)
    x2 = _moe(h16, rg, x2, _Mod(mods[3][:, 2 * D:], T, tb), wts["wg16"], wts["wu16"], wts["wd16"], 1, tb)

    return x2.reshape(B, T, D), new_state, k_out, v32.reshape(B, T, H, B_DV)


def kernel(x_prompt, x_sample, c_prompt, c_sample, state_conv, state_C, state_n, state_m, cache_k, cache_v, page_table, w_ada, b_ada, g_norm, w_in_a, b_gate_a, w_conv_a, b_conv_a, g_hn_a, w_out_a, g_kv, w_ada_kv, b_ada_kv, w_kv, g_kn, w_q_b, g_qn_b, lam_b, g_hn_b, w_o_b, rpb, w_router, b_router, w_gate_e, w_up_e, w_down_e):
    Bp, Tp, D = x_prompt.shape
    Bs = x_sample.shape[0]
    inner = w_out_a.shape[1]
    heads_b = g_hn_b.shape[1]
    W = heads_b * B_DV

    n_c = Bp + Bs
    c_all = jnp.concatenate([c_prompt, c_sample, jnp.zeros((-n_c % SUBLANES, D), F32)], axis=0)
    mods = _ada(c_all, w_ada.reshape(-1, D, 3 * D), b_ada.reshape(-1, 1, 3 * D))
    mods_kv = _ada(c_all, w_ada_kv[None], b_ada_kv[None, None])[0]

    n_gate = 2 * A_HEADS
    group_of_lane = np.arange(W) // B_DK
    gmat = jnp.asarray((group_of_lane[:, None] == np.arange(LANES)[None, :]).astype(np.float32)).astype(BF16)
    wts = {
        "g_norm": g_norm,
        "w_in16": w_in_a[0, :, :4 * inner].astype(BF16),
        "w_gate": jnp.pad(w_in_a[0][:, 4 * inner:], ((0, 0), (0, LANES - n_gate))),
        "bg": jnp.pad(b_gate_a[0], (0, LANES - n_gate))[None, :],
        "bgt": b_gate_a[0][:, None],
        "w_conv": w_conv_a[0], "b_conv": b_conv_a[0][None, :],
        "g_hn_a": g_hn_a[0].reshape(1, inner),
        "w_out16": w_out_a[0].astype(BF16),
        "g_kv": g_kv[None, :],
        "w_kv16": w_kv.astype(BF16), "w_q16": w_q_b[0].astype(BF16),
        "gmat": gmat, "gmat_t": gmat.T,
        "g_kn": jnp.tile(g_kn.reshape(-1), heads_b)[None, :],
        "g_qn": jnp.tile(g_qn_b[0].reshape(-1), heads_b)[None, :],
        "lam": lam_b[0], "g_hn_b": g_hn_b[0].reshape(1, W),
        "w_o16": w_o_b[0].astype(BF16),
        "rpb_tbl": _bias_table(rpb),
        "w_router": jnp.pad(w_router, ((0, 0), (0, LANES - N_EXPERTS))),
        "b_router": b_router[:, None],
        "wg16": w_gate_e.astype(BF16), "wu16": w_up_e.astype(BF16), "wd16": w_down_e.astype(BF16),
    }

    y_p, st_p, k_p, v_p = _trunk(x_prompt, mods[:, :Bp], mods_kv[:Bp], None, None, wts)
    y_s, st_s, k_s, v_s = _trunk(x_sample, mods[:, Bp:n_c], mods_kv[Bp:n_c],
                                 (state_conv[0], state_C[0], state_n[0], state_m[0]),
                                 (cache_k, cache_v, page_table), wts)
    stack = lambda st: tuple(a[None] for a in st)
    return (y_p, y_s) + stack(st_p) + (k_p, v_p) + stack(st_s) + (k_s, v_s)
```

```python
import functools
import math

import numpy as np
import jax
import jax.numpy as jnp
from jax import lax
from jax.experimental import pallas as pl
from jax.experimental.pallas import tpu as pltpu

F32, BF16 = jnp.float32, jnp.bfloat16
HIGHEST = lax.Precision.HIGHEST
EPS = 1e-6

A_HEADS = 4
A_CONV = 4
B_DK = 64
B_DV = 128
N_EXPERTS = 16
N_GROUPS = 4
RPB_BUCKETS = 32
RPB_MAX_DIST = 128

LANES = 128
SUBLANES = 8
VMEM_LIMIT_BYTES = 56 * 1024 * 1024

ROW_TILE = 512
INPROJ_TILE = 1024
ROUTER_TILE = 1024
MLSTM_CHUNK = 256
ATTN_TQ = 512
ATTN_TK = 512


def _params(*sem):
    return pltpu.CompilerParams(dimension_semantics=sem, vmem_limit_bytes=VMEM_LIMIT_BYTES)


def _nt_dot(a, b):
    return lax.dot_general(a, b, (((1,), (1,)), ((), ())), preferred_element_type=F32)


def _tn_dot(a, b):
    return lax.dot_general(a, b, (((0,), (0,)), ((), ())), preferred_element_type=F32)


def _silu(x):
    return x * jax.nn.sigmoid(x)


def _rms_mod(x, g, scale, shift):
    y = x * lax.rsqrt(jnp.mean(x * x, axis=-1, keepdims=True) + EPS)
    return (y * g) * (1.0 + scale) + shift


def _dot_split(a, b16):
    hi = a.astype(BF16)
    lo = (a - hi.astype(F32)).astype(BF16)
    return (jnp.dot(hi, b16, preferred_element_type=F32) + jnp.dot(lo, b16, preferred_element_type=F32))


def _group_rms(x, gmat, gmat_t, g, group):
    ss = _dot_split(x * x, gmat)
    r = lax.rsqrt(ss * (1.0 / group) + EPS)
    rf = _dot_split(r, gmat_t)
    return x * rf * g


def _ada_kernel(c_ref, w_ref, b_ref, o_ref):
    a = _silu(c_ref[...])
    o_ref[0] = jnp.dot(a, w_ref[0], precision=HIGHEST, preferred_element_type=F32) + b_ref[0]


def _ada(c_all, w, b):
    S, D, Fo = w.shape
    R = c_all.shape[0]
    tn = 1024
    return pl.pallas_call(
        _ada_kernel,
        grid=(S, Fo // tn),
        in_specs=[pl.BlockSpec((R, D), lambda s, j: (0, 0)),
                  pl.BlockSpec((1, D, tn), lambda s, j: (s, 0, j)),
                  pl.BlockSpec((1, 1, tn), lambda s, j: (s, 0, j))],
        out_specs=pl.BlockSpec((1, R, tn), lambda s, j: (s, 0, j)),
        out_shape=jax.ShapeDtypeStruct((S, R, Fo), F32),
        compiler_params=_params("parallel", "parallel"),
        name="ada",
    )(c_all, w, b)


class _Mod:
    def __init__(self, m, T, tm):
        B, D = m.shape
        if T % tm == 0:
            self.arr, self.tiles_per_group = m[:, None, :], T // tm
        else:
            assert (B * T) % tm == 0
            self.arr, self.tiles_per_group = jnp.repeat(m, T, axis=0).reshape(-1, tm, D), 1

    def spec(self, grid_rank):
        R, D = self.arr.shape[1:]
        tpg = self.tiles_per_group
        if grid_rank == 1:
            return pl.BlockSpec((1, R, D), lambda i: (i // tpg, 0, 0))
        return pl.BlockSpec((1, R, D), lambda i, j: (i // tpg, 0, 0))


def _causal_conv(x, tail, w, b):
    L, width = x.shape
    row8 = lax.broadcasted_iota(jnp.int32, (SUBLANES, width), 0)
    acc = b + x * w[A_CONV - 1:A_CONV]
    for s in range(1, A_CONV):
        xs = pltpu.roll(x, s, 0)
        top = jnp.where(row8 < s, pltpu.roll(tail, s, 0), xs[:SUBLANES])
        xs = top if L == SUBLANES else jnp.concatenate([top, xs[SUBLANES:]], axis=0)
        acc = acc + xs * w[A_CONV - 1 - s:A_CONV - s]
    return acc


def _inproj_kernel(x_ref, sh_ref, sc_ref, g_ref, w_ref, wg_ref, o_ref, og_ref, h_sc):
    @pl.when(pl.program_id(1) == 0)
    def _():
        h = _rms_mod(x_ref[...], g_ref[...], sc_ref[0], sh_ref[0])
        h_sc[...] = h.astype(BF16)
        og_ref[...] = jnp.dot(h, wg_ref[...], precision=HIGHEST, preferred_element_type=F32)

    o_ref[...] = jnp.dot(h_sc[...], w_ref[...], preferred_element_type=F32)


def _inproj(x2, shift, scale, g, w16, wgate, tm):
    N, D = x2.shape
    Fo = w16.shape[1]
    tn = 2048
    return pl.pallas_call(
        _inproj_kernel,
        grid=(N // tm, Fo // tn),
        in_specs=[pl.BlockSpec((tm, D), lambda i, j: (i, 0)),
                  shift.spec(2), scale.spec(2),
                  pl.BlockSpec((1, D), lambda i, j: (0, 0)),
                  pl.BlockSpec((D, tn), lambda i, j: (0, j)),
                  pl.BlockSpec((D, LANES), lambda i, j: (0, 0))],
        out_specs=[pl.BlockSpec((tm, tn), lambda i, j: (i, j)),
                   pl.BlockSpec((tm, LANES), lambda i, j: (i, 0))],
        out_shape=[jax.ShapeDtypeStruct((N, Fo), F32), jax.ShapeDtypeStruct((N, LANES), F32)],
        scratch_shapes=[pltpu.VMEM((tm, D), BF16)],
        compiler_params=_params("parallel", "arbitrary"),
        name="mlstm_inproj",
    )(x2, shift.arr, scale.arr, g, w16, wgate)


def _mlstm_kernel(q_ref, k_ref, v_ref, o_ref, gt_ref, gtt_ref, bg_ref, bgt_ref, cinit_ref,
                  c0_ref, n0_ref, m0_ref, wconv_ref, bconv_ref, ghn_ref,
                  hs_ref, c_ref, n_ref, m_ref, tail_sc, *, L, dh, heads, t_valid):
    inner = heads * dh

    @pl.when(pl.program_id(1) == 0)
    def _():
        c_ref[...] = c0_ref[...]
        n_ref[...] = n0_ref[...]
        m_ref[...] = m0_ref[...]
        tail_sc[...] = cinit_ref[0]

    gt = gt_ref[0] + bg_ref[...]
    gtt = gtt_ref[0] + bgt_ref[...]
    ti = lax.broadcasted_iota(jnp.int32, (L, L), 0)
    si = lax.broadcasted_iota(jnp.int32, (L, L), 1)
    causal = si <= ti
    tcol = lax.broadcasted_iota(jnp.int32, (L, 1), 0)
    trow = lax.broadcasted_iota(jnp.int32, (1, L), 1)

    for h in range(heads):
        sl = slice(h * dh, (h + 1) * dh)
        slk = slice(inner + h * dh, inner + (h + 1) * dh)
        qh = _silu(_causal_conv(q_ref[0, :, sl], tail_sc[:, sl], wconv_ref[:, sl], bconv_ref[:, sl]))
        kh = _silu(_causal_conv(k_ref[0, :, sl], tail_sc[:, slk], wconv_ref[:, slk], bconv_ref[:, slk])) * (dh ** -0.5)
        vb = v_ref[0, :, sl].astype(BF16)

        ig_col = gt[:, h:h + 1]
        lf_col = jax.nn.log_sigmoid(gt[:, heads + h:heads + h + 1])
        ig_row = gtt[h:h + 1, :]
        lf_row = jax.nn.log_sigmoid(gtt[heads + h:heads + h + 1, :])
        if t_valid is not None:
            ig_col = jnp.where(tcol < t_valid, ig_col, -jnp.inf)
            lf_col = jnp.where(tcol < t_valid, lf_col, 0.0)
            ig_row = jnp.where(trow < t_valid, ig_row, -jnp.inf)
            lf_row = jnp.where(trow < t_valid, lf_row, 0.0)

        b_col = jnp.sum(jnp.where(causal, lf_row, 0.0), axis=1, keepdims=True)
        b_row = jnp.sum(jnp.where(ti <= si, lf_col, 0.0), axis=0, keepdims=True)
        dlog = jnp.where(causal, b_col - b_row + ig_row, -jnp.inf)
        g_col = b_col + m_ref[0, h]
        m_col = jnp.maximum(g_col, jnp.max(dlog, axis=1, keepdims=True))
        w_intra = jnp.exp(dlog - m_col)
        w_inter = jnp.exp(g_col - m_col)

        qb = qh.astype(BF16)
        kb = kh.astype(BF16)
        s = w_intra * _nt_dot(qb, kb)
        ch = c_ref[0, h]
        nh = n_ref[0, h]
        num = w_inter * _nt_dot(qb, ch.astype(BF16)) + jnp.dot(s.astype(BF16), vb, preferred_element_type=F32)
        den = w_inter * jnp.sum(qh * nh, axis=1, keepdims=True) + jnp.sum(s, axis=1, keepdims=True)
        hv = num / jnp.maximum(jnp.abs(den), jnp.exp(-m_col))

        m_end = m_col[L - 1:L]
        we_inter = jnp.exp(g_col[L - 1:L] - m_end)
        we_col = jnp.exp(b_col[L - 1:L] - b_col + ig_col - m_end)
        kw = kh * we_col
        c_ref[0, h] = we_inter * ch + _tn_dot(vb, kw.astype(BF16))
        n_ref[0, h] = we_inter * nh + jnp.sum(kw, axis=0, keepdims=True)
        m_ref[0, h] = m_end

        hn = hv * lax.rsqrt(jnp.mean(hv * hv, axis=1, keepdims=True) + EPS) * ghn_ref[:, sl]
        hs_ref[0, :, sl] = (jax.nn.sigmoid(o_ref[0, :, sl]) * hn).astype(hs_ref.dtype)

    tail_sc[:, :inner] = q_ref[0, L - SUBLANES:, :]
    tail_sc[:, inner:] = k_ref[0, L - SUBLANES:, :]


def _mlstm(proj, gates, gates_t, bg, bgt, conv_init, c0, n0, m0, wconv, bconv, ghn, L, t_valid):
    B, Tp, _ = proj.shape
    heads, dh = c0.shape[1], c0.shape[2]
    inner = heads * dh
    nc = Tp // L
    kern = functools.partial(_mlstm_kernel, L=L, dh=dh, heads=heads, t_valid=t_valid)
    col = lambda j: pl.BlockSpec((1, L, inner), lambda b, c: (b, c, j))
    full = lambda shape: pl.BlockSpec(shape, lambda b, c: (0,) * len(shape))
    per_b = lambda shape: pl.BlockSpec((1,) + shape, lambda b, c: (b,) + (0,) * len(shape))
    return pl.pallas_call(
        kern,
        grid=(B, nc),
        in_specs=[col(0), col(1), col(2), col(3),
                  pl.BlockSpec((1, L, LANES), lambda b, c: (b, c, 0)),
                  pl.BlockSpec((1, SUBLANES, L), lambda b, c: (b, 0, c)),
                  full((1, LANES)), full((SUBLANES, 1)),
                  per_b((SUBLANES, 2 * inner)),
                  per_b((heads, dh, dh)), per_b((heads, 1, dh)), per_b((heads, 1, 1)),
                  full((A_CONV, 2 * inner)), full((1, 2 * inner)), full((1, inner))],
        out_specs=[pl.BlockSpec((1, L, inner), lambda b, c: (b, c, 0)),
                   per_b((heads, dh, dh)), per_b((heads, 1, dh)), per_b((heads, 1, 1))],
        out_shape=[jax.ShapeDtypeStruct((B, Tp, inner), BF16),
                   jax.ShapeDtypeStruct((B, heads, dh, dh), F32),
                   jax.ShapeDtypeStruct((B, heads, 1, dh), F32),
                   jax.ShapeDtypeStruct((B, heads, 1, 1), F32)],
        scratch_shapes=[pltpu.VMEM((SUBLANES, 2 * inner), F32)],
        compiler_params=_params("parallel", "arbitrary"),
        name="mlstm",
    )(proj, proj, proj, proj, gates, gates_t, bg, bgt, conv_init, c0, n0, m0, wconv, bconv, ghn)


ROUTE_ROWS = SUBLANES


def _route(h, wr, br):
    tm = h.shape[0]
    per = N_EXPERTS // N_GROUPS
    logits = jnp.dot(h, wr, precision=HIGHEST, preferred_element_type=F32)
    lt = logits.T[:N_EXPERTS]
    s = jax.nn.sigmoid(lt)
    sel = s + br
    neg = jnp.full((1, tm), -jnp.inf, F32)
    izero = jnp.zeros((1, tm), jnp.int32)

    best_score = best_e1 = best_e2 = best_w1 = best_w2 = None
    for grp in range(N_GROUPS):
        rows = [sel[grp * per + j:grp * per + j + 1] for j in range(per)]
        srow = [s[grp * per + j:grp * per + j + 1] for j in range(per)]
        t1, i1, w1 = rows[0], izero, srow[0]
        for j in range(1, per):
            better = rows[j] > t1
            t1 = jnp.where(better, rows[j], t1)
            i1 = jnp.where(better, j, i1)
            w1 = jnp.where(better, srow[j], w1)
        t2, i2, w2 = neg, izero, srow[0]
        for j in range(per):
            better = jnp.where(i1 == j, neg, rows[j]) > t2
            t2 = jnp.where(better, rows[j], t2)
            i2 = jnp.where(better, j, i2)
            w2 = jnp.where(better, srow[j], w2)
        score = t1 + t2
        e1, e2 = i1 + grp * per, i2 + grp * per
        if grp == 0:
            best_score, best_e1, best_e2, best_w1, best_w2 = score, e1, e2, w1, w2
        else:
            better = score > best_score
            best_score = jnp.where(better, score, best_score)
            best_e1 = jnp.where(better, e1, best_e1)
            best_e2 = jnp.where(better, e2, best_e2)
            best_w1 = jnp.where(better, w1, best_w1)
            best_w2 = jnp.where(better, w2, best_w2)
    tot = best_w1 + best_w2
    zero = jnp.zeros((ROUTE_ROWS - 4, tm), F32)
    return jnp.concatenate([best_e1.astype(F32), best_e2.astype(F32), best_w1 / tot, best_w2 / tot, zero], axis=0)


def _proj_router_kernel(a_ref, w_ref, x_ref, gate_ref, sh_ref, sc_ref, g_ref, wr_ref, br_ref,
                        xo_ref, h_ref, route_ref):
    mix = jnp.dot(a_ref[...], w_ref[...], preferred_element_type=F32)
    x = x_ref[...] + gate_ref[0] * mix
    xo_ref[...] = x
    h = _rms_mod(x, g_ref[...], sc_ref[0], sh_ref[0])
    h_ref[...] = h.astype(BF16)
    route_ref[...] = _route(h, wr_ref[...], br_ref[...])


def _proj_router(a16, w16, x2, gate, shift, scale, g, wr, br, tm):
    N, D = x2.shape
    K = a16.shape[1]
    row = lambda w: pl.BlockSpec((tm, w), lambda i: (i, 0))
    full = lambda shape: pl.BlockSpec(shape, lambda i: (0,) * len(shape))
    return pl.pallas_call(
        _proj_router_kernel,
        grid=(N // tm,),
        in_specs=[row(K), full((K, D)), row(D), gate.spec(1), shift.spec(1), scale.spec(1),
                  full((1, D)), full((D, LANES)), full((N_EXPERTS, 1))],
        out_specs=[row(D), row(D), pl.BlockSpec((ROUTE_ROWS, tm), lambda i: (0, i))],
        out_shape=[jax.ShapeDtypeStruct((N, D), F32), jax.ShapeDtypeStruct((N, D), BF16),
                   jax.ShapeDtypeStruct((ROUTE_ROWS, N), F32)],
        compiler_params=_params("parallel"),
        name="proj_router",
    )(a16, w16, x2, gate.arr, shift.arr, scale.arr, g, wr, br)


MOE_BLOCK = 1024
MOE_WINDOW = 128
MOE_CHUNK = 256
SEG_ALIGN = 16


def _moe_kernel(route_ref, h_ref, x_ref, gate_ref, tri_ref, wg_ref, wu_ref, wd_ref, o_ref,
                xs_sc, ys_sc, gs_sc, tok_sc, seg_sc, *, n_experts, n_sorted):
    e = pl.program_id(1)
    tb = h_ref.shape[0]
    S, CH, RW = n_sorted, MOE_CHUNK, MOE_WINDOW
    one_hot = lambda a, b: jnp.where(a, 1.0, jnp.where(b, 1.0, 0.0))

    @pl.when(e == 0)
    def _dispatch():
        route = route_ref[...]
        e1, e2, w1, w2 = route[0:1], route[1:2], route[2:3], route[3:4]
        eid = lax.broadcasted_iota(jnp.int32, (n_experts, tb), 0).astype(F32)
        hit1, hit2 = eid == e1, eid == e2
        routed = one_hot(hit1, hit2)
        earlier = jnp.dot(routed.astype(BF16), tri_ref[...], preferred_element_type=F32)
        count = jnp.sum(routed, axis=1, keepdims=True)
        padded = jnp.floor((count + (SEG_ALIGN - 1)) * (1.0 / SEG_ALIGN)) * SEG_ALIGN
        below = jnp.where(lax.broadcasted_iota(jnp.int32, (n_experts, n_experts), 1)
                          < lax.broadcasted_iota(jnp.int32, (n_experts, n_experts), 0), 1.0, 0.0)
        start = jnp.dot(below, jnp.broadcast_to(padded, (n_experts, LANES)), precision=HIGHEST,
                        preferred_element_type=F32)[:, :1]
        pos = start + earlier
        pos1 = jnp.sum(jnp.where(hit1, pos, 0.0), axis=0, keepdims=True)
        pos2 = jnp.sum(jnp.where(hit2, pos, 0.0), axis=0, keepdims=True)
        for ex in range(n_experts):
            seg_sc[0, ex] = start[ex, 0].astype(jnp.int32)
            seg_sc[1, ex] = (start[ex, 0] + count[ex, 0]).astype(jnp.int32)
        tok_sc[...] = jnp.concatenate([pos1, pos2, jnp.zeros((LANES - 2, tb), F32)], axis=0).T

        def gather(c, carry):
            r0 = pl.multiple_of(c * CH, CH)
            row = (lax.broadcasted_iota(jnp.int32, (CH, tb), 0) + r0).astype(F32)
            is1, is2 = row == pos1, row == pos2
            xs_sc[pl.ds(r0, CH), :] = jnp.dot(one_hot(is1, is2).astype(BF16), h_ref[...],
                                              preferred_element_type=F32).astype(BF16)
            gs_sc[pl.ds(r0, CH), :] = jnp.sum(jnp.where(is1, w1, jnp.where(is2, w2, 0.0)), axis=1, keepdims=True)
            return carry

        lax.fori_loop(0, S // CH, gather, 0)
        xs_sc[S:, :] = jnp.zeros((RW, xs_sc.shape[1]), BF16)
        gs_sc[S:, :] = jnp.zeros((RW, 1), F32)
        ys_sc[...] = jnp.zeros_like(ys_sc)

    seg_start, seg_end = seg_sc[0, e], seg_sc[1, e]

    def window(w, carry):
        r0 = pl.multiple_of(seg_start + w * RW, SEG_ALIGN)
        rows = xs_sc[pl.ds(r0, RW), :]
        a = jnp.dot(rows, wg_ref[0], preferred_element_type=F32)
        u = jnp.dot(rows, wu_ref[0], preferred_element_type=F32)
        mine = lax.broadcasted_iota(jnp.int32, (RW, 1), 0) + r0 < seg_end
        act = jnp.where(mine, _silu(a) * u * gs_sc[pl.ds(r0, RW), :], 0.0)
        ys_sc[pl.ds(r0, RW), :] += jnp.dot(act.astype(BF16), wd_ref[0], preferred_element_type=F32)
        return carry

    lax.fori_loop(0, (seg_end - seg_start + (RW - 1)) // RW, window, 0)

    @pl.when(e == n_experts - 1)
    def _combine():
        def to_bf16(c, carry):
            r0 = pl.multiple_of(c * CH, CH)
            xs_sc[pl.ds(r0, CH), :] = ys_sc[pl.ds(r0, CH), :].astype(BF16)
            return carry

        lax.fori_loop(0, S // CH, to_bf16, 0)

        def scatter(c, carry):
            t0 = pl.multiple_of(c * CH, CH)
            rec = tok_sc[pl.ds(t0, CH), :]
            col = lax.broadcasted_iota(jnp.int32, (CH, S), 1).astype(F32)
            sel = one_hot(col == rec[:, 0:1], col == rec[:, 1:2]).astype(BF16)
            y = jnp.dot(sel, xs_sc[0:S, :], preferred_element_type=F32)
            g = gate_ref[0] if gate_ref.shape[1] == 1 else gate_ref[0, pl.ds(t0, CH), :]
            o_ref[pl.ds(t0, CH), :] = x_ref[pl.ds(t0, CH), :] + g * y
            return carry

        lax.fori_loop(0, tb // CH, scatter, 0)


def _moe(h16, route, x2, gate, wg16, wu16, wd16, layer, tb):
    N, D = x2.shape
    _, E, _, Fe = wg16.shape
    n_sorted = -(-(2 * tb + E * SEG_ALIGN) // MOE_CHUNK) * MOE_CHUNK
    tri = jnp.asarray(np.triu(np.ones((tb, tb), np.float32), 1), dtype=BF16)
    row = lambda w: pl.BlockSpec((tb, w), lambda i, e: (i, 0))
    kern = functools.partial(_moe_kernel, n_experts=E, n_sorted=n_sorted)
    return pl.pallas_call(
        kern,
        grid=(N // tb, E),
        in_specs=[pl.BlockSpec((ROUTE_ROWS, tb), lambda i, e: (0, i)), row(D), row(D), gate.spec(2),
                  pl.BlockSpec((tb, tb), lambda i, e: (0, 0)),
                  pl.BlockSpec((None, 1, D, Fe), lambda i, e: (layer, e, 0, 0)),
                  pl.BlockSpec((None, 1, D, Fe), lambda i, e: (layer, e, 0, 0)),
                  pl.BlockSpec((None, 1, Fe, D), lambda i, e: (layer, e, 0, 0))],
        out_specs=row(D),
        out_shape=jax.ShapeDtypeStruct((N, D), F32),
        scratch_shapes=[pltpu.VMEM((n_sorted + MOE_WINDOW, D), BF16), pltpu.VMEM((n_sorted + MOE_WINDOW, D), F32),
                        pltpu.VMEM((n_sorted + MOE_WINDOW, 1), F32), pltpu.VMEM((tb, LANES), F32),
                        pltpu.SMEM((2, E), jnp.int32)],
        compiler_params=_params("parallel", "arbitrary"),
        name="moe",
    )(route, h16, x2, gate.arr, tri, wg16, wu16, wd16)


def _kvq_kernel(x_ref, shk_ref, sck_ref, gk_ref, shq_ref, scq_ref, gq_ref, wkv_ref, wq_ref,
                gmat_ref, gmatt_ref, gkn_ref, gqn_ref, k_ref, v32_ref, q_ref, *maybe_attn_refs):
    x = x_ref[...]
    y = x * lax.rsqrt(jnp.mean(x * x, axis=-1, keepdims=True) + EPS)
    hk = ((y * gk_ref[...]) * (1.0 + sck_ref[0]) + shk_ref[0]).astype(BF16)
    hq = ((y * gq_ref[...]) * (1.0 + scq_ref[0]) + shq_ref[0]).astype(BF16)
    W = v32_ref.shape[1]
    kv = jnp.dot(hk, wkv_ref[...], preferred_element_type=F32)
    k = _group_rms(kv[:, :W], gmat_ref[...], gmatt_ref[...], gkn_ref[...], B_DK)
    v = kv[:, W:]
    v32_ref[...] = v
    q = jnp.dot(hq, wq_ref[...], preferred_element_type=F32)
    q = _group_rms(q, gmat_ref[...], gmatt_ref[...], gqn_ref[...], B_DK)
    q_ref[...] = (q * (B_DK ** -0.5)).astype(q_ref.dtype)
    if maybe_attn_refs:
        k16_ref, vt_ref = maybe_attn_refs
        k_ref[0] = k.T
        k16_ref[...] = k.astype(BF16)
        vt_ref[0] = v.T.astype(BF16)
    else:
        k_ref[...] = k


def _kvq(x2, shk, sck, gk, shq, scq, gq, wkv16, wq16, gmat, gmat_t, gkn, gqn, tm, seq_len):
    N, D = x2.shape
    W = wq16.shape[1]
    row = lambda w: pl.BlockSpec((tm, w), lambda i: (i, 0))
    full = lambda shape: pl.BlockSpec(shape, lambda i: (0,) * len(shape))
    if seq_len % tm == 0:
        tps = seq_len // tm
        col = pl.BlockSpec((1, W, tm), lambda i: (i // tps, 0, i % tps))
        out_specs = [col, row(W), row(W), row(W), col]
        out_shape = [jax.ShapeDtypeStruct((N // seq_len, W, seq_len), F32), jax.ShapeDtypeStruct((N, W), F32),
                     jax.ShapeDtypeStruct((N, W), BF16), jax.ShapeDtypeStruct((N, W), BF16),
                     jax.ShapeDtypeStruct((N // seq_len, W, seq_len), BF16)]
    else:
        out_specs = [row(W), row(W), row(W)]
        out_shape = [jax.ShapeDtypeStruct((N, W), F32)] * 3
    return pl.pallas_call(
        _kvq_kernel,
        grid=(N // tm,),
        in_specs=[row(D), shk.spec(1), sck.spec(1), full((1, D)), shq.spec(1), scq.spec(1), full((1, D)),
                  full((D, 2 * W)), full((D, W)), full((W, LANES)), full((LANES, W)),
                  full((1, W)), full((1, W))],
        out_specs=out_specs,
        out_shape=out_shape,
        compiler_params=_params("parallel"),
        name="kvq",
    )(x2, shk.arr, sck.arr, gk, shq.arr, scq.arr, gq, wkv16, wq16, gmat, gmat_t, gkn, gqn)


def _lambda(lam_ref, lam_init):
    lv = lam_ref[...]
    a = jnp.sum(lv[0:1] * lv[1:2], axis=1, keepdims=True)
    b = jnp.sum(lv[2:3] * lv[3:4], axis=1, keepdims=True)
    return jnp.exp(a) - jnp.exp(b) + lam_init


ATTN_HEADS_PER_STEP = 4
ONES_ROWS = SUBLANES


def _attn_kernel(qi_ref, kj_ref, ty_ref, fin_ref, q_ref, k_ref, vt_ref, bias_ref, lam_ref, ghn_ref, o_ref,
                 qm_sc, m_sc, acc_sc, *, n_types, lam_init):
    step = pl.program_id(2)
    ty = ty_ref[step]
    tq = q_ref.shape[1]
    heads = ATTN_HEADS_PER_STEP
    hl = lambda hh: slice(hh * LANES, (hh + 1) * LANES)

    @pl.when(kj_ref[step] == 0)
    def _():
        q = q_ref[0]
        lane = lax.broadcasted_iota(jnp.int32, (tq, LANES), 1)
        zero = jnp.zeros((tq, LANES), BF16)
        for hh in range(heads):
            qh = q[:, hl(hh)]
            qm_sc[hh, 0:tq, :] = jnp.where(lane < B_DK, qh, zero)
            qm_sc[hh, tq:2 * tq, :] = jnp.where(lane >= B_DK, qh, zero)
        m_sc[...] = jnp.full_like(m_sc, -jnp.inf)
        acc_sc[...] = jnp.zeros_like(acc_sc)

    def update(adj_of):
        ones = jnp.ones((ONES_ROWS, k_ref.shape[1]), BF16)

        def scores(hh):
            return _nt_dot(k_ref[0, :, hl(hh)], qm_sc[hh])

        def softmax(hh, s):
            if adj_of is not None:
                adj = adj_of(hh)
                s = s + jnp.concatenate([adj, adj], axis=1)
            m_old = m_sc[hh]
            m_new = jnp.maximum(m_old, jnp.max(s, axis=0, keepdims=True))
            m_sc[hh] = m_new
            return jnp.exp(m_old - m_new), jnp.exp(s - m_new).astype(BF16)

        def weighted_values(hh, alpha, p):
            vt1 = jnp.concatenate([vt_ref[0, hl(hh), :], ones], axis=0)
            acc_sc[hh] = alpha * acc_sc[hh] + jnp.dot(vt1, p, preferred_element_type=F32)

        s_of, ap_of = {}, {}
        for t in range(heads + 3):
            if t - 3 >= 0:
                weighted_values(t - 3, *ap_of.pop(t - 3))
            if 0 <= t - 2 < heads:
                ap_of[t - 2] = softmax(t - 2, s_of.pop(t - 2))
            if t < heads:
                s_of[t] = scores(t)

    @pl.when(ty < 0)
    def _():
        update(None)

    for t in range(n_types):
        @pl.when(ty == t)
        def _(t=t):
            update(lambda hh, t=t: bias_ref[hh, t])

    @pl.when(fin_ref[step] == 1)
    def _():
        lam = _lambda(lam_ref, lam_init)
        for hh in range(heads):
            acc = acc_sc[hh]
            both = acc[0:B_DV] / acc[B_DV:B_DV + 1]
            ot = both[:, 0:tq] - lam * both[:, tq:2 * tq]
            on = ot * lax.rsqrt(jnp.mean(ot * ot, axis=0, keepdims=True) + EPS) * ghn_ref[hl(hh), :] * (1.0 - lam_init)
            o_ref[0, :, hl(hh)] = on.T.astype(o_ref.dtype)


def _attn_schedule(T, tq, tk):
    offsets = sorted({qi * tq - kj * tk for qi in range(T // tq) for kj in range(T // tk)
                      if qi * tq + tq - 1 >= kj * tk and qi * tq - kj * tk - (tk - 1) < RPB_MAX_DIST})
    qi_l, kj_l, ty_l, fin_l = [], [], [], []
    for qi in range(T // tq):
        kjs = [kj for kj in range(T // tk) if qi * tq + tq - 1 >= kj * tk]
        for kj in kjs:
            off = qi * tq - kj * tk
            qi_l.append(qi)
            kj_l.append(kj)
            ty_l.append(offsets.index(off) if off in offsets else -1)
            fin_l.append(int(kj == kjs[-1]))
    as_i32 = lambda v: jnp.asarray(np.asarray(v, np.int32))
    return offsets, as_i32(qi_l), as_i32(kj_l), as_i32(ty_l), as_i32(fin_l)


def _bias_table(rpb):
    n = jnp.arange(RPB_MAX_DIST, dtype=jnp.int32)
    max_exact = RPB_BUCKETS // 2
    nf = jnp.maximum(n, 1).astype(F32)
    large = max_exact + (jnp.log(nf / max_exact) / math.log(RPB_MAX_DIST / max_exact)
                         * (RPB_BUCKETS - max_exact)).astype(jnp.int32)
    bucket = jnp.where(n < max_exact, n, jnp.minimum(large, RPB_BUCKETS - 1))
    return (rpb[bucket] - rpb[RPB_BUCKETS - 1][None, :]).T.astype(F32)


def _bias_of_distance(tbl, dist):
    d = np.asarray(dist)
    idx = jnp.asarray(np.clip(d, 0, RPB_MAX_DIST - 1).astype(np.int32))
    vals = jnp.take(tbl, idx, axis=1)
    vals = jnp.where(jnp.asarray(d >= RPB_MAX_DIST), 0.0, vals)
    return jnp.where(jnp.asarray(d < 0), -jnp.inf, vals)


def _toeplitz_kernel(v_ref, o_ref):
    tk, tq = o_ref.shape[1:]
    rows = jnp.broadcast_to(v_ref[0], (tk, v_ref.shape[2]))
    o_ref[0] = pltpu.roll(rows, 0, 1, stride=1, stride_axis=0)[:, :tq]


def _bias_tiles_t(tbl, offsets, tq, tk):
    period = tq + tk
    w = np.arange(period)
    u = np.where(w < tq, w, w - period)
    vext = _bias_of_distance(tbl, np.stack([off + u for off in offsets]))
    H, n_types = vext.shape[:2]
    tiles = pl.pallas_call(
        _toeplitz_kernel,
        grid=(H * n_types,),
        in_specs=[pl.BlockSpec((1, 1, period), lambda i: (i, 0, 0))],
        out_specs=pl.BlockSpec((1, tk, tq), lambda i: (i, 0, 0)),
        out_shape=jax.ShapeDtypeStruct((H * n_types, tk, tq), F32),
        compiler_params=_params("parallel"),
        name="bias_tiles",
    )(vext.reshape(H * n_types, 1, period))
    return tiles.reshape(H, n_types, tk, tq)


def _attn_prompt(q16, k16, vt16, tbl, lam, ghn_col, lam_init, tq, tk):
    B, T, W = q16.shape
    H = W // LANES
    hps = ATTN_HEADS_PER_STEP
    hw = hps * LANES
    assert H % hps == 0
    offsets, qi, kj, ty, fin = _attn_schedule(T, tq, tk)
    bias = _bias_tiles_t(tbl, offsets, tq, tk)
    kern = functools.partial(_attn_kernel, n_types=len(offsets), lam_init=lam_init)
    grid_spec = pltpu.PrefetchScalarGridSpec(
        num_scalar_prefetch=4,
        grid=(H // hps, B, int(qi.shape[0])),
        in_specs=[pl.BlockSpec((1, tq, hw), lambda h, b, s, qi, kj, ty, fin: (b, qi[s], h)),
                  pl.BlockSpec((1, tk, hw), lambda h, b, s, qi, kj, ty, fin: (b, kj[s], h)),
                  pl.BlockSpec((1, hw, tk), lambda h, b, s, qi, kj, ty, fin: (b, h, kj[s])),
                  pl.BlockSpec((hps, len(offsets), tk, tq), lambda h, b, s, *_: (h, 0, 0, 0)),
                  pl.BlockSpec(lam.shape, lambda h, b, s, *_: (0, 0)),
                  pl.BlockSpec((hw, 1), lambda h, b, s, *_: (h, 0))],
        out_specs=pl.BlockSpec((1, tq, hw), lambda h, b, s, qi, kj, ty, fin: (b, qi[s], h)),
        scratch_shapes=[pltpu.VMEM((hps, 2 * tq, LANES), BF16), pltpu.VMEM((hps, 1, 2 * tq), F32),
                        pltpu.VMEM((hps, B_DV + ONES_ROWS, 2 * tq), F32)],
    )
    return pl.pallas_call(
        kern,
        grid_spec=grid_spec,
        out_shape=jax.ShapeDtypeStruct((B, T, W), BF16),
        compiler_params=_params("parallel", "parallel", "arbitrary"),
        name="attn_prompt",
    )(qi, kj, ty, fin, q16, k16, vt16, bias, lam, ghn_col)


PAGES_PER_STEP = 8


def _attn_paged_kernel(pt_ref, q_ref, *refs, heads, t_new, page, n_steps, lam_init):
    pps = PAGES_PER_STEP
    kc_refs, vc_refs = refs[:pps], refs[pps:2 * pps]
    kn_ref, vn_ref, blast_ref, bnew_ref, lam_ref, ghn_ref, o_ref, qm_sc, m_sc, l_sc, acc_sc = refs[2 * pps:]
    j = pl.program_id(1)
    R = SUBLANES
    hsl = lambda h: slice(h * R, (h + 1) * R)
    lsl = lambda h: slice(h * LANES, (h + 1) * LANES)

    @pl.when(j == 0)
    def _():
        q = q_ref[0]
        row = lax.broadcasted_iota(jnp.int32, (R, LANES), 0)
        lane = lax.broadcasted_iota(jnp.int32, (R, LANES), 1)
        keep = (row < t_new) == (lane < B_DK)
        for h in range(heads):
            qm_sc[hsl(h), :] = jnp.where(keep, q[:, lsl(h)], 0.0)
        m_sc[...] = jnp.full_like(m_sc, -jnp.inf)
        l_sc[...] = jnp.zeros_like(l_sc)
        acc_sc[...] = jnp.zeros_like(acc_sc)

    qm = qm_sc[...].astype(BF16)

    def update(s, pv_of):
        m_old = m_sc[...]
        m_new = jnp.maximum(m_old, jnp.max(s, axis=1, keepdims=True))
        alpha = jnp.exp(m_old - m_new)
        p = jnp.exp(s - m_new)
        l_sc[...] = alpha * l_sc[...] + jnp.sum(p, axis=1, keepdims=True)
        pb = p.astype(BF16)
        acc_sc[...] = alpha * acc_sc[...] + jnp.concatenate([pv_of(h, pb[hsl(h)]) for h in range(heads)], axis=0)
        m_sc[...] = m_new

    s = jnp.concatenate(
        [jnp.concatenate([jnp.dot(qm[hsl(h)], kc[0, lsl(h), :].astype(BF16), preferred_element_type=F32)
                          for h in range(heads)], axis=0) for kc in kc_refs], axis=1)
    s = s + jnp.where(j == n_steps - 1, blast_ref[...], 0.0)

    def pv_cached(h, ph):
        parts = [jnp.dot(ph[:, u * page:(u + 1) * page], vc[0, pl.ds(h, page, stride=heads), :].astype(BF16),
                         preferred_element_type=F32) for u, vc in enumerate(vc_refs)]
        return functools.reduce(lambda a, b: a + b, parts)

    update(s, pv_cached)

    @pl.when(j == n_steps - 1)
    def _():
        kn = kn_ref[0].astype(BF16)
        vn = vn_ref[0].astype(BF16)
        s_new = jnp.concatenate([_nt_dot(qm[hsl(h)], kn[:, lsl(h)]) for h in range(heads)], axis=0) + bnew_ref[...]
        update(s_new, lambda h, ph: jnp.dot(ph, vn[:, lsl(h)], preferred_element_type=F32))
        lam = _lambda(lam_ref, lam_init)
        full = acc_sc[...] / l_sc[...]
        for h in range(heads):
            fh = full[hsl(h)]
            o = fh - lam * pltpu.roll(fh, R - t_new, 0)
            on = o * lax.rsqrt(jnp.mean(o * o, axis=1, keepdims=True) + EPS)
            o_ref[0, :, h * LANES:(h + 1) * LANES] = on * ghn_ref[:, h * LANES:(h + 1) * LANES] * (1.0 - lam_init)


def _attn_paged(q, cache_k, cache_v, page_table, k_new, v_new, tbl, lam, ghn, lam_init):
    B, t_new, W = q.shape
    H = W // LANES
    n_pool, page = cache_k.shape[:2]
    n_pages = page_table.shape[1]
    past = n_pages * page
    R = SUBLANES
    pps = PAGES_PER_STEP
    assert 2 * t_new == R and page >= RPB_MAX_DIST and n_pages % pps == 0
    n_steps = n_pages // pps
    pad = lambda a: jnp.concatenate([a, jnp.zeros((B, R - t_new, W), a.dtype)], axis=1)
    q8 = jnp.concatenate([q, q], axis=1)
    t = np.arange(R)[:, None] % t_new
    d_last = past + t - ((n_pages - 1) * page + np.arange(page)[None, :])
    c = np.arange(R)[None, :]
    d_new = np.where(c < t_new, t - c, -1)
    flat = lambda b: b.reshape(H * R, b.shape[-1])
    bias_last = jnp.pad(flat(_bias_of_distance(tbl, d_last)), ((0, 0), ((pps - 1) * page, 0)))
    bias_new = flat(_bias_of_distance(tbl, d_new))
    kern = functools.partial(_attn_paged_kernel, heads=H, t_new=t_new, page=page, n_steps=n_steps, lam_init=lam_init)
    page_spec = lambda rows, width, u: pl.BlockSpec(
        (1, rows, width), lambda b, j, pt: (pt[b * n_pages + j * pps + u], 0, 0))
    per_b = pl.BlockSpec((1, R, W), lambda b, j, pt: (b, 0, 0))
    full = lambda a: pl.BlockSpec(a.shape, lambda b, j, pt: (0,) * a.ndim)
    grid_spec = pltpu.PrefetchScalarGridSpec(
        num_scalar_prefetch=1,
        grid=(B, n_steps),
        in_specs=[per_b] + [page_spec(W, page, u) for u in range(pps)]
                 + [page_spec(page * H, B_DV, u) for u in range(pps)]
                 + [per_b, per_b, full(bias_last), full(bias_new), full(lam), full(ghn)],
        out_specs=per_b,
        scratch_shapes=[pltpu.VMEM((H * R, LANES), F32), pltpu.VMEM((H * R, 1), F32),
                        pltpu.VMEM((H * R, 1), F32), pltpu.VMEM((H * R, B_DV), F32)],
    )
    ck = jnp.transpose(cache_k, (0, 2, 3, 4, 1)).reshape(n_pool, W, page)
    cv = cache_v.reshape(n_pool, page * H, B_DV)
    out = pl.pallas_call(
        kern,
        grid_spec=grid_spec,
        out_shape=jax.ShapeDtypeStruct((B, R, W), F32),
        compiler_params=_params("parallel", "arbitrary"),
        name="attn_paged",
    )(page_table.reshape(-1), q8, *([ck] * pps), *([cv] * pps), pad(k_new), pad(v_new),
      bias_last, bias_new, lam, ghn)
    return out[:, :t_new]


def _trunk(x, mods, mods_kv, state, past, wts):
    B, T, D = x.shape
    N = B * T
    tm = ROW_TILE
    tb = min(MOE_BLOCK, N)
    tr = min(ROUTER_TILE, N)
    mod = lambda m: _Mod(m, T, tm)
    split3 = lambda m: (mod(m[:, :D]), mod(m[:, D:2 * D]), mod(m[:, 2 * D:]))
    rmod = lambda m: _Mod(m, T, tr)
    x2 = x.reshape(N, D)

    heads = A_HEADS
    inner = wts["w_out16"].shape[0]
    dh = inner // heads
    gate = rmod(mods[0][:, 2 * D:])
    if state is None:
        t_rows, x_in = T, x2
    else:
        assert T <= SUBLANES
        t_rows = SUBLANES
        x_in = jnp.concatenate([x, jnp.zeros((B, t_rows - T, D), x.dtype)], axis=1).reshape(B * t_rows, D)
    t_in = min(INPROJ_TILE, B * t_rows)
    proj, gates = _inproj(x_in, _Mod(mods[0][:, :D], t_rows, t_in), _Mod(mods[0][:, D:2 * D], t_rows, t_in),
                          wts["g_norm"][0, 0][None], wts["w_in16"], wts["w_gate"], t_in)
    conv_new = proj.reshape(B, t_rows, 4 * inner)[:, T - (A_CONV - 1):T, :2 * inner]
    proj = proj.reshape(B, t_rows, 4 * inner)
    gates = gates.reshape(B, t_rows, LANES)
    if state is None:
        L, t_valid = math.gcd(T, MLSTM_CHUNK), None
        conv_init = jnp.zeros((B, SUBLANES, 2 * inner), F32)
        c0 = jnp.zeros((B, heads, dh, dh), F32)
        n0 = jnp.zeros((B, heads, 1, dh), F32)
        m0 = jnp.zeros((B, heads, 1, 1), F32)
    else:
        conv_st, c_st, n_st, m_st = state
        L, t_valid = SUBLANES, T
        conv_init = jnp.concatenate([jnp.zeros((B, SUBLANES - (A_CONV - 1), 2 * inner), F32), conv_st], axis=1)
        c0, n0, m0 = c_st, n_st[:, :, None, :], m_st[:, :, None, None]
    gates_t = jnp.swapaxes(gates[:, :, :SUBLANES], 1, 2)
    hs, c1, n1, m1 = _mlstm(proj, gates, gates_t, wts["bg"], wts["bgt"], conv_init, c0, n0, m0,
                            wts["w_conv"], wts["b_conv"], wts["g_hn_a"], L, t_valid)
    hs = hs[:, :T].reshape(N, inner)
    new_state = (conv_new, c1, n1[:, :, 0, :], m1[:, :, 0, 0])

    x2, h16, rg = _proj_router(hs, wts["w_out16"], x2, gate, rmod(mods[1][:, :D]), rmod(mods[1][:, D:2 * D]),
                               wts["g_norm"][0, 1][None], wts["w_router"], wts["b_router"], tr)
    x2 = _moe(h16, rg, x2, _Mod(mods[1][:, 2 * D:], T, tb), wts["wg16"], wts["wu16"], wts["wd16"], 0, tb)

    shift_kv, scale_kv = mod(mods_kv[:, :D]), mod(mods_kv[:, D:])
    shift, scale, gate = split3(mods[2])
    kvq = _kvq(x2, shift_kv, scale_kv, wts["g_kv"], shift, scale, wts["g_norm"][1, 0][None],
               wts["w_kv16"], wts["w_q16"], wts["gmat"], wts["gmat_t"], wts["g_kn"], wts["g_qn"], tm, T)
    v32 = kvq[1]
    W = v32.shape[1]
    H = W // LANES
    lam_init = 0.8 - 0.6 * math.exp(-0.3 * 1)
    if past is None:
        kt32, _, q, k16, vt16 = kvq
        o = _attn_prompt(q.reshape(B, T, W), k16.reshape(B, T, W), vt16, wts["rpb_tbl"], wts["lam"],
                         wts["g_hn_b"].reshape(W, 1), lam_init, math.gcd(T, ATTN_TQ), math.gcd(T, ATTN_TK))
        o = o.reshape(N, W)
        k_out = jnp.transpose(kt32.reshape(B, H, 2, B_DK, T), (0, 4, 1, 2, 3))
    else:
        k32, _, q = kvq
        k_out = k32.reshape(B, T, H, 2, B_DK)
        cache_k, cache_v, page_table = past
        o = _attn_paged(q.reshape(B, T, W), cache_k, cache_v, page_table, k32.reshape(B, T, W),
                        v32.reshape(B, T, W), wts["rpb_tbl"], wts["lam"], wts["g_hn_b"], lam_init)
        o = o.reshape(N, W).astype(BF16)

    x2, h16, rg = _proj_router(o, wts["w_o16"], x2, rmod(mods[2][:, 2 * D:]), rmod(mods[3][:, :D]),
                               rmod(mods[3][:, D:2 * D]), wts["g_norm"][1, 1][None], wts["w_router"],
                               wts["b_router"], tr)
    x2 = _moe(h16, rg, x2, _Mod(mods[3][:, 2 * D:], T, tb), wts["wg16"], wts["wu16"], wts["wd16"], 1, tb)

    return x2.reshape(B, T, D), new_state, k_out, v32.reshape(B, T, H, B_DV)


def kernel(x_prompt, x_sample, c_prompt, c_sample, state_conv, state_C, state_n, state_m, cache_k, cache_v, page_table, w_ada, b_ada, g_norm, w_in_a, b_gate_a, w_conv_a, b_conv_a, g_hn_a, w_out_a, g_kv, w_ada_kv, b_ada_kv, w_kv, g_kn, w_q_b, g_qn_b, lam_b, g_hn_b, w_o_b, rpb, w_router, b_router, w_gate_e, w_up_e, w_down_e):
    Bp, Tp, D = x_prompt.shape
    Bs = x_sample.shape[0]
    inner = w_out_a.shape[1]
    heads_b = g_hn_b.shape[1]
    W = heads_b * B_DV

    n_c = Bp + Bs
    c_all = jnp.concatenate([c_prompt, c_sample, jnp.zeros((-n_c % SUBLANES, D), F32)], axis=0)
    mods = _ada(c_all, w_ada.reshape(-1, D, 3 * D), b_ada.reshape(-1, 1, 3 * D))
    mods_kv = _ada(c_all, w_ada_kv[None], b_ada_kv[None, None])[0]

    n_gate = 2 * A_HEADS
    group_of_lane = np.arange(W) // B_DK
    gmat = jnp.asarray((group_of_lane[:, None] == np.arange(LANES)[None, :]).astype(np.float32)).astype(BF16)
    wts = {
        "g_norm": g_norm,
        "w_in16": w_in_a[0, :, :4 * inner].astype(BF16),
        "w_gate": jnp.pad(w_in_a[0][:, 4 * inner:], ((0, 0), (0, LANES - n_gate))),
        "bg": jnp.pad(b_gate_a[0], (0, LANES - n_gate))[None, :],
        "bgt": b_gate_a[0][:, None],
        "w_conv": w_conv_a[0], "b_conv": b_conv_a[0][None, :],
        "g_hn_a": g_hn_a[0].reshape(1, inner),
        "w_out16": w_out_a[0].astype(BF16),
        "g_kv": g_kv[None, :],
        "w_kv16": w_kv.astype(BF16), "w_q16": w_q_b[0].astype(BF16),
        "gmat": gmat, "gmat_t": gmat.T,
        "g_kn": jnp.tile(g_kn.reshape(-1), heads_b)[None, :],
        "g_qn": jnp.tile(g_qn_b[0].reshape(-1), heads_b)[None, :],
        "lam": lam_b[0], "g_hn_b": g_hn_b[0].reshape(1, W),
        "w_o16": w_o_b[0].astype(BF16),
        "rpb_tbl": _bias_table(rpb),
        "w_router": jnp.pad(w_router, ((0, 0), (0, LANES - N_EXPERTS))),
        "b_router": b_router[:, None],
        "wg16": w_gate_e.astype(BF16), "wu16": w_up_e.astype(BF16), "wd16": w_down_e.astype(BF16),
    }

    y_p, st_p, k_p, v_p = _trunk(x_prompt, mods[:, :Bp], mods_kv[:Bp], None, None, wts)
    y_s, st_s, k_s, v_s = _trunk(x_sample, mods[:, Bp:n_c], mods_kv[Bp:n_c],
                                 (state_conv[0], state_C[0], state_n[0], state_m[0]),
                                 (cache_k, cache_v, page_table), wts)
    stack = lambda st: tuple(a[None] for a in st)
    return (y_p, y_s) + stack(st_p) + (k_p, v_p) + stack(st_s) + (k_s, v_s)
```

```python
import functools
import math

import numpy as np
import jax
import jax.numpy as jnp
from jax import lax
from jax.experimental import pallas as pl
from jax.experimental.pallas import tpu as pltpu

F32, BF16 = jnp.float32, jnp.bfloat16
HIGHEST = lax.Precision.HIGHEST
EPS = 1e-6

A_HEADS = 4
A_CONV = 4
B_DK = 64
B_DV = 128
N_EXPERTS = 16
N_GROUPS = 4
RPB_BUCKETS = 32
RPB_MAX_DIST = 128

LANES = 128
SUBLANES = 8
VMEM_LIMIT_BYTES = 56 * 1024 * 1024

ROW_TILE = 512
INPROJ_TILE = 1024
ROUTER_TILE = 1024
MLSTM_CHUNK = 256
ATTN_TQ = 512
ATTN_TK = 512


def _params(*sem):
    return pltpu.CompilerParams(dimension_semantics=sem, vmem_limit_bytes=VMEM_LIMIT_BYTES)


def _nt_dot(a, b):
    return lax.dot_general(a, b, (((1,), (1,)), ((), ())), preferred_element_type=F32)


def _tn_dot(a, b):
    return lax.dot_general(a, b, (((0,), (0,)), ((), ())), preferred_element_type=F32)


def _silu(x):
    return x * jax.nn.sigmoid(x)


def _rms_mod(x, g, scale, shift):
    y = x * lax.rsqrt(jnp.mean(x * x, axis=-1, keepdims=True) + EPS)
    return (y * g) * (1.0 + scale) + shift


def _dot_split(a, b16):
    hi = a.astype(BF16)
    lo = (a - hi.astype(F32)).astype(BF16)
    return (jnp.dot(hi, b16, preferred_element_type=F32) + jnp.dot(lo, b16, preferred_element_type=F32))


def _group_rms(x, gmat, gmat_t, g, group):
    ss = jnp.dot((x * x).astype(BF16), gmat, preferred_element_type=F32)
    r = lax.rsqrt(ss * (1.0 / group) + EPS)
    rf = _dot_split(r, gmat_t)
    return x * rf * g


def _ada_kernel(c_ref, w_ref, b_ref, o_ref):
    a = _silu(c_ref[...])
    o_ref[0] = jnp.dot(a, w_ref[0], precision=HIGHEST, preferred_element_type=F32) + b_ref[0]


def _ada(c_all, w, b):
    S, D, Fo = w.shape
    R = c_all.shape[0]
    tn = 1024
    return pl.pallas_call(
        _ada_kernel,
        grid=(S, Fo // tn),
        in_specs=[pl.BlockSpec((R, D), lambda s, j: (0, 0)),
                  pl.BlockSpec((1, D, tn), lambda s, j: (s, 0, j)),
                  pl.BlockSpec((1, 1, tn), lambda s, j: (s, 0, j))],
        out_specs=pl.BlockSpec((1, R, tn), lambda s, j: (s, 0, j)),
        out_shape=jax.ShapeDtypeStruct((S, R, Fo), F32),
        compiler_params=_params("parallel", "parallel"),
        name="ada",
    )(c_all, w, b)


class _Mod:
    def __init__(self, m, T, tm):
        B, D = m.shape
        if T % tm == 0:
            self.arr, self.tiles_per_group = m[:, None, :], T // tm
        else:
            assert (B * T) % tm == 0
            self.arr, self.tiles_per_group = jnp.repeat(m, T, axis=0).reshape(-1, tm, D), 1

    def spec(self, grid_rank):
        R, D = self.arr.shape[1:]
        tpg = self.tiles_per_group
        if grid_rank == 1:
            return pl.BlockSpec((1, R, D), lambda i: (i // tpg, 0, 0))
        return pl.BlockSpec((1, R, D), lambda i, j: (i // tpg, 0, 0))


def _causal_conv(x, tail, w, b):
    L, width = x.shape
    row8 = lax.broadcasted_iota(jnp.int32, (SUBLANES, width), 0)
    acc = b + x * w[A_CONV - 1:A_CONV]
    for s in range(1, A_CONV):
        xs = pltpu.roll(x, s, 0)
        top = jnp.where(row8 < s, pltpu.roll(tail, s, 0), xs[:SUBLANES])
        xs = top if L == SUBLANES else jnp.concatenate([top, xs[SUBLANES:]], axis=0)
        acc = acc + xs * w[A_CONV - 1 - s:A_CONV - s]
    return acc


def _inproj_kernel(x_ref, sh_ref, sc_ref, g_ref, w_ref, wg_ref, o_ref, og_ref, h_sc):
    @pl.when(pl.program_id(1) == 0)
    def _():
        h = _rms_mod(x_ref[...], g_ref[...], sc_ref[0], sh_ref[0])
        h_sc[...] = h.astype(BF16)
        og_ref[...] = lax.dot_general(h, wg_ref[...], (((1,), (1,)), ((), ())), precision=HIGHEST,
                                      preferred_element_type=F32)

    o_ref[...] = _nt_dot(h_sc[...], w_ref[...])


def _inproj(x2, shift, scale, g, w16_t, wgate_t, tm):
    N, D = x2.shape
    Fo = w16_t.shape[0]
    tn = 2048
    return pl.pallas_call(
        _inproj_kernel,
        grid=(N // tm, Fo // tn),
        in_specs=[pl.BlockSpec((tm, D), lambda i, j: (i, 0)),
                  shift.spec(2), scale.spec(2),
                  pl.BlockSpec((1, D), lambda i, j: (0, 0)),
                  pl.BlockSpec((tn, D), lambda i, j: (j, 0)),
                  pl.BlockSpec((LANES, D), lambda i, j: (0, 0))],
        out_specs=[pl.BlockSpec((tm, tn), lambda i, j: (i, j)),
                   pl.BlockSpec((tm, LANES), lambda i, j: (i, 0))],
        out_shape=[jax.ShapeDtypeStruct((N, Fo), F32), jax.ShapeDtypeStruct((N, LANES), F32)],
        scratch_shapes=[pltpu.VMEM((tm, D), BF16)],
        compiler_params=_params("parallel", "arbitrary"),
        name="mlstm_inproj",
    )(x2, shift.arr, scale.arr, g, w16_t, wgate_t)


def _mlstm_kernel(q_ref, k_ref, v_ref, o_ref, gt_ref, gtt_ref, bg_ref, bgt_ref, cinit_ref,
                  c0_ref, n0_ref, m0_ref, wconv_ref, bconv_ref, ghn_ref,
                  hs_ref, c_ref, n_ref, m_ref, tail_sc, *, L, dh, heads, t_valid):
    inner = heads * dh

    @pl.when(pl.program_id(1) == 0)
    def _():
        c_ref[...] = c0_ref[...]
        n_ref[...] = n0_ref[...]
        m_ref[...] = m0_ref[...]
        tail_sc[...] = cinit_ref[0]

    gt = gt_ref[0] + bg_ref[...]
    gtt = gtt_ref[0] + bgt_ref[...]
    ti = lax.broadcasted_iota(jnp.int32, (L, L), 0)
    si = lax.broadcasted_iota(jnp.int32, (L, L), 1)
    causal = si <= ti
    tcol = lax.broadcasted_iota(jnp.int32, (L, 1), 0)
    trow = lax.broadcasted_iota(jnp.int32, (1, L), 1)

    for h in range(heads):
        sl = slice(h * dh, (h + 1) * dh)
        slk = slice(inner + h * dh, inner + (h + 1) * dh)
        qh = _silu(_causal_conv(q_ref[0, :, sl], tail_sc[:, sl], wconv_ref[:, sl], bconv_ref[:, sl]))
        kh = _silu(_causal_conv(k_ref[0, :, sl], tail_sc[:, slk], wconv_ref[:, slk], bconv_ref[:, slk])) * (dh ** -0.5)
        vb = v_ref[0, :, sl].astype(BF16)

        ig_col = gt[:, h:h + 1]
        lf_col = jax.nn.log_sigmoid(gt[:, heads + h:heads + h + 1])
        ig_row = gtt[h:h + 1, :]
        lf_row = jax.nn.log_sigmoid(gtt[heads + h:heads + h + 1, :])
        if t_valid is not None:
            ig_col = jnp.where(tcol < t_valid, ig_col, -jnp.inf)
            lf_col = jnp.where(tcol < t_valid, lf_col, 0.0)
            ig_row = jnp.where(trow < t_valid, ig_row, -jnp.inf)
            lf_row = jnp.where(trow < t_valid, lf_row, 0.0)

        b_col = jnp.sum(jnp.where(causal, lf_row, 0.0), axis=1, keepdims=True)
        b_row = jnp.sum(jnp.where(ti <= si, lf_col, 0.0), axis=0, keepdims=True)
        dlog = jnp.where(causal, b_col - b_row + ig_row, -jnp.inf)
        g_col = b_col + m_ref[0, h]
        m_col = jnp.maximum(g_col, jnp.max(dlog, axis=1, keepdims=True))
        w_intra = jnp.exp(dlog - m_col)
        w_inter = jnp.exp(g_col - m_col)

        qb = qh.astype(BF16)
        kb = kh.astype(BF16)
        s = w_intra * _nt_dot(qb, kb)
        ch = c_ref[0, h]
        nh = n_ref[0, h]
        num = w_inter * _nt_dot(qb, ch.astype(BF16)) + jnp.dot(s.astype(BF16), vb, preferred_element_type=F32)
        den = w_inter * jnp.sum(qh * nh, axis=1, keepdims=True) + jnp.sum(s, axis=1, keepdims=True)
        hv = num / jnp.maximum(jnp.abs(den), jnp.exp(-m_col))

        m_end = m_col[L - 1:L]
        we_inter = jnp.exp(g_col[L - 1:L] - m_end)
        we_col = jnp.exp(b_col[L - 1:L] - b_col + ig_col - m_end)
        kw = kh * we_col
        c_ref[0, h] = we_inter * ch + _tn_dot(vb, kw.astype(BF16))
        n_ref[0, h] = we_inter * nh + jnp.sum(kw, axis=0, keepdims=True)
        m_ref[0, h] = m_end

        hn = hv * lax.rsqrt(jnp.mean(hv * hv, axis=1, keepdims=True) + EPS) * ghn_ref[:, sl]
        hs_ref[0, :, sl] = (jax.nn.sigmoid(o_ref[0, :, sl]) * hn).astype(hs_ref.dtype)

    tail_sc[:, :inner] = q_ref[0, L - SUBLANES:, :]
    tail_sc[:, inner:] = k_ref[0, L - SUBLANES:, :]


def _mlstm(proj, gates, gates_t, bg, bgt, conv_init, c0, n0, m0, wconv, bconv, ghn, L, t_valid):
    B, Tp, _ = proj.shape
    heads, dh = c0.shape[1], c0.shape[2]
    inner = heads * dh
    nc = Tp // L
    kern = functools.partial(_mlstm_kernel, L=L, dh=dh, heads=heads, t_valid=t_valid)
    col = lambda j: pl.BlockSpec((1, L, inner), lambda b, c: (b, c, j))
    full = lambda shape: pl.BlockSpec(shape, lambda b, c: (0,) * len(shape))
    per_b = lambda shape: pl.BlockSpec((1,) + shape, lambda b, c: (b,) + (0,) * len(shape))
    return pl.pallas_call(
        kern,
        grid=(B, nc),
        in_specs=[col(0), col(1), col(2), col(3),
                  pl.BlockSpec((1, L, LANES), lambda b, c: (b, c, 0)),
                  pl.BlockSpec((1, SUBLANES, L), lambda b, c: (b, 0, c)),
                  full((1, LANES)), full((SUBLANES, 1)),
                  per_b((SUBLANES, 2 * inner)),
                  per_b((heads, dh, dh)), per_b((heads, 1, dh)), per_b((heads, 1, 1)),
                  full((A_CONV, 2 * inner)), full((1, 2 * inner)), full((1, inner))],
        out_specs=[pl.BlockSpec((1, L, inner), lambda b, c: (b, c, 0)),
                   per_b((heads, dh, dh)), per_b((heads, 1, dh)), per_b((heads, 1, 1))],
        out_shape=[jax.ShapeDtypeStruct((B, Tp, inner), BF16),
                   jax.ShapeDtypeStruct((B, heads, dh, dh), F32),
                   jax.ShapeDtypeStruct((B, heads, 1, dh), F32),
                   jax.ShapeDtypeStruct((B, heads, 1, 1), F32)],
        scratch_shapes=[pltpu.VMEM((SUBLANES, 2 * inner), F32)],
        compiler_params=_params("parallel", "arbitrary"),
        name="mlstm",
    )(proj, proj, proj, proj, gates, gates_t, bg, bgt, conv_init, c0, n0, m0, wconv, bconv, ghn)


ROUTE_ROWS = SUBLANES


def _route(h, wr, br):
    tm = h.shape[0]
    per = N_EXPERTS // N_GROUPS
    logits = jnp.dot(h, wr, precision=HIGHEST, preferred_element_type=F32)
    lt = logits.T[:N_EXPERTS]
    s = jax.nn.sigmoid(lt)
    sel = s + br
    neg = jnp.full((1, tm), -jnp.inf, F32)
    izero = jnp.zeros((1, tm), jnp.int32)

    best_score = best_e1 = best_e2 = best_w1 = best_w2 = None
    for grp in range(N_GROUPS):
        rows = [sel[grp * per + j:grp * per + j + 1] for j in range(per)]
        srow = [s[grp * per + j:grp * per + j + 1] for j in range(per)]
        t1, i1, w1 = rows[0], izero, srow[0]
        for j in range(1, per):
            better = rows[j] > t1
            t1 = jnp.where(better, rows[j], t1)
            i1 = jnp.where(better, j, i1)
            w1 = jnp.where(better, srow[j], w1)
        t2, i2, w2 = neg, izero, srow[0]
        for j in range(per):
            better = jnp.where(i1 == j, neg, rows[j]) > t2
            t2 = jnp.where(better, rows[j], t2)
            i2 = jnp.where(better, j, i2)
            w2 = jnp.where(better, srow[j], w2)
        score = t1 + t2
        e1, e2 = i1 + grp * per, i2 + grp * per
        if grp == 0:
            best_score, best_e1, best_e2, best_w1, best_w2 = score, e1, e2, w1, w2
        else:
            better = score > best_score
            best_score = jnp.where(better, score, best_score)
            best_e1 = jnp.where(better, e1, best_e1)
            best_e2 = jnp.where(better, e2, best_e2)
            best_w1 = jnp.where(better, w1, best_w1)
            best_w2 = jnp.where(better, w2, best_w2)
    tot = best_w1 + best_w2
    zero = jnp.zeros((ROUTE_ROWS - 4, tm), F32)
    return jnp.concatenate([best_e1.astype(F32), best_e2.astype(F32), best_w1 / tot, best_w2 / tot, zero], axis=0)


def _proj_router_kernel(a_ref, w_ref, x_ref, gate_ref, sh_ref, sc_ref, g_ref, wr_ref, br_ref,
                        xo_ref, h_ref, route_ref):
    mix = jnp.dot(a_ref[...], w_ref[...], preferred_element_type=F32)
    x = x_ref[...] + gate_ref[0] * mix
    xo_ref[...] = x
    h = _rms_mod(x, g_ref[...], sc_ref[0], sh_ref[0])
    h_ref[...] = h.astype(BF16)
    route_ref[...] = _route(h, wr_ref[...], br_ref[...])


def _proj_router(a16, w16, x2, gate, shift, scale, g, wr, br, tm):
    N, D = x2.shape
    K = a16.shape[1]
    row = lambda w: pl.BlockSpec((tm, w), lambda i: (i, 0))
    full = lambda shape: pl.BlockSpec(shape, lambda i: (0,) * len(shape))
    return pl.pallas_call(
        _proj_router_kernel,
        grid=(N // tm,),
        in_specs=[row(K), full((K, D)), row(D), gate.spec(1), shift.spec(1), scale.spec(1),
                  full((1, D)), full((D, LANES)), full((N_EXPERTS, 1))],
        out_specs=[row(D), row(D), pl.BlockSpec((ROUTE_ROWS, tm), lambda i: (0, i))],
        out_shape=[jax.ShapeDtypeStruct((N, D), F32), jax.ShapeDtypeStruct((N, D), BF16),
                   jax.ShapeDtypeStruct((ROUTE_ROWS, N), F32)],
        compiler_params=_params("parallel"),
        name="proj_router",
    )(a16, w16, x2, gate.arr, shift.arr, scale.arr, g, wr, br)


MOE_BLOCK = 1024
MOE_WINDOW = 256
MOE_CHUNK = 256
SEG_ALIGN = 16


def _moe_kernel(route_ref, h_ref, x_ref, gate_ref, tri_ref, wg_ref, wu_ref, wd_ref, o_ref,
                xs_sc, ys_sc, gs_sc, tok_sc, seg_sc, *, n_experts, n_sorted):
    e = pl.program_id(1)
    tb = h_ref.shape[0]
    S, CH, RW = n_sorted, MOE_CHUNK, MOE_WINDOW
    one_hot = lambda a, b: jnp.where(a, 1.0, jnp.where(b, 1.0, 0.0))

    @pl.when(e == 0)
    def _dispatch():
        route = route_ref[...]
        e1, e2, w1, w2 = route[0:1], route[1:2], route[2:3], route[3:4]
        eid = lax.broadcasted_iota(jnp.int32, (n_experts, tb), 0).astype(F32)
        hit1, hit2 = eid == e1, eid == e2
        routed = one_hot(hit1, hit2)
        earlier = jnp.dot(routed.astype(BF16), tri_ref[...], preferred_element_type=F32)
        count = jnp.sum(routed, axis=1, keepdims=True)
        padded = jnp.floor((count + (SEG_ALIGN - 1)) * (1.0 / SEG_ALIGN)) * SEG_ALIGN
        below = jnp.where(lax.broadcasted_iota(jnp.int32, (n_experts, n_experts), 1)
                          < lax.broadcasted_iota(jnp.int32, (n_experts, n_experts), 0), 1.0, 0.0)
        start = jnp.dot(below, jnp.broadcast_to(padded, (n_experts, LANES)), precision=HIGHEST,
                        preferred_element_type=F32)[:, :1]
        pos = start + earlier
        pos1 = jnp.sum(jnp.where(hit1, pos, 0.0), axis=0, keepdims=True)
        pos2 = jnp.sum(jnp.where(hit2, pos, 0.0), axis=0, keepdims=True)
        for ex in range(n_experts):
            seg_sc[0, ex] = start[ex, 0].astype(jnp.int32)
            seg_sc[1, ex] = (start[ex, 0] + count[ex, 0]).astype(jnp.int32)
        tok_sc[...] = jnp.concatenate([pos1, pos2, jnp.zeros((LANES - 2, tb), F32)], axis=0).T

        def gather(c, carry):
            r0 = pl.multiple_of(c * CH, CH)
            row = (lax.broadcasted_iota(jnp.int32, (CH, tb), 0) + r0).astype(F32)
            is1, is2 = row == pos1, row == pos2
            xs_sc[pl.ds(r0, CH), :] = jnp.dot(one_hot(is1, is2).astype(BF16), h_ref[...],
                                              preferred_element_type=F32).astype(BF16)
            gs_sc[pl.ds(r0, CH), :] = jnp.sum(jnp.where(is1, w1, jnp.where(is2, w2, 0.0)), axis=1, keepdims=True)
            return carry

        lax.fori_loop(0, S // CH, gather, 0)
        xs_sc[S:, :] = jnp.zeros((RW, xs_sc.shape[1]), BF16)
        gs_sc[S:, :] = jnp.zeros((RW, 1), F32)
        ys_sc[...] = jnp.zeros_like(ys_sc)

    seg_start, seg_end = seg_sc[0, e], seg_sc[1, e]

    def window(w, carry):
        r0 = pl.multiple_of(seg_start + w * RW, SEG_ALIGN)
        rows = xs_sc[pl.ds(r0, RW), :]
        a = jnp.dot(rows, wg_ref[0], preferred_element_type=F32)
        u = jnp.dot(rows, wu_ref[0], preferred_element_type=F32)
        mine = lax.broadcasted_iota(jnp.int32, (RW, 1), 0) + r0 < seg_end
        act = jnp.where(mine, _silu(a) * u * gs_sc[pl.ds(r0, RW), :], 0.0)
        ys_sc[pl.ds(r0, RW), :] += jnp.dot(act.astype(BF16), wd_ref[0], preferred_element_type=F32)
        return carry

    lax.fori_loop(0, (seg_end - seg_start + (RW - 1)) // RW, window, 0)

    @pl.when(e == n_experts - 1)
    def _combine():
        def to_bf16(c, carry):
            r0 = pl.multiple_of(c * CH, CH)
            xs_sc[pl.ds(r0, CH), :] = ys_sc[pl.ds(r0, CH), :].astype(BF16)
            return carry

        lax.fori_loop(0, S // CH, to_bf16, 0)

        def scatter(c, carry):
            t0 = pl.multiple_of(c * CH, CH)
            rec = tok_sc[pl.ds(t0, CH), :]
            col = lax.broadcasted_iota(jnp.int32, (CH, S), 1).astype(F32)
            sel = one_hot(col == rec[:, 0:1], col == rec[:, 1:2]).astype(BF16)
            y = jnp.dot(sel, xs_sc[0:S, :], preferred_element_type=F32)
            g = gate_ref[0] if gate_ref.shape[1] == 1 else gate_ref[0, pl.ds(t0, CH), :]
            o_ref[pl.ds(t0, CH), :] = x_ref[pl.ds(t0, CH), :] + g * y
            return carry

        lax.fori_loop(0, tb // CH, scatter, 0)


def _moe(h16, route, x2, gate, wg16, wu16, wd16, layer, tb):
    N, D = x2.shape
    _, E, _, Fe = wg16.shape
    n_sorted = -(-(2 * tb + E * SEG_ALIGN) // MOE_CHUNK) * MOE_CHUNK
    tri = jnp.asarray(np.triu(np.ones((tb, tb), np.float32), 1), dtype=BF16)
    row = lambda w: pl.BlockSpec((tb, w), lambda i, e: (i, 0))
    kern = functools.partial(_moe_kernel, n_experts=E, n_sorted=n_sorted)
    return pl.pallas_call(
        kern,
        grid=(N // tb, E),
        in_specs=[pl.BlockSpec((ROUTE_ROWS, tb), lambda i, e: (0, i)), row(D), row(D), gate.spec(2),
                  pl.BlockSpec((tb, tb), lambda i, e: (0, 0)),
                  pl.BlockSpec((None, 1, D, Fe), lambda i, e: (layer, e, 0, 0)),
                  pl.BlockSpec((None, 1, D, Fe), lambda i, e: (layer, e, 0, 0)),
                  pl.BlockSpec((None, 1, Fe, D), lambda i, e: (layer, e, 0, 0))],
        out_specs=row(D),
        out_shape=jax.ShapeDtypeStruct((N, D), F32),
        scratch_shapes=[pltpu.VMEM((n_sorted + MOE_WINDOW, D), BF16), pltpu.VMEM((n_sorted + MOE_WINDOW, D), F32),
                        pltpu.VMEM((n_sorted + MOE_WINDOW, 1), F32), pltpu.VMEM((tb, LANES), F32),
                        pltpu.SMEM((2, E), jnp.int32)],
        compiler_params=_params("parallel", "arbitrary"),
        name="moe",
    )(route, h16, x2, gate.arr, tri, wg16, wu16, wd16)


def _kvq_kernel(x_ref, shk_ref, sck_ref, gk_ref, shq_ref, scq_ref, gq_ref, wkv_ref, wq_ref,
                gmat_ref, gmatt_ref, gkn_ref, gqn_ref, k_ref, v32_ref, q_ref, *maybe_attn_refs):
    x = x_ref[...]
    y = x * lax.rsqrt(jnp.mean(x * x, axis=-1, keepdims=True) + EPS)
    hk = ((y * gk_ref[...]) * (1.0 + sck_ref[0]) + shk_ref[0]).astype(BF16)
    hq = ((y * gq_ref[...]) * (1.0 + scq_ref[0]) + shq_ref[0]).astype(BF16)
    W = v32_ref.shape[1]
    kv = jnp.dot(hk, wkv_ref[...], preferred_element_type=F32)
    k = _group_rms(kv[:, :W], gmat_ref[...], gmatt_ref[...], gkn_ref[...], B_DK)
    v = kv[:, W:]
    v32_ref[...] = v
    q = jnp.dot(hq, wq_ref[...], preferred_element_type=F32)
    q = _group_rms(q, gmat_ref[...], gmatt_ref[...], gqn_ref[...], B_DK)
    q_ref[...] = (q * (B_DK ** -0.5)).astype(q_ref.dtype)
    if maybe_attn_refs:
        k16_ref, vt_ref = maybe_attn_refs
        k_ref[0] = k.T
        k16_ref[...] = k.astype(BF16)
        vt_ref[0] = v.T.astype(BF16)
    else:
        k_ref[...] = k


def _kvq(x2, shk, sck, gk, shq, scq, gq, wkv16, wq16, gmat, gmat_t, gkn, gqn, tm, seq_len):
    N, D = x2.shape
    W = wq16.shape[1]
    row = lambda w: pl.BlockSpec((tm, w), lambda i: (i, 0))
    full = lambda shape: pl.BlockSpec(shape, lambda i: (0,) * len(shape))
    if seq_len % tm == 0:
        tps = seq_len // tm
        col = pl.BlockSpec((1, W, tm), lambda i: (i // tps, 0, i % tps))
        out_specs = [col, row(W), row(W), row(W), col]
        out_shape = [jax.ShapeDtypeStruct((N // seq_len, W, seq_len), F32), jax.ShapeDtypeStruct((N, W), F32),
                     jax.ShapeDtypeStruct((N, W), BF16), jax.ShapeDtypeStruct((N, W), BF16),
                     jax.ShapeDtypeStruct((N // seq_len, W, seq_len), BF16)]
    else:
        out_specs = [row(W), row(W), row(W)]
        out_shape = [jax.ShapeDtypeStruct((N, W), F32)] * 3
    return pl.pallas_call(
        _kvq_kernel,
        grid=(N // tm,),
        in_specs=[row(D), shk.spec(1), sck.spec(1), full((1, D)), shq.spec(1), scq.spec(1), full((1, D)),
                  full((D, 2 * W)), full((D, W)), full((W, LANES)), full((LANES, W)),
                  full((1, W)), full((1, W))],
        out_specs=out_specs,
        out_shape=out_shape,
        compiler_params=_params("parallel"),
        name="kvq",
    )(x2, shk.arr, sck.arr, gk, shq.arr, scq.arr, gq, wkv16, wq16, gmat, gmat_t, gkn, gqn)


def _lambda(lam_ref, lam_init):
    lv = lam_ref[...]
    a = jnp.sum(lv[0:1] * lv[1:2], axis=1, keepdims=True)
    b = jnp.sum(lv[2:3] * lv[3:4], axis=1, keepdims=True)
    return jnp.exp(a) - jnp.exp(b) + lam_init


ATTN_HEADS_PER_STEP = 4
ONES_ROWS = SUBLANES


def _attn_kernel(qi_ref, kj_ref, ty_ref, fin_ref, q_ref, k_ref, vt_ref, bias_ref, lam_ref, ghn_ref, o_ref,
                 qm_sc, m_sc, acc_sc, *, n_types, lam_init):
    step = pl.program_id(2)
    ty = ty_ref[step]
    tq = q_ref.shape[1]
    heads = ATTN_HEADS_PER_STEP
    hl = lambda hh: slice(hh * LANES, (hh + 1) * LANES)

    @pl.when(kj_ref[step] == 0)
    def _():
        q = q_ref[0]
        lane = lax.broadcasted_iota(jnp.int32, (tq, LANES), 1)
        zero = jnp.zeros((tq, LANES), BF16)
        for hh in range(heads):
            qh = q[:, hl(hh)]
            qm_sc[hh, 0:tq, :] = jnp.where(lane < B_DK, qh, zero)
            qm_sc[hh, tq:2 * tq, :] = jnp.where(lane >= B_DK, qh, zero)
        m_sc[...] = jnp.full_like(m_sc, -jnp.inf)
        acc_sc[...] = jnp.zeros_like(acc_sc)

    def update(adj_of):
        ones = jnp.ones((ONES_ROWS, k_ref.shape[1]), BF16)

        def scores(hh):
            return _nt_dot(k_ref[0, :, hl(hh)], qm_sc[hh])

        def softmax(hh, s):
            if adj_of is not None:
                adj = adj_of(hh)
                s = s + jnp.concatenate([adj, adj], axis=1)
            m_old = m_sc[hh]
            m_new = jnp.maximum(m_old, jnp.max(s, axis=0, keepdims=True))
            m_sc[hh] = m_new
            return jnp.exp(m_old - m_new), jnp.exp(s - m_new).astype(BF16)

        def weighted_values(hh, alpha, p):
            vt1 = jnp.concatenate([vt_ref[0, hl(hh), :], ones], axis=0)
            acc_sc[hh] = alpha * acc_sc[hh] + jnp.dot(vt1, p, preferred_element_type=F32)

        s_of, ap_of = {}, {}
        for t in range(heads + 3):
            if t - 3 >= 0:
                weighted_values(t - 3, *ap_of.pop(t - 3))
            if 0 <= t - 2 < heads:
                ap_of[t - 2] = softmax(t - 2, s_of.pop(t - 2))
            if t < heads:
                s_of[t] = scores(t)

    @pl.when(ty < 0)
    def _():
        update(None)

    for t in range(n_types):
        @pl.when(ty == t)
        def _(t=t):
            update(lambda hh, t=t: bias_ref[hh, t])

    @pl.when(fin_ref[step] == 1)
    def _():
        lam = _lambda(lam_ref, lam_init)
        for hh in range(heads):
            acc = acc_sc[hh]
            both = acc[0:B_DV] / acc[B_DV:B_DV + 1]
            ot = both[:, 0:tq] - lam * both[:, tq:2 * tq]
            on = ot * lax.rsqrt(jnp.mean(ot * ot, axis=0, keepdims=True) + EPS) * ghn_ref[hl(hh), :] * (1.0 - lam_init)
            o_ref[0, :, hl(hh)] = on.T.astype(o_ref.dtype)


def _attn_schedule(T, tq, tk):
    offsets = sorted({qi * tq - kj * tk for qi in range(T // tq) for kj in range(T // tk)
                      if qi * tq + tq - 1 >= kj * tk and qi * tq - kj * tk - (tk - 1) < RPB_MAX_DIST})
    qi_l, kj_l, ty_l, fin_l = [], [], [], []
    for qi in range(T // tq):
        kjs = [kj for kj in range(T // tk) if qi * tq + tq - 1 >= kj * tk]
        for kj in kjs:
            off = qi * tq - kj * tk
            qi_l.append(qi)
            kj_l.append(kj)
            ty_l.append(offsets.index(off) if off in offsets else -1)
            fin_l.append(int(kj == kjs[-1]))
    as_i32 = lambda v: jnp.asarray(np.asarray(v, np.int32))
    return offsets, as_i32(qi_l), as_i32(kj_l), as_i32(ty_l), as_i32(fin_l)


def _bias_table(rpb):
    n = jnp.arange(RPB_MAX_DIST, dtype=jnp.int32)
    max_exact = RPB_BUCKETS // 2
    nf = jnp.maximum(n, 1).astype(F32)
    large = max_exact + (jnp.log(nf / max_exact) / math.log(RPB_MAX_DIST / max_exact)
                         * (RPB_BUCKETS - max_exact)).astype(jnp.int32)
    bucket = jnp.where(n < max_exact, n, jnp.minimum(large, RPB_BUCKETS - 1))
    return (rpb[bucket] - rpb[RPB_BUCKETS - 1][None, :]).T.astype(F32)


def _bias_of_distance(tbl, dist):
    d = np.asarray(dist)
    idx = jnp.asarray(np.clip(d, 0, RPB_MAX_DIST - 1).astype(np.int32))
    vals = jnp.take(tbl, idx, axis=1)
    vals = jnp.where(jnp.asarray(d >= RPB_MAX_DIST), 0.0, vals)
    return jnp.where(jnp.asarray(d < 0), -jnp.inf, vals)


def _toeplitz_kernel(v_ref, o_ref):
    tk, tq = o_ref.shape[1:]
    rows = jnp.broadcast_to(v_ref[0], (tk, v_ref.shape[2]))
    o_ref[0] = pltpu.roll(rows, 0, 1, stride=1, stride_axis=0)[:, :tq]


def _bias_tiles_t(tbl, offsets, tq, tk):
    period = tq + tk
    w = np.arange(period)
    u = np.where(w < tq, w, w - period)
    vext = _bias_of_distance(tbl, np.stack([off + u for off in offsets]))
    H, n_types = vext.shape[:2]
    tiles = pl.pallas_call(
        _toeplitz_kernel,
        grid=(H * n_types,),
        in_specs=[pl.BlockSpec((1, 1, period), lambda i: (i, 0, 0))],
        out_specs=pl.BlockSpec((1, tk, tq), lambda i: (i, 0, 0)),
        out_shape=jax.ShapeDtypeStruct((H * n_types, tk, tq), F32),
        compiler_params=_params("parallel"),
        name="bias_tiles",
    )(vext.reshape(H * n_types, 1, period))
    return tiles.reshape(H, n_types, tk, tq)


def _attn_prompt(q16, k16, vt16, tbl, lam, ghn_col, lam_init, tq, tk):
    B, T, W = q16.shape
    H = W // LANES
    hps = ATTN_HEADS_PER_STEP
    hw = hps * LANES
    assert H % hps == 0
    offsets, qi, kj, ty, fin = _attn_schedule(T, tq, tk)
    bias = _bias_tiles_t(tbl, offsets, tq, tk)
    kern = functools.partial(_attn_kernel, n_types=len(offsets), lam_init=lam_init)
    grid_spec = pltpu.PrefetchScalarGridSpec(
        num_scalar_prefetch=4,
        grid=(H // hps, B, int(qi.shape[0])),
        in_specs=[pl.BlockSpec((1, tq, hw), lambda h, b, s, qi, kj, ty, fin: (b, qi[s], h)),
                  pl.BlockSpec((1, tk, hw), lambda h, b, s, qi, kj, ty, fin: (b, kj[s], h)),
                  pl.BlockSpec((1, hw, tk), lambda h, b, s, qi, kj, ty, fin: (b, h, kj[s])),
                  pl.BlockSpec((hps, len(offsets), tk, tq), lambda h, b, s, *_: (h, 0, 0, 0)),
                  pl.BlockSpec(lam.shape, lambda h, b, s, *_: (0, 0)),
                  pl.BlockSpec((hw, 1), lambda h, b, s, *_: (h, 0))],
        out_specs=pl.BlockSpec((1, tq, hw), lambda h, b, s, qi, kj, ty, fin: (b, qi[s], h)),
        scratch_shapes=[pltpu.VMEM((hps, 2 * tq, LANES), BF16), pltpu.VMEM((hps, 1, 2 * tq), F32),
                        pltpu.VMEM((hps, B_DV + ONES_ROWS, 2 * tq), F32)],
    )
    return pl.pallas_call(
        kern,
        grid_spec=grid_spec,
        out_shape=jax.ShapeDtypeStruct((B, T, W), BF16),
        compiler_params=_params("parallel", "parallel", "arbitrary"),
        name="attn_prompt",
    )(qi, kj, ty, fin, q16, k16, vt16, bias, lam, ghn_col)


PAGES_PER_STEP = 8


def _attn_paged_kernel(pt_ref, q_ref, *refs, heads, t_new, page, n_steps, lam_init):
    pps = PAGES_PER_STEP
    kc_refs, vc_refs = refs[:pps], refs[pps:2 * pps]
    kn_ref, vn_ref, blast_ref, bnew_ref, lam_ref, ghn_ref, o_ref, qm_sc, m_sc, l_sc, acc_sc = refs[2 * pps:]
    j = pl.program_id(1)
    R = SUBLANES
    hsl = lambda h: slice(h * R, (h + 1) * R)
    lsl = lambda h: slice(h * LANES, (h + 1) * LANES)

    @pl.when(j == 0)
    def _():
        q = q_ref[0]
        row = lax.broadcasted_iota(jnp.int32, (R, LANES), 0)
        lane = lax.broadcasted_iota(jnp.int32, (R, LANES), 1)
        keep = (row < t_new) == (lane < B_DK)
        for h in range(heads):
            qm_sc[hsl(h), :] = jnp.where(keep, q[:, lsl(h)], 0.0)
        m_sc[...] = jnp.full_like(m_sc, -jnp.inf)
        l_sc[...] = jnp.zeros_like(l_sc)
        acc_sc[...] = jnp.zeros_like(acc_sc)

    qm = qm_sc[...].astype(BF16)

    def update(s, pv_of):
        m_old = m_sc[...]
        m_new = jnp.maximum(m_old, jnp.max(s, axis=1, keepdims=True))
        alpha = jnp.exp(m_old - m_new)
        p = jnp.exp(s - m_new)
        l_sc[...] = alpha * l_sc[...] + jnp.sum(p, axis=1, keepdims=True)
        pb = p.astype(BF16)
        acc_sc[...] = alpha * acc_sc[...] + jnp.concatenate([pv_of(h, pb[hsl(h)]) for h in range(heads)], axis=0)
        m_sc[...] = m_new

    s = jnp.concatenate(
        [jnp.concatenate([jnp.dot(qm[hsl(h)], kc[0, lsl(h), :].astype(BF16), preferred_element_type=F32)
                          for h in range(heads)], axis=0) for kc in kc_refs], axis=1)
    s = s + jnp.where(j == n_steps - 1, blast_ref[...], 0.0)

    def pv_cached(h, ph):
        parts = [jnp.dot(ph[:, u * page:(u + 1) * page], vc[0, pl.ds(h, page, stride=heads), :].astype(BF16),
                         preferred_element_type=F32) for u, vc in enumerate(vc_refs)]
        return functools.reduce(lambda a, b: a + b, parts)

    update(s, pv_cached)

    @pl.when(j == n_steps - 1)
    def _():
        kn = kn_ref[0].astype(BF16)
        vn = vn_ref[0].astype(BF16)
        s_new = jnp.concatenate([_nt_dot(qm[hsl(h)], kn[:, lsl(h)]) for h in range(heads)], axis=0) + bnew_ref[...]
        update(s_new, lambda h, ph: jnp.dot(ph, vn[:, lsl(h)], preferred_element_type=F32))
        lam = _lambda(lam_ref, lam_init)
        full = acc_sc[...] / l_sc[...]
        for h in range(heads):
            fh = full[hsl(h)]
            o = fh - lam * pltpu.roll(fh, R - t_new, 0)
            on = o * lax.rsqrt(jnp.mean(o * o, axis=1, keepdims=True) + EPS)
            o_ref[0, :, h * LANES:(h + 1) * LANES] = on * ghn_ref[:, h * LANES:(h + 1) * LANES] * (1.0 - lam_init)


def _attn_paged(q, cache_k, cache_v, page_table, k_new, v_new, tbl, lam, ghn, lam_init):
    B, t_new, W = q.shape
    H = W // LANES
    n_pool, page = cache_k.shape[:2]
    n_pages = page_table.shape[1]
    past = n_pages * page
    R = SUBLANES
    pps = PAGES_PER_STEP
    assert 2 * t_new == R and page >= RPB_MAX_DIST and n_pages % pps == 0
    n_steps = n_pages // pps
    pad = lambda a: jnp.concatenate([a, jnp.zeros((B, R - t_new, W), a.dtype)], axis=1)
    q8 = jnp.concatenate([q, q], axis=1)
    t = np.arange(R)[:, None] % t_new
    d_last = past + t - ((n_pages - 1) * page + np.arange(page)[None, :])
    c = np.arange(R)[None, :]
    d_new = np.where(c < t_new, t - c, -1)
    flat = lambda b: b.reshape(H * R, b.shape[-1])
    bias_last = jnp.pad(flat(_bias_of_distance(tbl, d_last)), ((0, 0), ((pps - 1) * page, 0)))
    bias_new = flat(_bias_of_distance(tbl, d_new))
    kern = functools.partial(_attn_paged_kernel, heads=H, t_new=t_new, page=page, n_steps=n_steps, lam_init=lam_init)
    page_spec = lambda rows, width, u: pl.BlockSpec(
        (1, rows, width), lambda b, j, pt: (pt[b * n_pages + j * pps + u], 0, 0))
    per_b = pl.BlockSpec((1, R, W), lambda b, j, pt: (b, 0, 0))
    full = lambda a: pl.BlockSpec(a.shape, lambda b, j, pt: (0,) * a.ndim)
    grid_spec = pltpu.PrefetchScalarGridSpec(
        num_scalar_prefetch=1,
        grid=(B, n_steps),
        in_specs=[per_b] + [page_spec(W, page, u) for u in range(pps)]
                 + [page_spec(page * H, B_DV, u) for u in range(pps)]
                 + [per_b, per_b, full(bias_last), full(bias_new), full(lam), full(ghn)],
        out_specs=per_b,
        scratch_shapes=[pltpu.VMEM((H * R, LANES), F32), pltpu.VMEM((H * R, 1), F32),
                        pltpu.VMEM((H * R, 1), F32), pltpu.VMEM((H * R, B_DV), F32)],
    )
    ck = jnp.transpose(cache_k, (0, 2, 3, 4, 1)).reshape(n_pool, W, page)
    cv = cache_v.reshape(n_pool, page * H, B_DV)
    out = pl.pallas_call(
        kern,
        grid_spec=grid_spec,
        out_shape=jax.ShapeDtypeStruct((B, R, W), F32),
        compiler_params=_params("parallel", "arbitrary"),
        name="attn_paged",
    )(page_table.reshape(-1), q8, *([ck] * pps), *([cv] * pps), pad(k_new), pad(v_new),
      bias_last, bias_new, lam, ghn)
    return out[:, :t_new]


def _trunk(x, mods, mods_kv, state, past, wts):
    B, T, D = x.shape
    N = B * T
    tm = ROW_TILE
    tb = min(MOE_BLOCK, N)
    tr = min(ROUTER_TILE, N)
    mod = lambda m: _Mod(m, T, tm)
    split3 = lambda m: (mod(m[:, :D]), mod(m[:, D:2 * D]), mod(m[:, 2 * D:]))
    rmod = lambda m: _Mod(m, T, tr)
    x2 = x.reshape(N, D)

    heads = A_HEADS
    inner = wts["w_out16"].shape[0]
    dh = inner // heads
    gate = rmod(mods[0][:, 2 * D:])
    if state is None:
        t_rows, x_in = T, x2
    else:
        assert T <= SUBLANES
        t_rows = SUBLANES
        x_in = jnp.concatenate([x, jnp.zeros((B, t_rows - T, D), x.dtype)], axis=1).reshape(B * t_rows, D)
    t_in = min(INPROJ_TILE, B * t_rows)
    proj, gates = _inproj(x_in, _Mod(mods[0][:, :D], t_rows, t_in), _Mod(mods[0][:, D:2 * D], t_rows, t_in),
                          wts["g_norm"][0, 0][None], wts["w_in16"], wts["w_gate"], t_in)
    conv_new = proj.reshape(B, t_rows, 4 * inner)[:, T - (A_CONV - 1):T, :2 * inner]
    proj = proj.reshape(B, t_rows, 4 * inner)
    gates = gates.reshape(B, t_rows, LANES)
    if state is None:
        L, t_valid = math.gcd(T, MLSTM_CHUNK), None
        conv_init = jnp.zeros((B, SUBLANES, 2 * inner), F32)
        c0 = jnp.zeros((B, heads, dh, dh), F32)
        n0 = jnp.zeros((B, heads, 1, dh), F32)
        m0 = jnp.zeros((B, heads, 1, 1), F32)
    else:
        conv_st, c_st, n_st, m_st = state
        L, t_valid = SUBLANES, T
        conv_init = jnp.concatenate([jnp.zeros((B, SUBLANES - (A_CONV - 1), 2 * inner), F32), conv_st], axis=1)
        c0, n0, m0 = c_st, n_st[:, :, None, :], m_st[:, :, None, None]
    gates_t = jnp.swapaxes(gates[:, :, :SUBLANES], 1, 2)
    hs, c1, n1, m1 = _mlstm(proj, gates, gates_t, wts["bg"], wts["bgt"], conv_init, c0, n0, m0,
                            wts["w_conv"], wts["b_conv"], wts["g_hn_a"], L, t_valid)
    hs = hs[:, :T].reshape(N, inner)
    new_state = (conv_new, c1, n1[:, :, 0, :], m1[:, :, 0, 0])

    x2, h16, rg = _proj_router(hs, wts["w_out16"], x2, gate, rmod(mods[1][:, :D]), rmod(mods[1][:, D:2 * D]),
                               wts["g_norm"][0, 1][None], wts["w_router"], wts["b_router"], tr)
    x2 = _moe(h16, rg, x2, _Mod(mods[1][:, 2 * D:], T, tb), wts["wg16"], wts["wu16"], wts["wd16"], 0, tb)

    shift_kv, scale_kv = mod(mods_kv[:, :D]), mod(mods_kv[:, D:])
    shift, scale, gate = split3(mods[2])
    kvq = _kvq(x2, shift_kv, scale_kv, wts["g_kv"], shift, scale, wts["g_norm"][1, 0][None],
               wts["w_kv16"], wts["w_q16"], wts["gmat"], wts["gmat_t"], wts["g_kn"], wts["g_qn"], tm, T)
    v32 = kvq[1]
    W = v32.shape[1]
    H = W // LANES
    lam_init = 0.8 - 0.6 * math.exp(-0.3 * 1)
    if past is None:
        kt32, _, q, k16, vt16 = kvq
        o = _attn_prompt(q.reshape(B, T, W), k16.reshape(B, T, W), vt16, wts["rpb_tbl"], wts["lam"],
                         wts["g_hn_b"].reshape(W, 1), lam_init, math.gcd(T, ATTN_TQ), math.gcd(T, ATTN_TK))
        o = o.reshape(N, W)
        k_out = jnp.transpose(kt32.reshape(B, H, 2, B_DK, T), (0, 4, 1, 2, 3))
    else:
        k32, _, q = kvq
        k_out = k32.reshape(B, T, H, 2, B_DK)
        cache_k, cache_v, page_table = past
        o = _attn_paged(q.reshape(B, T, W), cache_k, cache_v, page_table, k32.reshape(B, T, W),
                        v32.reshape(B, T, W), wts["rpb_tbl"], wts["lam"], wts["g_hn_b"], lam_init)
        o = o.reshape(N, W).astype(BF16)

    x2, h16, rg = _proj_router(o, wts["w_o16"], x2, rmod(mods[2][:, 2 * D:]), rmod(mods[3][:, :D]),
                               rmod(mods[3][:, D:2 * D]), wts["g_norm"][1, 1][None], wts["w_router"],
                               wts["b_router"], tr)
    x2 = _moe(h16, rg, x2, _Mod(mods[3][:, 2 * D:], T, tb), wts["wg16"], wts["wu16"], wts["wd16"], 1, tb)

    return x2.reshape(B, T, D), new_state, k_out, v32.reshape(B, T, H, B_DV)


def kernel(x_prompt, x_sample, c_prompt, c_sample, state_conv, state_C, state_n, state_m, cache_k, cache_v, page_table, w_ada, b_ada, g_norm, w_in_a, b_gate_a, w_conv_a, b_conv_a, g_hn_a, w_out_a, g_kv, w_ada_kv, b_ada_kv, w_kv, g_kn, w_q_b, g_qn_b, lam_b, g_hn_b, w_o_b, rpb, w_router, b_router, w_gate_e, w_up_e, w_down_e):
    Bp, Tp, D = x_prompt.shape
    Bs = x_sample.shape[0]
    inner = w_out_a.shape[1]
    heads_b = g_hn_b.shape[1]
    W = heads_b * B_DV

    n_c = Bp + Bs
    c_all = jnp.concatenate([c_prompt, c_sample, jnp.zeros((-n_c % SUBLANES, D), F32)], axis=0)
    mods = _ada(c_all, w_ada.reshape(-1, D, 3 * D), b_ada.reshape(-1, 1, 3 * D))
    mods_kv = _ada(c_all, w_ada_kv[None], b_ada_kv[None, None])[0]

    n_gate = 2 * A_HEADS
    w_in_t = jnp.swapaxes(w_in_a[0], 0, 1)
    group_of_lane = np.arange(W) // B_DK
    gmat = jnp.asarray((group_of_lane[:, None] == np.arange(LANES)[None, :]).astype(np.float32)).astype(BF16)
    wts = {
        "g_norm": g_norm,
        "w_in16": w_in_t[:4 * inner].astype(BF16),
        "w_gate": jnp.pad(w_in_t[4 * inner:], ((0, LANES - n_gate), (0, 0))),
        "bg": jnp.pad(b_gate_a[0], (0, LANES - n_gate))[None, :],
        "bgt": b_gate_a[0][:, None],
        "w_conv": w_conv_a[0], "b_conv": b_conv_a[0][None, :],
        "g_hn_a": g_hn_a[0].reshape(1, inner),
        "w_out16": w_out_a[0].astype(BF16),
        "g_kv": g_kv[None, :],
        "w_kv16": w_kv.astype(BF16), "w_q16": w_q_b[0].astype(BF16),
        "gmat": gmat, "gmat_t": gmat.T,
        "g_kn": jnp.tile(g_kn.reshape(-1), heads_b)[None, :],
        "g_qn": jnp.tile(g_qn_b[0].reshape(-1), heads_b)[None, :],
        "lam": lam_b[0], "g_hn_b": g_hn_b[0].reshape(1, W),
        "w_o16": w_o_b[0].astype(BF16),
        "rpb_tbl": _bias_table(rpb),
        "w_router": jnp.pad(w_router, ((0, 0), (0, LANES - N_EXPERTS))),
        "b_router": b_router[:, None],
        "wg16": w_gate_e.astype(BF16), "wu16": w_up_e.astype(BF16), "wd16": w_down_e.astype(BF16),
    }

    y_p, st_p, k_p, v_p = _trunk(x_prompt, mods[:, :Bp], mods_kv[:Bp], None, None, wts)
    y_s, st_s, k_s, v_s = _trunk(x_sample, mods[:, Bp:n_c], mods_kv[Bp:n_c],
                                 (state_conv[0], state_C[0], state_n[0], state_m[0]),
                                 (cache_k, cache_v, page_table), wts)
    stack = lambda st: tuple(a[None] for a in st)
    return (y_p, y_s) + stack(st_p) + (k_p, v_p) + stack(st_s) + (k_s, v_s)
```

```python
import functools
import math

import numpy as np
import jax
import jax.numpy as jnp
from jax import lax
from jax.experimental import pallas as pl
from jax.experimental.pallas import tpu as pltpu

F32, BF16 = jnp.float32, jnp.bfloat16
HIGHEST = lax.Precision.HIGHEST
EPS = 1e-6

A_HEADS = 4
A_CONV = 4
B_DK = 64
B_DV = 128
N_EXPERTS = 16
N_GROUPS = 4
RPB_BUCKETS = 32
RPB_MAX_DIST = 128

LANES = 128
SUBLANES = 8
VMEM_LIMIT_BYTES = 56 * 1024 * 1024

ROW_TILE = 512
INPROJ_TILE = 1024
ROUTER_TILE = 1024
MLSTM_CHUNK = 256
ATTN_TQ = 512
ATTN_TK = 512


def _params(*sem):
    return pltpu.CompilerParams(dimension_semantics=sem, vmem_limit_bytes=VMEM_LIMIT_BYTES)


def _nt_dot(a, b):
    return lax.dot_general(a, b, (((1,), (1,)), ((), ())), preferred_element_type=F32)


def _tn_dot(a, b):
    return lax.dot_general(a, b, (((0,), (0,)), ((), ())), preferred_element_type=F32)


def _silu(x):
    return x * jax.nn.sigmoid(x)


def _rms_mod(x, g, scale, shift):
    y = x * lax.rsqrt(jnp.mean(x * x, axis=-1, keepdims=True) + EPS)
    return (y * g) * (1.0 + scale) + shift


def _dot_split(a, b16):
    hi = a.astype(BF16)
    lo = (a - hi.astype(F32)).astype(BF16)
    return (jnp.dot(hi, b16, preferred_element_type=F32) + jnp.dot(lo, b16, preferred_element_type=F32))


def _dot_x3(a, b, dims):
    a_hi, b_hi = a.astype(BF16), b.astype(BF16)
    a_lo = (a - a_hi.astype(F32)).astype(BF16)
    b_lo = (b - b_hi.astype(F32)).astype(BF16)
    dot = lambda x, y: lax.dot_general(x, y, dims, preferred_element_type=F32)
    return dot(a_hi, b_hi) + dot(a_hi, b_lo) + dot(a_lo, b_hi)


def _group_rms(x, gmat, gmat_t, g, group):
    ss = jnp.dot((x * x).astype(BF16), gmat, preferred_element_type=F32)
    r = lax.rsqrt(ss * (1.0 / group) + EPS)
    rf = _dot_split(r, gmat_t)
    return x * rf * g


def _ada_kernel(c_ref, w_ref, b_ref, o_ref):
    a = _silu(c_ref[...])
    o_ref[0] = _dot_x3(a, w_ref[0], (((1,), (0,)), ((), ()))) + b_ref[0]


def _ada(c_all, w, b):
    S, D, Fo = w.shape
    R = c_all.shape[0]
    tn = 1024
    return pl.pallas_call(
        _ada_kernel,
        grid=(S, Fo // tn),
        in_specs=[pl.BlockSpec((R, D), lambda s, j: (0, 0)),
                  pl.BlockSpec((1, D, tn), lambda s, j: (s, 0, j)),
                  pl.BlockSpec((1, 1, tn), lambda s, j: (s, 0, j))],
        out_specs=pl.BlockSpec((1, R, tn), lambda s, j: (s, 0, j)),
        out_shape=jax.ShapeDtypeStruct((S, R, Fo), F32),
        compiler_params=_params("parallel", "parallel"),
        name="ada",
    )(c_all, w, b)


class _Mod:
    def __init__(self, m, T, tm):
        B, D = m.shape
        if T % tm == 0:
            self.arr, self.tiles_per_group = m[:, None, :], T // tm
        else:
            assert (B * T) % tm == 0
            self.arr, self.tiles_per_group = jnp.repeat(m, T, axis=0).reshape(-1, tm, D), 1

    def spec(self, grid_rank):
        R, D = self.arr.shape[1:]
        tpg = self.tiles_per_group
        if grid_rank == 1:
            return pl.BlockSpec((1, R, D), lambda i: (i // tpg, 0, 0))
        return pl.BlockSpec((1, R, D), lambda i, j: (i // tpg, 0, 0))


def _causal_conv(x, tail, w, b):
    L, width = x.shape
    row8 = lax.broadcasted_iota(jnp.int32, (SUBLANES, width), 0)
    acc = b + x * w[A_CONV - 1:A_CONV]
    for s in range(1, A_CONV):
        xs = pltpu.roll(x, s, 0)
        top = jnp.where(row8 < s, pltpu.roll(tail, s, 0), xs[:SUBLANES])
        xs = top if L == SUBLANES else jnp.concatenate([top, xs[SUBLANES:]], axis=0)
        acc = acc + xs * w[A_CONV - 1 - s:A_CONV - s]
    return acc


def _inproj_kernel(x_ref, sh_ref, sc_ref, g_ref, w_ref, wg_ref, o_ref, og_ref, h_sc):
    @pl.when(pl.program_id(1) == 0)
    def _():
        h = _rms_mod(x_ref[...], g_ref[...], sc_ref[0], sh_ref[0])
        h_sc[...] = h.astype(BF16)
        og_ref[...] = _dot_x3(h, wg_ref[...], (((1,), (1,)), ((), ())))

    o_ref[...] = _nt_dot(h_sc[...], w_ref[...])


def _inproj(x2, shift, scale, g, w16_t, wgate_t, tm):
    N, D = x2.shape
    Fo = w16_t.shape[0]
    tn = 2048
    return pl.pallas_call(
        _inproj_kernel,
        grid=(N // tm, Fo // tn),
        in_specs=[pl.BlockSpec((tm, D), lambda i, j: (i, 0)),
                  shift.spec(2), scale.spec(2),
                  pl.BlockSpec((1, D), lambda i, j: (0, 0)),
                  pl.BlockSpec((tn, D), lambda i, j: (j, 0)),
                  pl.BlockSpec((LANES, D), lambda i, j: (0, 0))],
        out_specs=[pl.BlockSpec((tm, tn), lambda i, j: (i, j)),
                   pl.BlockSpec((tm, LANES), lambda i, j: (i, 0))],
        out_shape=[jax.ShapeDtypeStruct((N, Fo), F32), jax.ShapeDtypeStruct((N, LANES), F32)],
        scratch_shapes=[pltpu.VMEM((tm, D), BF16)],
        compiler_params=_params("parallel", "arbitrary"),
        name="mlstm_inproj",
    )(x2, shift.arr, scale.arr, g, w16_t, wgate_t)


def _mlstm_kernel(q_ref, k_ref, v_ref, o_ref, gt_ref, gtt_ref, bg_ref, bgt_ref, cinit_ref,
                  c0_ref, n0_ref, m0_ref, wconv_ref, bconv_ref, ghn_ref,
                  hs_ref, c_ref, n_ref, m_ref, tail_sc, *, L, dh, heads, t_valid):
    inner = heads * dh

    @pl.when(pl.program_id(1) == 0)
    def _():
        c_ref[...] = c0_ref[...]
        n_ref[...] = n0_ref[...]
        m_ref[...] = m0_ref[...]
        tail_sc[...] = cinit_ref[0]

    gt = gt_ref[0] + bg_ref[...]
    gtt = gtt_ref[0] + bgt_ref[...]
    ti = lax.broadcasted_iota(jnp.int32, (L, L), 0)
    si = lax.broadcasted_iota(jnp.int32, (L, L), 1)
    causal = si <= ti
    tcol = lax.broadcasted_iota(jnp.int32, (L, 1), 0)
    trow = lax.broadcasted_iota(jnp.int32, (1, L), 1)

    for h in range(heads):
        sl = slice(h * dh, (h + 1) * dh)
        slk = slice(inner + h * dh, inner + (h + 1) * dh)
        qh = _silu(_causal_conv(q_ref[0, :, sl], tail_sc[:, sl], wconv_ref[:, sl], bconv_ref[:, sl]))
        kh = _silu(_causal_conv(k_ref[0, :, sl], tail_sc[:, slk], wconv_ref[:, slk], bconv_ref[:, slk])) * (dh ** -0.5)
        vb = v_ref[0, :, sl].astype(BF16)

        ig_col = gt[:, h:h + 1]
        lf_col = jax.nn.log_sigmoid(gt[:, heads + h:heads + h + 1])
        ig_row = gtt[h:h + 1, :]
        lf_row = jax.nn.log_sigmoid(gtt[heads + h:heads + h + 1, :])
        if t_valid is not None:
            ig_col = jnp.where(tcol < t_valid, ig_col, -jnp.inf)
            lf_col = jnp.where(tcol < t_valid, lf_col, 0.0)
            ig_row = jnp.where(trow < t_valid, ig_row, -jnp.inf)
            lf_row = jnp.where(trow < t_valid, lf_row, 0.0)

        b_col = jnp.sum(jnp.where(causal, lf_row, 0.0), axis=1, keepdims=True)
        b_row = jnp.sum(jnp.where(ti <= si, lf_col, 0.0), axis=0, keepdims=True)
        dlog = jnp.where(causal, b_col - b_row + ig_row, -jnp.inf)
        g_col = b_col + m_ref[0, h]
        m_col = jnp.maximum(g_col, jnp.max(dlog, axis=1, keepdims=True))
        w_intra = jnp.exp(dlog - m_col)
        w_inter = jnp.exp(g_col - m_col)

        qb = qh.astype(BF16)
        kb = kh.astype(BF16)
        s = w_intra * _nt_dot(qb, kb)
        ch = c_ref[0, h]
        nh = n_ref[0, h]
        num = w_inter * _nt_dot(qb, ch.astype(BF16)) + jnp.dot(s.astype(BF16), vb, preferred_element_type=F32)
        den = w_inter * jnp.sum(qh * nh, axis=1, keepdims=True) + jnp.sum(s, axis=1, keepdims=True)
        hv = num / jnp.maximum(jnp.abs(den), jnp.exp(-m_col))

        m_end = m_col[L - 1:L]
        we_inter = jnp.exp(g_col[L - 1:L] - m_end)
        we_col = jnp.exp(b_col[L - 1:L] - b_col + ig_col - m_end)
        kw = kh * we_col
        c_ref[0, h] = we_inter * ch + _tn_dot(vb, kw.astype(BF16))
        n_ref[0, h] = we_inter * nh + jnp.sum(kw, axis=0, keepdims=True)
        m_ref[0, h] = m_end

        hn = hv * lax.rsqrt(jnp.mean(hv * hv, axis=1, keepdims=True) + EPS) * ghn_ref[:, sl]
        hs_ref[0, :, sl] = (jax.nn.sigmoid(o_ref[0, :, sl]) * hn).astype(hs_ref.dtype)

    tail_sc[:, :inner] = q_ref[0, L - SUBLANES:, :]
    tail_sc[:, inner:] = k_ref[0, L - SUBLANES:, :]


def _mlstm(proj, gates, gates_t, bg, bgt, conv_init, c0, n0, m0, wconv, bconv, ghn, L, t_valid):
    B, Tp, _ = proj.shape
    heads, dh = c0.shape[1], c0.shape[2]
    inner = heads * dh
    nc = Tp // L
    kern = functools.partial(_mlstm_kernel, L=L, dh=dh, heads=heads, t_valid=t_valid)
    col = lambda j: pl.BlockSpec((1, L, inner), lambda b, c: (b, c, j))
    full = lambda shape: pl.BlockSpec(shape, lambda b, c: (0,) * len(shape))
    per_b = lambda shape: pl.BlockSpec((1,) + shape, lambda b, c: (b,) + (0,) * len(shape))
    return pl.pallas_call(
        kern,
        grid=(B, nc),
        in_specs=[col(0), col(1), col(2), col(3),
                  pl.BlockSpec((1, L, LANES), lambda b, c: (b, c, 0)),
                  pl.BlockSpec((1, SUBLANES, L), lambda b, c: (b, 0, c)),
                  full((1, LANES)), full((SUBLANES, 1)),
                  per_b((SUBLANES, 2 * inner)),
                  per_b((heads, dh, dh)), per_b((heads, 1, dh)), per_b((heads, 1, 1)),
                  full((A_CONV, 2 * inner)), full((1, 2 * inner)), full((1, inner))],
        out_specs=[pl.BlockSpec((1, L, inner), lambda b, c: (b, c, 0)),
                   per_b((heads, dh, dh)), per_b((heads, 1, dh)), per_b((heads, 1, 1))],
        out_shape=[jax.ShapeDtypeStruct((B, Tp, inner), BF16),
                   jax.ShapeDtypeStruct((B, heads, dh, dh), F32),
                   jax.ShapeDtypeStruct((B, heads, 1, dh), F32),
                   jax.ShapeDtypeStruct((B, heads, 1, 1), F32)],
        scratch_shapes=[pltpu.VMEM((SUBLANES, 2 * inner), F32)],
        compiler_params=_params("parallel", "arbitrary"),
        name="mlstm",
    )(proj, proj, proj, proj, gates, gates_t, bg, bgt, conv_init, c0, n0, m0, wconv, bconv, ghn)


ROUTE_ROWS = SUBLANES


def _route(h, wr, br):
    tm = h.shape[0]
    per = N_EXPERTS // N_GROUPS
    logits = _dot_x3(h, wr, (((1,), (0,)), ((), ())))
    lt = logits.T[:N_EXPERTS]
    s = jax.nn.sigmoid(lt)
    sel = s + br
    neg = jnp.full((1, tm), -jnp.inf, F32)
    izero = jnp.zeros((1, tm), jnp.int32)

    best_score = best_e1 = best_e2 = best_w1 = best_w2 = None
    for grp in range(N_GROUPS):
        rows = [sel[grp * per + j:grp * per + j + 1] for j in range(per)]
        srow = [s[grp * per + j:grp * per + j + 1] for j in range(per)]
        t1, i1, w1 = rows[0], izero, srow[0]
        for j in range(1, per):
            better = rows[j] > t1
            t1 = jnp.where(better, rows[j], t1)
            i1 = jnp.where(better, j, i1)
            w1 = jnp.where(better, srow[j], w1)
        t2, i2, w2 = neg, izero, srow[0]
        for j in range(per):
            better = jnp.where(i1 == j, neg, rows[j]) > t2
            t2 = jnp.where(better, rows[j], t2)
            i2 = jnp.where(better, j, i2)
            w2 = jnp.where(better, srow[j], w2)
        score = t1 + t2
        e1, e2 = i1 + grp * per, i2 + grp * per
        if grp == 0:
            best_score, best_e1, best_e2, best_w1, best_w2 = score, e1, e2, w1, w2
        else:
            better = score > best_score
            best_score = jnp.where(better, score, best_score)
            best_e1 = jnp.where(better, e1, best_e1)
            best_e2 = jnp.where(better, e2, best_e2)
            best_w1 = jnp.where(better, w1, best_w1)
            best_w2 = jnp.where(better, w2, best_w2)
    tot = best_w1 + best_w2
    zero = jnp.zeros((ROUTE_ROWS - 4, tm), F32)
    return jnp.concatenate([best_e1.astype(F32), best_e2.astype(F32), best_w1 / tot, best_w2 / tot, zero], axis=0)


def _proj_router_kernel(a_ref, w_ref, x_ref, gate_ref, sh_ref, sc_ref, g_ref, wr_ref, br_ref,
                        xo_ref, h_ref, route_ref):
    mix = jnp.dot(a_ref[...], w_ref[...], preferred_element_type=F32)
    x = x_ref[...] + gate_ref[0] * mix
    xo_ref[...] = x
    h = _rms_mod(x, g_ref[...], sc_ref[0], sh_ref[0])
    h_ref[...] = h.astype(BF16)
    route_ref[...] = _route(h, wr_ref[...], br_ref[...])


def _proj_router(a16, w16, x2, gate, shift, scale, g, wr, br, tm):
    N, D = x2.shape
    K = a16.shape[1]
    row = lambda w: pl.BlockSpec((tm, w), lambda i: (i, 0))
    full = lambda shape: pl.BlockSpec(shape, lambda i: (0,) * len(shape))
    return pl.pallas_call(
        _proj_router_kernel,
        grid=(N // tm,),
        in_specs=[row(K), full((K, D)), row(D), gate.spec(1), shift.spec(1), scale.spec(1),
                  full((1, D)), full((D, LANES)), full((N_EXPERTS, 1))],
        out_specs=[row(D), row(D), pl.BlockSpec((ROUTE_ROWS, tm), lambda i: (0, i))],
        out_shape=[jax.ShapeDtypeStruct((N, D), F32), jax.ShapeDtypeStruct((N, D), BF16),
                   jax.ShapeDtypeStruct((ROUTE_ROWS, N), F32)],
        compiler_params=_params("parallel"),
        name="proj_router",
    )(a16, w16, x2, gate.arr, shift.arr, scale.arr, g, wr, br)


MOE_BLOCK = 1024
MOE_WINDOW = 256
MOE_CHUNK = 256
SEG_ALIGN = 16


def _moe_kernel(route_ref, h_ref, x_ref, gate_ref, tri_ref, wg_ref, wu_ref, wd_ref, o_ref,
                xs_sc, ys_sc, gs_sc, tok_sc, seg_sc, *, n_experts, n_sorted):
    e = pl.program_id(1)
    tb = h_ref.shape[0]
    S, CH, RW = n_sorted, MOE_CHUNK, MOE_WINDOW
    one_hot = lambda a, b: jnp.where(a, 1.0, jnp.where(b, 1.0, 0.0))

    @pl.when(e == 0)
    def _dispatch():
        route = route_ref[...]
        e1, e2, w1, w2 = route[0:1], route[1:2], route[2:3], route[3:4]
        eid = lax.broadcasted_iota(jnp.int32, (n_experts, tb), 0).astype(F32)
        hit1, hit2 = eid == e1, eid == e2
        routed = one_hot(hit1, hit2)
        earlier = jnp.dot(routed.astype(BF16), tri_ref[...], preferred_element_type=F32)
        count = jnp.sum(routed, axis=1, keepdims=True)
        padded = jnp.floor((count + (SEG_ALIGN - 1)) * (1.0 / SEG_ALIGN)) * SEG_ALIGN
        below = jnp.where(lax.broadcasted_iota(jnp.int32, (n_experts, n_experts), 1)
                          < lax.broadcasted_iota(jnp.int32, (n_experts, n_experts), 0), 1.0, 0.0)
        start = jnp.dot(below, jnp.broadcast_to(padded, (n_experts, LANES)), precision=HIGHEST,
                        preferred_element_type=F32)[:, :1]
        pos = start + earlier
        pos1 = jnp.sum(jnp.where(hit1, pos, 0.0), axis=0, keepdims=True)
        pos2 = jnp.sum(jnp.where(hit2, pos, 0.0), axis=0, keepdims=True)
        for ex in range(n_experts):
            seg_sc[0, ex] = start[ex, 0].astype(jnp.int32)
            seg_sc[1, ex] = (start[ex, 0] + count[ex, 0]).astype(jnp.int32)
        tok_sc[...] = jnp.concatenate([pos1, pos2, jnp.zeros((LANES - 2, tb), F32)], axis=0).T

        def gather(c, carry):
            r0 = pl.multiple_of(c * CH, CH)
            row = (lax.broadcasted_iota(jnp.int32, (CH, tb), 0) + r0).astype(F32)
            is1, is2 = row == pos1, row == pos2
            xs_sc[pl.ds(r0, CH), :] = jnp.dot(one_hot(is1, is2).astype(BF16), h_ref[...],
                                              preferred_element_type=F32).astype(BF16)
            gs_sc[pl.ds(r0, CH), :] = jnp.sum(jnp.where(is1, w1, jnp.where(is2, w2, 0.0)), axis=1, keepdims=True)
            return carry

        lax.fori_loop(0, S // CH, gather, 0)
        xs_sc[S:, :] = jnp.zeros((RW, xs_sc.shape[1]), BF16)
        gs_sc[S:, :] = jnp.zeros((RW, 1), F32)
        ys_sc[...] = jnp.zeros_like(ys_sc)

    seg_start, seg_end = seg_sc[0, e], seg_sc[1, e]

    def window(w, carry):
        r0 = pl.multiple_of(seg_start + w * RW, SEG_ALIGN)
        rows = xs_sc[pl.ds(r0, RW), :]
        a = jnp.dot(rows, wg_ref[0], preferred_element_type=F32)
        u = jnp.dot(rows, wu_ref[0], preferred_element_type=F32)
        mine = lax.broadcasted_iota(jnp.int32, (RW, 1), 0) + r0 < seg_end
        act = jnp.where(mine, _silu(a) * u * gs_sc[pl.ds(r0, RW), :], 0.0)
        ys_sc[pl.ds(r0, RW), :] += jnp.dot(act.astype(BF16), wd_ref[0], preferred_element_type=F32)
        return carry

    lax.fori_loop(0, (seg_end - seg_start + (RW - 1)) // RW, window, 0)

    @pl.when(e == n_experts - 1)
    def _combine():
        def to_bf16(c, carry):
            r0 = pl.multiple_of(c * CH, CH)
            xs_sc[pl.ds(r0, CH), :] = ys_sc[pl.ds(r0, CH), :].astype(BF16)
            return carry

        lax.fori_loop(0, S // CH, to_bf16, 0)

        def scatter(c, carry):
            t0 = pl.multiple_of(c * CH, CH)
            rec = tok_sc[pl.ds(t0, CH), :]
            col = lax.broadcasted_iota(jnp.int32, (CH, S), 1).astype(F32)
            sel = one_hot(col == rec[:, 0:1], col == rec[:, 1:2]).astype(BF16)
            y = jnp.dot(sel, xs_sc[0:S, :], preferred_element_type=F32)
            g = gate_ref[0] if gate_ref.shape[1] == 1 else gate_ref[0, pl.ds(t0, CH), :]
            o_ref[pl.ds(t0, CH), :] = x_ref[pl.ds(t0, CH), :] + g * y
            return carry

        lax.fori_loop(0, tb // CH, scatter, 0)


def _moe(h16, route, x2, gate, wg16, wu16, wd16, layer, tb):
    N, D = x2.shape
    _, E, _, Fe = wg16.shape
    n_sorted = -(-(2 * tb + E * SEG_ALIGN) // MOE_CHUNK) * MOE_CHUNK
    tri = jnp.asarray(np.triu(np.ones((tb, tb), np.float32), 1), dtype=BF16)
    row = lambda w: pl.BlockSpec((tb, w), lambda i, e: (i, 0))
    kern = functools.partial(_moe_kernel, n_experts=E, n_sorted=n_sorted)
    return pl.pallas_call(
        kern,
        grid=(N // tb, E),
        in_specs=[pl.BlockSpec((ROUTE_ROWS, tb), lambda i, e: (0, i)), row(D), row(D), gate.spec(2),
                  pl.BlockSpec((tb, tb), lambda i, e: (0, 0)),
                  pl.BlockSpec((None, 1, D, Fe), lambda i, e: (layer, e, 0, 0)),
                  pl.BlockSpec((None, 1, D, Fe), lambda i, e: (layer, e, 0, 0)),
                  pl.BlockSpec((None, 1, Fe, D), lambda i, e: (layer, e, 0, 0))],
        out_specs=row(D),
        out_shape=jax.ShapeDtypeStruct((N, D), F32),
        scratch_shapes=[pltpu.VMEM((n_sorted + MOE_WINDOW, D), BF16), pltpu.VMEM((n_sorted + MOE_WINDOW, D), F32),
                        pltpu.VMEM((n_sorted + MOE_WINDOW, 1), F32), pltpu.VMEM((tb, LANES), F32),
                        pltpu.SMEM((2, E), jnp.int32)],
        compiler_params=_params("parallel", "arbitrary"),
        name="moe",
    )(route, h16, x2, gate.arr, tri, wg16, wu16, wd16)


def _kvq_kernel(x_ref, shk_ref, sck_ref, gk_ref, shq_ref, scq_ref, gq_ref, wkv_ref, wq_ref,
                gmat_ref, gmatt_ref, gkn_ref, gqn_ref, k_ref, v32_ref, q_ref, *maybe_attn_refs):
    x = x_ref[...]
    y = x * lax.rsqrt(jnp.mean(x * x, axis=-1, keepdims=True) + EPS)
    hk = ((y * gk_ref[...]) * (1.0 + sck_ref[0]) + shk_ref[0]).astype(BF16)
    hq = ((y * gq_ref[...]) * (1.0 + scq_ref[0]) + shq_ref[0]).astype(BF16)
    W = v32_ref.shape[1]
    kv = jnp.dot(hk, wkv_ref[...], preferred_element_type=F32)
    k = _group_rms(kv[:, :W], gmat_ref[...], gmatt_ref[...], gkn_ref[...], B_DK)
    v = kv[:, W:]
    v32_ref[...] = v
    q = jnp.dot(hq, wq_ref[...], preferred_element_type=F32)
    q = _group_rms(q, gmat_ref[...], gmatt_ref[...], gqn_ref[...], B_DK)
    q_ref[...] = (q * (B_DK ** -0.5)).astype(q_ref.dtype)
    if maybe_attn_refs:
        k16_ref, vt_ref = maybe_attn_refs
        k_ref[0] = k.T
        k16_ref[...] = k.astype(BF16)
        vt_ref[0] = v.T.astype(BF16)
    else:
        k_ref[...] = k


def _kvq(x2, shk, sck, gk, shq, scq, gq, wkv16, wq16, gmat, gmat_t, gkn, gqn, tm, seq_len):
    N, D = x2.shape
    W = wq16.shape[1]
    row = lambda w: pl.BlockSpec((tm, w), lambda i: (i, 0))
    full = lambda shape: pl.BlockSpec(shape, lambda i: (0,) * len(shape))
    if seq_len % tm == 0:
        tps = seq_len // tm
        col = pl.BlockSpec((1, W, tm), lambda i: (i // tps, 0, i % tps))
        out_specs = [col, row(W), row(W), row(W), col]
        out_shape = [jax.ShapeDtypeStruct((N // seq_len, W, seq_len), F32), jax.ShapeDtypeStruct((N, W), F32),
                     jax.ShapeDtypeStruct((N, W), BF16), jax.ShapeDtypeStruct((N, W), BF16),
                     jax.ShapeDtypeStruct((N // seq_len, W, seq_len), BF16)]
    else:
        out_specs = [row(W), row(W), row(W)]
        out_shape = [jax.ShapeDtypeStruct((N, W), F32)] * 3
    return pl.pallas_call(
        _kvq_kernel,
        grid=(N // tm,),
        in_specs=[row(D), shk.spec(1), sck.spec(1), full((1, D)), shq.spec(1), scq.spec(1), full((1, D)),
                  full((D, 2 * W)), full((D, W)), full((W, LANES)), full((LANES, W)),
                  full((1, W)), full((1, W))],
        out_specs=out_specs,
        out_shape=out_shape,
        compiler_params=_params("parallel"),
        name="kvq",
    )(x2, shk.arr, sck.arr, gk, shq.arr, scq.arr, gq, wkv16, wq16, gmat, gmat_t, gkn, gqn)


def _lambda(lam_ref, lam_init):
    lv = lam_ref[...]
    a = jnp.sum(lv[0:1] * lv[1:2], axis=1, keepdims=True)
    b = jnp.sum(lv[2:3] * lv[3:4], axis=1, keepdims=True)
    return jnp.exp(a) - jnp.exp(b) + lam_init


ATTN_HEADS_PER_STEP = 4
ONES_ROWS = SUBLANES


def _attn_kernel(qi_ref, kj_ref, ty_ref, fin_ref, q_ref, k_ref, vt_ref, bias_ref, lam_ref, ghn_ref, o_ref,
                 qm_sc, m_sc, acc_sc, *, n_types, lam_init):
    step = pl.program_id(2)
    ty = ty_ref[step]
    tq = q_ref.shape[1]
    heads = ATTN_HEADS_PER_STEP
    hl = lambda hh: slice(hh * LANES, (hh + 1) * LANES)

    @pl.when(kj_ref[step] == 0)
    def _():
        q = q_ref[0]
        lane = lax.broadcasted_iota(jnp.int32, (tq, LANES), 1)
        zero = jnp.zeros((tq, LANES), BF16)
        for hh in range(heads):
            qh = q[:, hl(hh)]
            qm_sc[hh, 0:tq, :] = jnp.where(lane < B_DK, qh, zero)
            qm_sc[hh, tq:2 * tq, :] = jnp.where(lane >= B_DK, qh, zero)
        m_sc[...] = jnp.full_like(m_sc, -jnp.inf)
        acc_sc[...] = jnp.zeros_like(acc_sc)

    def update(adj_of):
        ones = jnp.ones((ONES_ROWS, k_ref.shape[1]), BF16)

        def scores(hh):
            return _nt_dot(k_ref[0, :, hl(hh)], qm_sc[hh])

        def softmax(hh, s):
            if adj_of is not None:
                adj = adj_of(hh)
                s = s + jnp.concatenate([adj, adj], axis=1)
            m_old = m_sc[hh]
            m_new = jnp.maximum(m_old, jnp.max(s, axis=0, keepdims=True))
            m_sc[hh] = m_new
            return jnp.exp(m_old - m_new), jnp.exp(s - m_new).astype(BF16)

        def weighted_values(hh, alpha, p):
            vt1 = jnp.concatenate([vt_ref[0, hl(hh), :], ones], axis=0)
            acc_sc[hh] = alpha * acc_sc[hh] + jnp.dot(vt1, p, preferred_element_type=F32)

        s_of, ap_of = {}, {}
        for t in range(heads + 3):
            if t - 3 >= 0:
                weighted_values(t - 3, *ap_of.pop(t - 3))
            if 0 <= t - 2 < heads:
                ap_of[t - 2] = softmax(t - 2, s_of.pop(t - 2))
            if t < heads:
                s_of[t] = scores(t)

    @pl.when(ty < 0)
    def _():
        update(None)

    for t in range(n_types):
        @pl.when(ty == t)
        def _(t=t):
            update(lambda hh, t=t: bias_ref[hh, t])

    @pl.when(fin_ref[step] == 1)
    def _():
        lam = _lambda(lam_ref, lam_init)
        for hh in range(heads):
            acc = acc_sc[hh]
            both = acc[0:B_DV] / acc[B_DV:B_DV + 1]
            ot = both[:, 0:tq] - lam * both[:, tq:2 * tq]
            on = ot * lax.rsqrt(jnp.mean(ot * ot, axis=0, keepdims=True) + EPS) * ghn_ref[hl(hh), :] * (1.0 - lam_init)
            o_ref[0, :, hl(hh)] = on.T.astype(o_ref.dtype)


def _attn_schedule(T, tq, tk):
    offsets = sorted({qi * tq - kj * tk for qi in range(T // tq) for kj in range(T // tk)
                      if qi * tq + tq - 1 >= kj * tk and qi * tq - kj * tk - (tk - 1) < RPB_MAX_DIST})
    qi_l, kj_l, ty_l, fin_l = [], [], [], []
    for qi in range(T // tq):
        kjs = [kj for kj in range(T // tk) if qi * tq + tq - 1 >= kj * tk]
        for kj in kjs:
            off = qi * tq - kj * tk
            qi_l.append(qi)
            kj_l.append(kj)
            ty_l.append(offsets.index(off) if off in offsets else -1)
            fin_l.append(int(kj == kjs[-1]))
    as_i32 = lambda v: jnp.asarray(np.asarray(v, np.int32))
    return offsets, as_i32(qi_l), as_i32(kj_l), as_i32(ty_l), as_i32(fin_l)


def _bias_table(rpb):
    n = jnp.arange(RPB_MAX_DIST, dtype=jnp.int32)
    max_exact = RPB_BUCKETS // 2
    nf = jnp.maximum(n, 1).astype(F32)
    large = max_exact + (jnp.log(nf / max_exact) / math.log(RPB_MAX_DIST / max_exact)
                         * (RPB_BUCKETS - max_exact)).astype(jnp.int32)
    bucket = jnp.where(n < max_exact, n, jnp.minimum(large, RPB_BUCKETS - 1))
    return (rpb[bucket] - rpb[RPB_BUCKETS - 1][None, :]).T.astype(F32)


def _bias_of_distance(tbl, dist):
    d = np.asarray(dist)
    idx = jnp.asarray(np.clip(d, 0, RPB_MAX_DIST - 1).astype(np.int32))
    vals = jnp.take(tbl, idx, axis=1)
    vals = jnp.where(jnp.asarray(d >= RPB_MAX_DIST), 0.0, vals)
    return jnp.where(jnp.asarray(d < 0), -jnp.inf, vals)


def _toeplitz_kernel(v_ref, o_ref):
    tk, tq = o_ref.shape[1:]
    rows = jnp.broadcast_to(v_ref[0], (tk, v_ref.shape[2]))
    o_ref[0] = pltpu.roll(rows, 0, 1, stride=1, stride_axis=0)[:, :tq]


def _bias_tiles_t(tbl, offsets, tq, tk):
    period = tq + tk
    w = np.arange(period)
    u = np.where(w < tq, w, w - period)
    vext = _bias_of_distance(tbl, np.stack([off + u for off in offsets]))
    H, n_types = vext.shape[:2]
    tiles = pl.pallas_call(
        _toeplitz_kernel,
        grid=(H * n_types,),
        in_specs=[pl.BlockSpec((1, 1, period), lambda i: (i, 0, 0))],
        out_specs=pl.BlockSpec((1, tk, tq), lambda i: (i, 0, 0)),
        out_shape=jax.ShapeDtypeStruct((H * n_types, tk, tq), F32),
        compiler_params=_params("parallel"),
        name="bias_tiles",
    )(vext.reshape(H * n_types, 1, period))
    return tiles.reshape(H, n_types, tk, tq)


def _attn_prompt(q16, k16, vt16, tbl, lam, ghn_col, lam_init, tq, tk):
    B, T, W = q16.shape
    H = W // LANES
    hps = ATTN_HEADS_PER_STEP
    hw = hps * LANES
    assert H % hps == 0
    offsets, qi, kj, ty, fin = _attn_schedule(T, tq, tk)
    bias = _bias_tiles_t(tbl, offsets, tq, tk)
    kern = functools.partial(_attn_kernel, n_types=len(offsets), lam_init=lam_init)
    grid_spec = pltpu.PrefetchScalarGridSpec(
        num_scalar_prefetch=4,
        grid=(H // hps, B, int(qi.shape[0])),
        in_specs=[pl.BlockSpec((1, tq, hw), lambda h, b, s, qi, kj, ty, fin: (b, qi[s], h)),
                  pl.BlockSpec((1, tk, hw), lambda h, b, s, qi, kj, ty, fin: (b, kj[s], h)),
                  pl.BlockSpec((1, hw, tk), lambda h, b, s, qi, kj, ty, fin: (b, h, kj[s])),
                  pl.BlockSpec((hps, len(offsets), tk, tq), lambda h, b, s, *_: (h, 0, 0, 0)),
                  pl.BlockSpec(lam.shape, lambda h, b, s, *_: (0, 0)),
                  pl.BlockSpec((hw, 1), lambda h, b, s, *_: (h, 0))],
        out_specs=pl.BlockSpec((1, tq, hw), lambda h, b, s, qi, kj, ty, fin: (b, qi[s], h)),
        scratch_shapes=[pltpu.VMEM((hps, 2 * tq, LANES), BF16), pltpu.VMEM((hps, 1, 2 * tq), F32),
                        pltpu.VMEM((hps, B_DV + ONES_ROWS, 2 * tq), F32)],
    )
    return pl.pallas_call(
        kern,
        grid_spec=grid_spec,
        out_shape=jax.ShapeDtypeStruct((B, T, W), BF16),
        compiler_params=_params("parallel", "parallel", "arbitrary"),
        name="attn_prompt",
    )(qi, kj, ty, fin, q16, k16, vt16, bias, lam, ghn_col)


PAGES_PER_STEP = 16


def _attn_paged_kernel(pt_ref, q_ref, *refs, heads, t_new, page, n_steps, lam_init):
    pps = PAGES_PER_STEP
    kc_refs, vc_refs = refs[:pps], refs[pps:2 * pps]
    kn_ref, vn_ref, blast_ref, bnew_ref, lam_ref, ghn_ref, o_ref, qm_sc, m_sc, l_sc, acc_sc = refs[2 * pps:]
    j = pl.program_id(1)
    R = SUBLANES
    hsl = lambda h: slice(h * R, (h + 1) * R)
    lsl = lambda h: slice(h * LANES, (h + 1) * LANES)

    @pl.when(j == 0)
    def _():
        q = q_ref[0]
        row = lax.broadcasted_iota(jnp.int32, (R, LANES), 0)
        lane = lax.broadcasted_iota(jnp.int32, (R, LANES), 1)
        keep = (row < t_new) == (lane < B_DK)
        for h in range(heads):
            qm_sc[hsl(h), :] = jnp.where(keep, q[:, lsl(h)], 0.0)
        m_sc[...] = jnp.full_like(m_sc, -jnp.inf)
        l_sc[...] = jnp.zeros_like(l_sc)
        acc_sc[...] = jnp.zeros_like(acc_sc)

    qm = qm_sc[...].astype(BF16)

    def update(s, pv_of):
        m_old = m_sc[...]
        m_new = jnp.maximum(m_old, jnp.max(s, axis=1, keepdims=True))
        alpha = jnp.exp(m_old - m_new)
        p = jnp.exp(s - m_new)
        l_sc[...] = alpha * l_sc[...] + jnp.sum(p, axis=1, keepdims=True)
        pb = p.astype(BF16)
        acc_sc[...] = alpha * acc_sc[...] + jnp.concatenate([pv_of(h, pb[hsl(h)]) for h in range(heads)], axis=0)
        m_sc[...] = m_new

    s = jnp.concatenate(
        [jnp.concatenate([jnp.dot(qm[hsl(h)], kc[0, lsl(h), :].astype(BF16), preferred_element_type=F32)
                          for h in range(heads)], axis=0) for kc in kc_refs], axis=1)
    s = s + jnp.where(j == n_steps - 1, blast_ref[...], 0.0)

    def pv_cached(h, ph):
        parts = [jnp.dot(ph[:, u * page:(u + 1) * page], vc[0, pl.ds(h, page, stride=heads), :].astype(BF16),
                         preferred_element_type=F32) for u, vc in enumerate(vc_refs)]
        return functools.reduce(lambda a, b: a + b, parts)

    update(s, pv_cached)

    @pl.when(j == n_steps - 1)
    def _():
        kn = kn_ref[0].astype(BF16)
        vn = vn_ref[0].astype(BF16)
        s_new = jnp.concatenate([_nt_dot(qm[hsl(h)], kn[:, lsl(h)]) for h in range(heads)], axis=0) + bnew_ref[...]
        update(s_new, lambda h, ph: jnp.dot(ph, vn[:, lsl(h)], preferred_element_type=F32))
        lam = _lambda(lam_ref, lam_init)
        full = acc_sc[...] / l_sc[...]
        for h in range(heads):
            fh = full[hsl(h)]
            o = fh - lam * pltpu.roll(fh, R - t_new, 0)
            on = o * lax.rsqrt(jnp.mean(o * o, axis=1, keepdims=True) + EPS)
            o_ref[0, :, h * LANES:(h + 1) * LANES] = on * ghn_ref[:, h * LANES:(h + 1) * LANES] * (1.0 - lam_init)


def _attn_paged(q, cache_k, cache_v, page_table, k_new, v_new, tbl, lam, ghn, lam_init):
    B, t_new, W = q.shape
    H = W // LANES
    n_pool, page = cache_k.shape[:2]
    n_pages = page_table.shape[1]
    past = n_pages * page
    R = SUBLANES
    pps = PAGES_PER_STEP
    assert 2 * t_new == R and page >= RPB_MAX_DIST and n_pages % pps == 0
    n_steps = n_pages // pps
    pad = lambda a: jnp.concatenate([a, jnp.zeros((B, R - t_new, W), a.dtype)], axis=1)
    q8 = jnp.concatenate([q, q], axis=1)
    t = np.arange(R)[:, None] % t_new
    d_last = past + t - ((n_pages - 1) * page + np.arange(page)[None, :])
    c = np.arange(R)[None, :]
    d_new = np.where(c < t_new, t - c, -1)
    flat = lambda b: b.reshape(H * R, b.shape[-1])
    bias_last = jnp.pad(flat(_bias_of_distance(tbl, d_last)), ((0, 0), ((pps - 1) * page, 0)))
    bias_new = flat(_bias_of_distance(tbl, d_new))
    kern = functools.partial(_attn_paged_kernel, heads=H, t_new=t_new, page=page, n_steps=n_steps, lam_init=lam_init)
    page_spec = lambda rows, width, u: pl.BlockSpec(
        (1, rows, width), lambda b, j, pt: (pt[b * n_pages + j * pps + u], 0, 0))
    per_b = pl.BlockSpec((1, R, W), lambda b, j, pt: (b, 0, 0))
    full = lambda a: pl.BlockSpec(a.shape, lambda b, j, pt: (0,) * a.ndim)
    grid_spec = pltpu.PrefetchScalarGridSpec(
        num_scalar_prefetch=1,
        grid=(B, n_steps),
        in_specs=[per_b] + [page_spec(W, page, u) for u in range(pps)]
                 + [page_spec(page * H, B_DV, u) for u in range(pps)]
                 + [per_b, per_b, full(bias_last), full(bias_new), full(lam), full(ghn)],
        out_specs=per_b,
        scratch_shapes=[pltpu.VMEM((H * R, LANES), F32), pltpu.VMEM((H * R, 1), F32),
                        pltpu.VMEM((H * R, 1), F32), pltpu.VMEM((H * R, B_DV), F32)],
    )
    ck = jnp.transpose(cache_k, (0, 2, 3, 4, 1)).reshape(n_pool, W, page)
    cv = cache_v.reshape(n_pool, page * H, B_DV)
    out = pl.pallas_call(
        kern,
        grid_spec=grid_spec,
        out_shape=jax.ShapeDtypeStruct((B, R, W), F32),
        compiler_params=_params("parallel", "arbitrary"),
        name="attn_paged",
    )(page_table.reshape(-1), q8, *([ck] * pps), *([cv] * pps), pad(k_new), pad(v_new),
      bias_last, bias_new, lam, ghn)
    return out[:, :t_new]


def _trunk(x, mods, mods_kv, state, past, wts):
    B, T, D = x.shape
    N = B * T
    tm = ROW_TILE
    tb = min(MOE_BLOCK, N)
    tr = min(ROUTER_TILE, N)
    mod = lambda m: _Mod(m, T, tm)
    split3 = lambda m: (mod(m[:, :D]), mod(m[:, D:2 * D]), mod(m[:, 2 * D:]))
    rmod = lambda m: _Mod(m, T, tr)
    x2 = x.reshape(N, D)

    heads = A_HEADS
    inner = wts["w_out16"].shape[0]
    dh = inner // heads
    gate = rmod(mods[0][:, 2 * D:])
    if state is None:
        t_rows, x_in = T, x2
    else:
        assert T <= SUBLANES
        t_rows = SUBLANES
        x_in = jnp.concatenate([x, jnp.zeros((B, t_rows - T, D), x.dtype)], axis=1).reshape(B * t_rows, D)
    t_in = min(INPROJ_TILE, B * t_rows)
    proj, gates = _inproj(x_in, _Mod(mods[0][:, :D], t_rows, t_in), _Mod(mods[0][:, D:2 * D], t_rows, t_in),
                          wts["g_norm"][0, 0][None], wts["w_in16"], wts["w_gate"], t_in)
    conv_new = proj.reshape(B, t_rows, 4 * inner)[:, T - (A_CONV - 1):T, :2 * inner]
    proj = proj.reshape(B, t_rows, 4 * inner)
    gates = gates.reshape(B, t_rows, LANES)
    if state is None:
        L, t_valid = math.gcd(T, MLSTM_CHUNK), None
        conv_init = jnp.zeros((B, SUBLANES, 2 * inner), F32)
        c0 = jnp.zeros((B, heads, dh, dh), F32)
        n0 = jnp.zeros((B, heads, 1, dh), F32)
        m0 = jnp.zeros((B, heads, 1, 1), F32)
    else:
        conv_st, c_st, n_st, m_st = state
        L, t_valid = SUBLANES, T
        conv_init = jnp.concatenate([jnp.zeros((B, SUBLANES - (A_CONV - 1), 2 * inner), F32), conv_st], axis=1)
        c0, n0, m0 = c_st, n_st[:, :, None, :], m_st[:, :, None, None]
    gates_t = jnp.swapaxes(gates[:, :, :SUBLANES], 1, 2)
    hs, c1, n1, m1 = _mlstm(proj, gates, gates_t, wts["bg"], wts["bgt"], conv_init, c0, n0, m0,
                            wts["w_conv"], wts["b_conv"], wts["g_hn_a"], L, t_valid)
    hs = hs[:, :T].reshape(N, inner)
    new_state = (conv_new, c1, n1[:, :, 0, :], m1[:, :, 0, 0])

    x2, h16, rg = _proj_router(hs, wts["w_out16"], x2, gate, rmod(mods[1][:, :D]), rmod(mods[1][:, D:2 * D]),
                               wts["g_norm"][0, 1][None], wts["w_router"], wts["b_router"], tr)
    x2 = _moe(h16, rg, x2, _Mod(mods[1][:, 2 * D:], T, tb), wts["wg16"], wts["wu16"], wts["wd16"], 0, tb)

    shift_kv, scale_kv = mod(mods_kv[:, :D]), mod(mods_kv[:, D:])
    shift, scale, gate = split3(mods[2])
    kvq = _kvq(x2, shift_kv, scale_kv, wts["g_kv"], shift, scale, wts["g_norm"][1, 0][None],
               wts["w_kv16"], wts["w_q16"], wts["gmat"], wts["gmat_t"], wts["g_kn"], wts["g_qn"], tm, T)
    v32 = kvq[1]
    W = v32.shape[1]
    H = W // LANES
    lam_init = 0.8 - 0.6 * math.exp(-0.3 * 1)
    if past is None:
        kt32, _, q, k16, vt16 = kvq
        o = _attn_prompt(q.reshape(B, T, W), k16.reshape(B, T, W), vt16, wts["rpb_tbl"], wts["lam"],
                         wts["g_hn_b"].reshape(W, 1), lam_init, math.gcd(T, ATTN_TQ), math.gcd(T, ATTN_TK))
        o = o.reshape(N, W)
        k_out = jnp.transpose(kt32.reshape(B, H, 2, B_DK, T), (0, 4, 1, 2, 3))
    else:
        k32, _, q = kvq
        k_out = k32.reshape(B, T, H, 2, B_DK)
        cache_k, cache_v, page_table = past
        o = _attn_paged(q.reshape(B, T, W), cache_k, cache_v, page_table, k32.reshape(B, T, W),
                        v32.reshape(B, T, W), wts["rpb_tbl"], wts["lam"], wts["g_hn_b"], lam_init)
        o = o.reshape(N, W).astype(BF16)

    x2, h16, rg = _proj_router(o, wts["w_o16"], x2, rmod(mods[2][:, 2 * D:]), rmod(mods[3][:, :D]),
                               rmod(mods[3][:, D:2 * D]), wts["g_norm"][1, 1][None], wts["w_router"],
                               wts["b_router"], tr)
    x2 = _moe(h16, rg, x2, _Mod(mods[3][:, 2 * D:], T, tb), wts["wg16"], wts["wu16"], wts["wd16"], 1, tb)

    return x2.reshape(B, T, D), new_state, k_out, v32.reshape(B, T, H, B_DV)


def kernel(x_prompt, x_sample, c_prompt, c_sample, state_conv, state_C, state_n, state_m, cache_k, cache_v, page_table, w_ada, b_ada, g_norm, w_in_a, b_gate_a, w_conv_a, b_conv_a, g_hn_a, w_out_a, g_kv, w_ada_kv, b_ada_kv, w_kv, g_kn, w_q_b, g_qn_b, lam_b, g_hn_b, w_o_b, rpb, w_router, b_router, w_gate_e, w_up_e, w_down_e):
    Bp, Tp, D = x_prompt.shape
    Bs = x_sample.shape[0]
    inner = w_out_a.shape[1]
    heads_b = g_hn_b.shape[1]
    W = heads_b * B_DV

    n_c = Bp + Bs
    c_all = jnp.concatenate([c_prompt, c_sample, jnp.zeros((-n_c % SUBLANES, D), F32)], axis=0)
    mods = _ada(c_all, w_ada.reshape(-1, D, 3 * D), b_ada.reshape(-1, 1, 3 * D))
    mods_kv = _ada(c_all, w_ada_kv[None], b_ada_kv[None, None])[0]

    n_gate = 2 * A_HEADS
    w_in_t = jnp.swapaxes(w_in_a[0], 0, 1)
    group_of_lane = np.arange(W) // B_DK
    gmat = jnp.asarray((group_of_lane[:, None] == np.arange(LANES)[None, :]).astype(np.float32)).astype(BF16)
    wts = {
        "g_norm": g_norm,
        "w_in16": w_in_t[:4 * inner].astype(BF16),
        "w_gate": jnp.pad(w_in_t[4 * inner:], ((0, LANES - n_gate), (0, 0))),
        "bg": jnp.pad(b_gate_a[0], (0, LANES - n_gate))[None, :],
        "bgt": b_gate_a[0][:, None],
        "w_conv": w_conv_a[0], "b_conv": b_conv_a[0][None, :],
        "g_hn_a": g_hn_a[0].reshape(1, inner),
        "w_out16": w_out_a[0].astype(BF16),
        "g_kv": g_kv[None, :],
        "w_kv16": w_kv.astype(BF16), "w_q16": w_q_b[0].astype(BF16),
        "gmat": gmat, "gmat_t": gmat.T,
        "g_kn": jnp.tile(g_kn.reshape(-1), heads_b)[None, :],
        "g_qn": jnp.tile(g_qn_b[0].reshape(-1), heads_b)[None, :],
        "lam": lam_b[0], "g_hn_b": g_hn_b[0].reshape(1, W),
        "w_o16": w_o_b[0].astype(BF16),
        "rpb_tbl": _bias_table(rpb),
        "w_router": jnp.pad(w_router, ((0, 0), (0, LANES - N_EXPERTS))),
        "b_router": b_router[:, None],
        "wg16": w_gate_e.astype(BF16), "wu16": w_up_e.astype(BF16), "wd16": w_down_e.astype(BF16),
    }

    y_p, st_p, k_p, v_p = _trunk(x_prompt, mods[:, :Bp], mods_kv[:Bp], None, None, wts)
    y_s, st_s, k_s, v_s = _trunk(x_sample, mods[:, Bp:n_c], mods_kv[Bp:n_c],
                                 (state_conv[0], state_C[0], state_n[0], state_m[0]),
                                 (cache_k, cache_v, page_table), wts)
    stack = lambda st: tuple(a[None] for a in st)
    return (y_p, y_s) + stack(st_p) + (k_p, v_p) + stack(st_s) + (k_s, v_s)
```

```python
import functools
import math

import numpy as np
import jax
import jax.numpy as jnp
from jax import lax
from jax.experimental import pallas as pl
from jax.experimental.pallas import tpu as pltpu

F32, BF16 = jnp.float32, jnp.bfloat16
HIGHEST = lax.Precision.HIGHEST
EPS = 1e-6

A_HEADS = 4
A_CONV = 4
B_DK = 64
B_DV = 128
N_EXPERTS = 16
N_GROUPS = 4
RPB_BUCKETS = 32
RPB_MAX_DIST = 128

LANES = 128
SUBLANES = 8
VMEM_LIMIT_BYTES = 56 * 1024 * 1024

ROW_TILE = 512
INPROJ_TILE = 1024
ROUTER_TILE = 1024
MLSTM_CHUNK = 256
ATTN_TQ = 512
ATTN_TK = 512


def _params(*sem):
    return pltpu.CompilerParams(dimension_semantics=sem, vmem_limit_bytes=VMEM_LIMIT_BYTES)


def _nt_dot(a, b):
    return lax.dot_general(a, b, (((1,), (1,)), ((), ())), preferred_element_type=F32)


def _tn_dot(a, b):
    return lax.dot_general(a, b, (((0,), (0,)), ((), ())), preferred_element_type=F32)


def _silu(x):
    return x * jax.nn.sigmoid(x)


def _rms_mod(x, g, scale, shift):
    y = x * lax.rsqrt(jnp.mean(x * x, axis=-1, keepdims=True) + EPS)
    return (y * g) * (1.0 + scale) + shift


def _dot_split(a, b16):
    hi = a.astype(BF16)
    lo = (a - hi.astype(F32)).astype(BF16)
    return (jnp.dot(hi, b16, preferred_element_type=F32) + jnp.dot(lo, b16, preferred_element_type=F32))


def _dot_x3(a, b, dims):
    a_hi, b_hi = a.astype(BF16), b.astype(BF16)
    a_lo = (a - a_hi.astype(F32)).astype(BF16)
    b_lo = (b - b_hi.astype(F32)).astype(BF16)
    dot = lambda x, y: lax.dot_general(x, y, dims, preferred_element_type=F32)
    return dot(a_hi, b_hi) + dot(a_hi, b_lo) + dot(a_lo, b_hi)


def _group_rms(x, gmat, gmat_t, g, group):
    ss = jnp.dot((x * x).astype(BF16), gmat, preferred_element_type=F32)
    r = lax.rsqrt(ss * (1.0 / group) + EPS)
    rf = _dot_split(r, gmat_t)
    return x * rf * g


def _ada_kernel(c_ref, w_ref, b_ref, o_ref):
    a = _silu(c_ref[...])
    o_ref[0] = _dot_x3(a, w_ref[0], (((1,), (0,)), ((), ()))) + b_ref[0]


def _ada(c_all, w, b):
    S, D, Fo = w.shape
    R = c_all.shape[0]
    tn = 1024
    return pl.pallas_call(
        _ada_kernel,
        grid=(S, Fo // tn),
        in_specs=[pl.BlockSpec((R, D), lambda s, j: (0, 0)),
                  pl.BlockSpec((1, D, tn), lambda s, j: (s, 0, j)),
                  pl.BlockSpec((1, 1, tn), lambda s, j: (s, 0, j))],
        out_specs=pl.BlockSpec((1, R, tn), lambda s, j: (s, 0, j)),
        out_shape=jax.ShapeDtypeStruct((S, R, Fo), F32),
        compiler_params=_params("parallel", "parallel"),
        name="ada",
    )(c_all, w, b)


class _Mod:
    def __init__(self, m, T, tm):
        B, D = m.shape
        if T % tm == 0:
            self.arr, self.tiles_per_group = m[:, None, :], T // tm
        else:
            assert (B * T) % tm == 0
            self.arr, self.tiles_per_group = jnp.repeat(m, T, axis=0).reshape(-1, tm, D), 1

    def spec(self, grid_rank):
        R, D = self.arr.shape[1:]
        tpg = self.tiles_per_group
        if grid_rank == 1:
            return pl.BlockSpec((1, R, D), lambda i: (i // tpg, 0, 0))
        return pl.BlockSpec((1, R, D), lambda i, j: (i // tpg, 0, 0))


def _causal_conv(x, tail, w, b):
    L, width = x.shape
    row8 = lax.broadcasted_iota(jnp.int32, (SUBLANES, width), 0)
    acc = b + x * w[A_CONV - 1:A_CONV]
    for s in range(1, A_CONV):
        xs = pltpu.roll(x, s, 0)
        top = jnp.where(row8 < s, pltpu.roll(tail, s, 0), xs[:SUBLANES])
        xs = top if L == SUBLANES else jnp.concatenate([top, xs[SUBLANES:]], axis=0)
        acc = acc + xs * w[A_CONV - 1 - s:A_CONV - s]
    return acc


def _inproj_kernel(x_ref, sh_ref, sc_ref, g_ref, w_ref, wg_ref, o_ref, og_ref, h_sc):
    @pl.when(pl.program_id(1) == 0)
    def _():
        h = _rms_mod(x_ref[...], g_ref[...], sc_ref[0], sh_ref[0])
        h_sc[...] = h.astype(BF16)
        og_ref[...] = _dot_x3(h, wg_ref[...], (((1,), (1,)), ((), ())))

    o_ref[...] = _nt_dot(h_sc[...], w_ref[...])


def _inproj(x2, shift, scale, g, w16_t, wgate_t, tm):
    N, D = x2.shape
    Fo = w16_t.shape[0]
    tn = 2048
    return pl.pallas_call(
        _inproj_kernel,
        grid=(N // tm, Fo // tn),
        in_specs=[pl.BlockSpec((tm, D), lambda i, j: (i, 0)),
                  shift.spec(2), scale.spec(2),
                  pl.BlockSpec((1, D), lambda i, j: (0, 0)),
                  pl.BlockSpec((tn, D), lambda i, j: (j, 0)),
                  pl.BlockSpec((LANES, D), lambda i, j: (0, 0))],
        out_specs=[pl.BlockSpec((tm, tn), lambda i, j: (i, j)),
                   pl.BlockSpec((tm, LANES), lambda i, j: (i, 0))],
        out_shape=[jax.ShapeDtypeStruct((N, Fo), F32), jax.ShapeDtypeStruct((N, LANES), F32)],
        scratch_shapes=[pltpu.VMEM((tm, D), BF16)],
        compiler_params=_params("parallel", "arbitrary"),
        name="mlstm_inproj",
    )(x2, shift.arr, scale.arr, g, w16_t, wgate_t)


def _mlstm_kernel(q_ref, k_ref, v_ref, o_ref, gt_ref, gtt_ref, bg_ref, bgt_ref, cinit_ref,
                  c0_ref, n0_ref, m0_ref, wconv_ref, bconv_ref, ghn_ref,
                  hs_ref, c_ref, n_ref, m_ref, tail_sc, *, L, dh, heads, t_valid):
    inner = heads * dh

    @pl.when(pl.program_id(1) == 0)
    def _():
        c_ref[...] = c0_ref[...]
        n_ref[...] = n0_ref[...]
        m_ref[...] = m0_ref[...]
        tail_sc[...] = cinit_ref[0]

    gt = gt_ref[0] + bg_ref[...]
    gtt = gtt_ref[0] + bgt_ref[...]
    ti = lax.broadcasted_iota(jnp.int32, (L, L), 0)
    si = lax.broadcasted_iota(jnp.int32, (L, L), 1)
    causal = si <= ti
    tcol = lax.broadcasted_iota(jnp.int32, (L, 1), 0)
    trow = lax.broadcasted_iota(jnp.int32, (1, L), 1)

    for h in range(heads):
        sl = slice(h * dh, (h + 1) * dh)
        slk = slice(inner + h * dh, inner + (h + 1) * dh)
        qh = _silu(_causal_conv(q_ref[0, :, sl], tail_sc[:, sl], wconv_ref[:, sl], bconv_ref[:, sl]))
        kh = _silu(_causal_conv(k_ref[0, :, sl], tail_sc[:, slk], wconv_ref[:, slk], bconv_ref[:, slk])) * (dh ** -0.5)
        vb = v_ref[0, :, sl].astype(BF16)

        ig_col = gt[:, h:h + 1]
        lf_col = jax.nn.log_sigmoid(gt[:, heads + h:heads + h + 1])
        ig_row = gtt[h:h + 1, :]
        lf_row = jax.nn.log_sigmoid(gtt[heads + h:heads + h + 1, :])
        if t_valid is not None:
            ig_col = jnp.where(tcol < t_valid, ig_col, -jnp.inf)
            lf_col = jnp.where(tcol < t_valid, lf_col, 0.0)
            ig_row = jnp.where(trow < t_valid, ig_row, -jnp.inf)
            lf_row = jnp.where(trow < t_valid, lf_row, 0.0)

        b_col = jnp.sum(jnp.where(causal, lf_row, 0.0), axis=1, keepdims=True)
        b_row = jnp.sum(jnp.where(ti <= si, lf_col, 0.0), axis=0, keepdims=True)
        dlog = jnp.where(causal, b_col - b_row + ig_row, -jnp.inf)
        g_col = b_col + m_ref[0, h]
        m_col = jnp.maximum(g_col, jnp.max(dlog, axis=1, keepdims=True))
        w_intra = jnp.exp(dlog - m_col)
        w_inter = jnp.exp(g_col - m_col)

        qb = qh.astype(BF16)
        kb = kh.astype(BF16)
        s = w_intra * _nt_dot(qb, kb)
        ch = c_ref[0, h]
        nh = n_ref[0, h]
        num = w_inter * _nt_dot(qb, ch.astype(BF16)) + jnp.dot(s.astype(BF16), vb, preferred_element_type=F32)
        den = w_inter * jnp.sum(qh * nh, axis=1, keepdims=True) + jnp.sum(s, axis=1, keepdims=True)
        hv = num / jnp.maximum(jnp.abs(den), jnp.exp(-m_col))

        m_end = m_col[L - 1:L]
        we_inter = jnp.exp(g_col[L - 1:L] - m_end)
        we_col = jnp.exp(b_col[L - 1:L] - b_col + ig_col - m_end)
        kw = kh * we_col
        c_ref[0, h] = we_inter * ch + _tn_dot(vb, kw.astype(BF16))
        n_ref[0, h] = we_inter * nh + jnp.sum(kw, axis=0, keepdims=True)
        m_ref[0, h] = m_end

        hn = hv * lax.rsqrt(jnp.mean(hv * hv, axis=1, keepdims=True) + EPS) * ghn_ref[:, sl]
        hs_ref[0, :, sl] = (jax.nn.sigmoid(o_ref[0, :, sl]) * hn).astype(hs_ref.dtype)

    tail_sc[:, :inner] = q_ref[0, L - SUBLANES:, :]
    tail_sc[:, inner:] = k_ref[0, L - SUBLANES:, :]


def _mlstm(proj, gates, gates_t, bg, bgt, conv_init, c0, n0, m0, wconv, bconv, ghn, L, t_valid):
    B, Tp, _ = proj.shape
    heads, dh = c0.shape[1], c0.shape[2]
    inner = heads * dh
    nc = Tp // L
    kern = functools.partial(_mlstm_kernel, L=L, dh=dh, heads=heads, t_valid=t_valid)
    col = lambda j: pl.BlockSpec((1, L, inner), lambda b, c: (b, c, j))
    full = lambda shape: pl.BlockSpec(shape, lambda b, c: (0,) * len(shape))
    per_b = lambda shape: pl.BlockSpec((1,) + shape, lambda b, c: (b,) + (0,) * len(shape))
    return pl.pallas_call(
        kern,
        grid=(B, nc),
        in_specs=[col(0), col(1), col(2), col(3),
                  pl.BlockSpec((1, L, LANES), lambda b, c: (b, c, 0)),
                  pl.BlockSpec((1, SUBLANES, L), lambda b, c: (b, 0, c)),
                  full((1, LANES)), full((SUBLANES, 1)),
                  per_b((SUBLANES, 2 * inner)),
                  per_b((heads, dh, dh)), per_b((heads, 1, dh)), per_b((heads, 1, 1)),
                  full((A_CONV, 2 * inner)), full((1, 2 * inner)), full((1, inner))],
        out_specs=[pl.BlockSpec((1, L, inner), lambda b, c: (b, c, 0)),
                   per_b((heads, dh, dh)), per_b((heads, 1, dh)), per_b((heads, 1, 1))],
        out_shape=[jax.ShapeDtypeStruct((B, Tp, inner), BF16),
                   jax.ShapeDtypeStruct((B, heads, dh, dh), F32),
                   jax.ShapeDtypeStruct((B, heads, 1, dh), F32),
                   jax.ShapeDtypeStruct((B, heads, 1, 1), F32)],
        scratch_shapes=[pltpu.VMEM((SUBLANES, 2 * inner), F32)],
        compiler_params=_params("parallel", "arbitrary"),
        name="mlstm",
    )(proj, proj, proj, proj, gates, gates_t, bg, bgt, conv_init, c0, n0, m0, wconv, bconv, ghn)


ROUTE_ROWS = SUBLANES


def _route(h, wr, br):
    tm = h.shape[0]
    per = N_EXPERTS // N_GROUPS
    logits = _dot_x3(h, wr, (((1,), (0,)), ((), ())))
    lt = logits.T[:N_EXPERTS]
    s = jax.nn.sigmoid(lt)
    sel = s + br
    neg = jnp.full((1, tm), -jnp.inf, F32)
    izero = jnp.zeros((1, tm), jnp.int32)

    best_score = best_e1 = best_e2 = best_w1 = best_w2 = None
    for grp in range(N_GROUPS):
        rows = [sel[grp * per + j:grp * per + j + 1] for j in range(per)]
        srow = [s[grp * per + j:grp * per + j + 1] for j in range(per)]
        t1, i1, w1 = rows[0], izero, srow[0]
        for j in range(1, per):
            better = rows[j] > t1
            t1 = jnp.where(better, rows[j], t1)
            i1 = jnp.where(better, j, i1)
            w1 = jnp.where(better, srow[j], w1)
        t2, i2, w2 = neg, izero, srow[0]
        for j in range(per):
            better = jnp.where(i1 == j, neg, rows[j]) > t2
            t2 = jnp.where(better, rows[j], t2)
            i2 = jnp.where(better, j, i2)
            w2 = jnp.where(better, srow[j], w2)
        score = t1 + t2
        e1, e2 = i1 + grp * per, i2 + grp * per
        if grp == 0:
            best_score, best_e1, best_e2, best_w1, best_w2 = score, e1, e2, w1, w2
        else:
            better = score > best_score
            best_score = jnp.where(better, score, best_score)
            best_e1 = jnp.where(better, e1, best_e1)
            best_e2 = jnp.where(better, e2, best_e2)
            best_w1 = jnp.where(better, w1, best_w1)
            best_w2 = jnp.where(better, w2, best_w2)
    tot = best_w1 + best_w2
    zero = jnp.zeros((ROUTE_ROWS - 4, tm), F32)
    return jnp.concatenate([best_e1.astype(F32), best_e2.astype(F32), best_w1 / tot, best_w2 / tot, zero], axis=0)


def _proj_router_kernel(a_ref, w_ref, x_ref, gate_ref, sh_ref, sc_ref, g_ref, wr_ref, br_ref,
                        xo_ref, h_ref, route_ref):
    mix = jnp.dot(a_ref[...], w_ref[...], preferred_element_type=F32)
    x = x_ref[...] + gate_ref[0] * mix
    xo_ref[...] = x
    h = _rms_mod(x, g_ref[...], sc_ref[0], sh_ref[0])
    h_ref[...] = h.astype(BF16)
    route_ref[...] = _route(h, wr_ref[...], br_ref[...])


def _proj_router(a16, w16, x2, gate, shift, scale, g, wr, br, tm):
    N, D = x2.shape
    K = a16.shape[1]
    row = lambda w: pl.BlockSpec((tm, w), lambda i: (i, 0))
    full = lambda shape: pl.BlockSpec(shape, lambda i: (0,) * len(shape))
    return pl.pallas_call(
        _proj_router_kernel,
        grid=(N // tm,),
        in_specs=[row(K), full((K, D)), row(D), gate.spec(1), shift.spec(1), scale.spec(1),
                  full((1, D)), full((D, LANES)), full((N_EXPERTS, 1))],
        out_specs=[row(D), row(D), pl.BlockSpec((ROUTE_ROWS, tm), lambda i: (0, i))],
        out_shape=[jax.ShapeDtypeStruct((N, D), F32), jax.ShapeDtypeStruct((N, D), BF16),
                   jax.ShapeDtypeStruct((ROUTE_ROWS, N), F32)],
        compiler_params=_params("parallel"),
        name="proj_router",
    )(a16, w16, x2, gate.arr, shift.arr, scale.arr, g, wr, br)


MOE_BLOCK = 1024
MOE_WINDOW = 256
MOE_CHUNK = 256
SEG_ALIGN = 16


def _moe_kernel(route_ref, h_ref, x_ref, gate_ref, tri_ref, wg_ref, wu_ref, wd_ref, o_ref,
                xs_sc, ys_sc, gs_sc, tok_sc, seg_sc, *, n_experts, n_sorted):
    e = pl.program_id(1)
    tb = h_ref.shape[0]
    S, CH, RW = n_sorted, MOE_CHUNK, MOE_WINDOW
    one_hot = lambda a, b: jnp.where(a, 1.0, jnp.where(b, 1.0, 0.0))

    @pl.when(e == 0)
    def _dispatch():
        route = route_ref[...]
        e1, e2, w1, w2 = route[0:1], route[1:2], route[2:3], route[3:4]
        eid = lax.broadcasted_iota(jnp.int32, (n_experts, tb), 0).astype(F32)
        hit1, hit2 = eid == e1, eid == e2
        routed = one_hot(hit1, hit2)
        earlier = jnp.dot(routed.astype(BF16), tri_ref[...], preferred_element_type=F32)
        count = jnp.sum(routed, axis=1, keepdims=True)
        padded = jnp.floor((count + (SEG_ALIGN - 1)) * (1.0 / SEG_ALIGN)) * SEG_ALIGN
        below = jnp.where(lax.broadcasted_iota(jnp.int32, (n_experts, n_experts), 1)
                          < lax.broadcasted_iota(jnp.int32, (n_experts, n_experts), 0), 1.0, 0.0)
        start = jnp.dot(below, jnp.broadcast_to(padded, (n_experts, LANES)), precision=HIGHEST,
                        preferred_element_type=F32)[:, :1]
        pos = start + earlier
        pos1 = jnp.sum(jnp.where(hit1, pos, 0.0), axis=0, keepdims=True)
        pos2 = jnp.sum(jnp.where(hit2, pos, 0.0), axis=0, keepdims=True)
        for ex in range(n_experts):
            seg_sc[0, ex] = start[ex, 0].astype(jnp.int32)
            seg_sc[1, ex] = (start[ex, 0] + count[ex, 0]).astype(jnp.int32)
        tok_sc[...] = jnp.concatenate([pos1, pos2, jnp.zeros((LANES - 2, tb), F32)], axis=0).T

        def gather(c, carry):
            r0 = pl.multiple_of(c * CH, CH)
            row = (lax.broadcasted_iota(jnp.int32, (CH, tb), 0) + r0).astype(F32)
            is1, is2 = row == pos1, row == pos2
            xs_sc[pl.ds(r0, CH), :] = jnp.dot(one_hot(is1, is2).astype(BF16), h_ref[...],
                                              preferred_element_type=F32).astype(BF16)
            gs_sc[pl.ds(r0, CH), :] = jnp.sum(jnp.where(is1, w1, jnp.where(is2, w2, 0.0)), axis=1, keepdims=True)
            return carry

        lax.fori_loop(0, S // CH, gather, 0)
        xs_sc[S:, :] = jnp.zeros((RW, xs_sc.shape[1]), BF16)
        gs_sc[S:, :] = jnp.zeros((RW, 1), F32)
        ys_sc[...] = jnp.zeros_like(ys_sc)

    seg_start, seg_end = seg_sc[0, e], seg_sc[1, e]

    def window(w, carry):
        r0 = pl.multiple_of(seg_start + w * RW, SEG_ALIGN)
        rows = xs_sc[pl.ds(r0, RW), :]
        a = jnp.dot(rows, wg_ref[0], preferred_element_type=F32)
        u = jnp.dot(rows, wu_ref[0], preferred_element_type=F32)
        mine = lax.broadcasted_iota(jnp.int32, (RW, 1), 0) + r0 < seg_end
        act = jnp.where(mine, _silu(a) * u * gs_sc[pl.ds(r0, RW), :], 0.0)
        ys_sc[pl.ds(r0, RW), :] += jnp.dot(act.astype(BF16), wd_ref[0], preferred_element_type=F32)
        return carry

    lax.fori_loop(0, (seg_end - seg_start + (RW - 1)) // RW, window, 0)

    @pl.when(e == n_experts - 1)
    def _combine():
        def to_bf16(c, carry):
            r0 = pl.multiple_of(c * CH, CH)
            xs_sc[pl.ds(r0, CH), :] = ys_sc[pl.ds(r0, CH), :].astype(BF16)
            return carry

        lax.fori_loop(0, S // CH, to_bf16, 0)

        def scatter(c, carry):
            t0 = pl.multiple_of(c * CH, CH)
            rec = tok_sc[pl.ds(t0, CH), :]
            col = lax.broadcasted_iota(jnp.int32, (CH, S), 1).astype(F32)
            sel = one_hot(col == rec[:, 0:1], col == rec[:, 1:2]).astype(BF16)
            y = jnp.dot(sel, xs_sc[0:S, :], preferred_element_type=F32)
            g = gate_ref[0] if gate_ref.shape[1] == 1 else gate_ref[0, pl.ds(t0, CH), :]
            o_ref[pl.ds(t0, CH), :] = x_ref[pl.ds(t0, CH), :] + g * y
            return carry

        lax.fori_loop(0, tb // CH, scatter, 0)


def _moe(h16, route, x2, gate, wg16, wu16, wd16, layer, tb):
    N, D = x2.shape
    _, E, _, Fe = wg16.shape
    n_sorted = -(-(2 * tb + E * SEG_ALIGN) // MOE_CHUNK) * MOE_CHUNK
    tri = jnp.asarray(np.triu(np.ones((tb, tb), np.float32), 1), dtype=BF16)
    row = lambda w: pl.BlockSpec((tb, w), lambda i, e: (i, 0))
    kern = functools.partial(_moe_kernel, n_experts=E, n_sorted=n_sorted)
    return pl.pallas_call(
        kern,
        grid=(N // tb, E),
        in_specs=[pl.BlockSpec((ROUTE_ROWS, tb), lambda i, e: (0, i)), row(D), row(D), gate.spec(2),
                  pl.BlockSpec((tb, tb), lambda i, e: (0, 0)),
                  pl.BlockSpec((None, 1, D, Fe), lambda i, e: (layer, e, 0, 0)),
                  pl.BlockSpec((None, 1, D, Fe), lambda i, e: (layer, e, 0, 0)),
                  pl.BlockSpec((None, 1, Fe, D), lambda i, e: (layer, e, 0, 0))],
        out_specs=row(D),
        out_shape=jax.ShapeDtypeStruct((N, D), F32),
        scratch_shapes=[pltpu.VMEM((n_sorted + MOE_WINDOW, D), BF16), pltpu.VMEM((n_sorted + MOE_WINDOW, D), F32),
                        pltpu.VMEM((n_sorted + MOE_WINDOW, 1), F32), pltpu.VMEM((tb, LANES), F32),
                        pltpu.SMEM((2, E), jnp.int32)],
        compiler_params=_params("parallel", "arbitrary"),
        name="moe",
    )(route, h16, x2, gate.arr, tri, wg16, wu16, wd16)


def _kvq_kernel(x_ref, shk_ref, sck_ref, gk_ref, shq_ref, scq_ref, gq_ref, wkv_ref, wq_ref,
                gmat_ref, gmatt_ref, gkn_ref, gqn_ref, k_ref, v32_ref, q_ref, *maybe_attn_refs):
    x = x_ref[...]
    y = x * lax.rsqrt(jnp.mean(x * x, axis=-1, keepdims=True) + EPS)
    hk = ((y * gk_ref[...]) * (1.0 + sck_ref[0]) + shk_ref[0]).astype(BF16)
    hq = ((y * gq_ref[...]) * (1.0 + scq_ref[0]) + shq_ref[0]).astype(BF16)
    W = v32_ref.shape[1]
    kv = jnp.dot(hk, wkv_ref[...], preferred_element_type=F32)
    k = _group_rms(kv[:, :W], gmat_ref[...], gmatt_ref[...], gkn_ref[...], B_DK)
    v = kv[:, W:]
    v32_ref[...] = v
    q = jnp.dot(hq, wq_ref[...], preferred_element_type=F32)
    q = _group_rms(q, gmat_ref[...], gmatt_ref[...], gqn_ref[...], B_DK)
    q_ref[...] = (q * (B_DK ** -0.5)).astype(q_ref.dtype)
    if maybe_attn_refs:
        k16_ref, vt_ref = maybe_attn_refs
        k_ref[0] = k.T
        k16_ref[...] = k.astype(BF16)
        vt_ref[0] = v.T.astype(BF16)
    else:
        k_ref[...] = k


def _kvq(x2, shk, sck, gk, shq, scq, gq, wkv16, wq16, gmat, gmat_t, gkn, gqn, tm, seq_len):
    N, D = x2.shape
    W = wq16.shape[1]
    row = lambda w: pl.BlockSpec((tm, w), lambda i: (i, 0))
    full = lambda shape: pl.BlockSpec(shape, lambda i: (0,) * len(shape))
    if seq_len % tm == 0:
        tps = seq_len // tm
        col = pl.BlockSpec((1, W, tm), lambda i: (i // tps, 0, i % tps))
        out_specs = [col, row(W), row(W), row(W), col]
        out_shape = [jax.ShapeDtypeStruct((N // seq_len, W, seq_len), F32), jax.ShapeDtypeStruct((N, W), F32),
                     jax.ShapeDtypeStruct((N, W), BF16), jax.ShapeDtypeStruct((N, W), BF16),
                     jax.ShapeDtypeStruct((N // seq_len, W, seq_len), BF16)]
    else:
        out_specs = [row(W), row(W), row(W)]
        out_shape = [jax.ShapeDtypeStruct((N, W), F32)] * 3
    return pl.pallas_call(
        _kvq_kernel,
        grid=(N // tm,),
        in_specs=[row(D), shk.spec(1), sck.spec(1), full((1, D)), shq.spec(1), scq.spec(1), full((1, D)),
                  full((D, 2 * W)), full((D, W)), full((W, LANES)), full((LANES, W)),
                  full((1, W)), full((1, W))],
        out_specs=out_specs,
        out_shape=out_shape,
        compiler_params=_params("parallel"),
        name="kvq",
    )(x2, shk.arr, sck.arr, gk, shq.arr, scq.arr, gq, wkv16, wq16, gmat, gmat_t, gkn, gqn)


def _lambda(lam_ref, lam_init):
    lv = lam_ref[...]
    a = jnp.sum(lv[0:1] * lv[1:2], axis=1, keepdims=True)
    b = jnp.sum(lv[2:3] * lv[3:4], axis=1, keepdims=True)
    return jnp.exp(a) - jnp.exp(b) + lam_init


ATTN_HEADS_PER_STEP = 8
ONES_ROWS = SUBLANES


def _attn_kernel(qi_ref, kj_ref, ty_ref, fin_ref, q_ref, k_ref, vt_ref, bias_ref, lam_ref, ghn_ref, o_ref,
                 qm_sc, m_sc, acc_sc, *, n_types, lam_init):
    step = pl.program_id(2)
    ty = ty_ref[step]
    tq = q_ref.shape[1]
    heads = ATTN_HEADS_PER_STEP
    hl = lambda hh: slice(hh * LANES, (hh + 1) * LANES)

    @pl.when(kj_ref[step] == 0)
    def _():
        q = q_ref[0]
        lane = lax.broadcasted_iota(jnp.int32, (tq, LANES), 1)
        zero = jnp.zeros((tq, LANES), BF16)
        for hh in range(heads):
            qh = q[:, hl(hh)]
            qm_sc[hh, 0:tq, :] = jnp.where(lane < B_DK, qh, zero)
            qm_sc[hh, tq:2 * tq, :] = jnp.where(lane >= B_DK, qh, zero)
        m_sc[...] = jnp.full_like(m_sc, -jnp.inf)
        acc_sc[...] = jnp.zeros_like(acc_sc)

    def update(adj_of):
        ones = jnp.ones((ONES_ROWS, k_ref.shape[1]), BF16)

        def scores(hh):
            return _nt_dot(k_ref[0, :, hl(hh)], qm_sc[hh])

        def softmax(hh, s):
            if adj_of is not None:
                adj = adj_of(hh)
                s = s + jnp.concatenate([adj, adj], axis=1)
            m_old = m_sc[hh]
            m_new = jnp.maximum(m_old, jnp.max(s, axis=0, keepdims=True))
            m_sc[hh] = m_new
            return jnp.exp(m_old - m_new), jnp.exp(s - m_new).astype(BF16)

        def weighted_values(hh, alpha, p):
            vt1 = jnp.concatenate([vt_ref[0, hl(hh), :], ones], axis=0)
            acc_sc[hh] = alpha * acc_sc[hh] + jnp.dot(vt1, p, preferred_element_type=F32)

        s_of, ap_of = {}, {}
        for t in range(heads + 3):
            if t - 3 >= 0:
                weighted_values(t - 3, *ap_of.pop(t - 3))
            if 0 <= t - 2 < heads:
                ap_of[t - 2] = softmax(t - 2, s_of.pop(t - 2))
            if t < heads:
                s_of[t] = scores(t)

    @pl.when(ty < 0)
    def _():
        update(None)

    for t in range(n_types):
        @pl.when(ty == t)
        def _(t=t):
            update(lambda hh, t=t: bias_ref[hh, t])

    @pl.when(fin_ref[step] == 1)
    def _():
        lam = _lambda(lam_ref, lam_init)
        for hh in range(heads):
            acc = acc_sc[hh]
            both = acc[0:B_DV] / acc[B_DV:B_DV + 1]
            ot = both[:, 0:tq] - lam * both[:, tq:2 * tq]
            on = ot * lax.rsqrt(jnp.mean(ot * ot, axis=0, keepdims=True) + EPS) * ghn_ref[hl(hh), :] * (1.0 - lam_init)
            o_ref[0, :, hl(hh)] = on.T.astype(o_ref.dtype)


def _attn_schedule(T, tq, tk):
    offsets = sorted({qi * tq - kj * tk for qi in range(T // tq) for kj in range(T // tk)
                      if qi * tq + tq - 1 >= kj * tk and qi * tq - kj * tk - (tk - 1) < RPB_MAX_DIST})
    qi_l, kj_l, ty_l, fin_l = [], [], [], []
    for qi in range(T // tq):
        kjs = [kj for kj in range(T // tk) if qi * tq + tq - 1 >= kj * tk]
        for kj in kjs:
            off = qi * tq - kj * tk
            qi_l.append(qi)
            kj_l.append(kj)
            ty_l.append(offsets.index(off) if off in offsets else -1)
            fin_l.append(int(kj == kjs[-1]))
    as_i32 = lambda v: jnp.asarray(np.asarray(v, np.int32))
    return offsets, as_i32(qi_l), as_i32(kj_l), as_i32(ty_l), as_i32(fin_l)


def _bias_table(rpb):
    n = jnp.arange(RPB_MAX_DIST, dtype=jnp.int32)
    max_exact = RPB_BUCKETS // 2
    nf = jnp.maximum(n, 1).astype(F32)
    large = max_exact + (jnp.log(nf / max_exact) / math.log(RPB_MAX_DIST / max_exact)
                         * (RPB_BUCKETS - max_exact)).astype(jnp.int32)
    bucket = jnp.where(n < max_exact, n, jnp.minimum(large, RPB_BUCKETS - 1))
    return (rpb[bucket] - rpb[RPB_BUCKETS - 1][None, :]).T.astype(F32)


def _bias_of_distance(tbl, dist):
    d = np.asarray(dist)
    idx = jnp.asarray(np.clip(d, 0, RPB_MAX_DIST - 1).astype(np.int32))
    vals = jnp.take(tbl, idx, axis=1)
    vals = jnp.where(jnp.asarray(d >= RPB_MAX_DIST), 0.0, vals)
    return jnp.where(jnp.asarray(d < 0), -jnp.inf, vals)


def _toeplitz_kernel(v_ref, o_ref):
    tk, tq = o_ref.shape[1:]
    rows = jnp.broadcast_to(v_ref[0], (tk, v_ref.shape[2]))
    o_ref[0] = pltpu.roll(rows, 0, 1, stride=1, stride_axis=0)[:, :tq]


def _bias_tiles_t(tbl, offsets, tq, tk):
    period = tq + tk
    w = np.arange(period)
    u = np.where(w < tq, w, w - period)
    vext = _bias_of_distance(tbl, np.stack([off + u for off in offsets]))
    H, n_types = vext.shape[:2]
    tiles = pl.pallas_call(
        _toeplitz_kernel,
        grid=(H * n_types,),
        in_specs=[pl.BlockSpec((1, 1, period), lambda i: (i, 0, 0))],
        out_specs=pl.BlockSpec((1, tk, tq), lambda i: (i, 0, 0)),
        out_shape=jax.ShapeDtypeStruct((H * n_types, tk, tq), F32),
        compiler_params=_params("parallel"),
        name="bias_tiles",
    )(vext.reshape(H * n_types, 1, period))
    return tiles.reshape(H, n_types, tk, tq)


def _attn_prompt(q16, k16, vt16, tbl, lam, ghn_col, lam_init, tq, tk):
    B, T, W = q16.shape
    H = W // LANES
    hps = ATTN_HEADS_PER_STEP
    hw = hps * LANES
    assert H % hps == 0
    offsets, qi, kj, ty, fin = _attn_schedule(T, tq, tk)
    bias = _bias_tiles_t(tbl, offsets, tq, tk)
    kern = functools.partial(_attn_kernel, n_types=len(offsets), lam_init=lam_init)
    grid_spec = pltpu.PrefetchScalarGridSpec(
        num_scalar_prefetch=4,
        grid=(H // hps, B, int(qi.shape[0])),
        in_specs=[pl.BlockSpec((1, tq, hw), lambda h, b, s, qi, kj, ty, fin: (b, qi[s], h)),
                  pl.BlockSpec((1, tk, hw), lambda h, b, s, qi, kj, ty, fin: (b, kj[s], h)),
                  pl.BlockSpec((1, hw, tk), lambda h, b, s, qi, kj, ty, fin: (b, h, kj[s])),
                  pl.BlockSpec((hps, len(offsets), tk, tq), lambda h, b, s, *_: (h, 0, 0, 0)),
                  pl.BlockSpec(lam.shape, lambda h, b, s, *_: (0, 0)),
                  pl.BlockSpec((hw, 1), lambda h, b, s, *_: (h, 0))],
        out_specs=pl.BlockSpec((1, tq, hw), lambda h, b, s, qi, kj, ty, fin: (b, qi[s], h)),
        scratch_shapes=[pltpu.VMEM((hps, 2 * tq, LANES), BF16), pltpu.VMEM((hps, 1, 2 * tq), F32),
                        pltpu.VMEM((hps, B_DV + ONES_ROWS, 2 * tq), F32)],
    )
    return pl.pallas_call(
        kern,
        grid_spec=grid_spec,
        out_shape=jax.ShapeDtypeStruct((B, T, W), BF16),
        compiler_params=_params("parallel", "parallel", "arbitrary"),
        name="attn_prompt",
    )(qi, kj, ty, fin, q16, k16, vt16, bias, lam, ghn_col)


PAGES_PER_STEP = 16


def _attn_paged_kernel(pt_ref, q_ref, *refs, heads, t_new, page, n_steps, lam_init):
    pps = PAGES_PER_STEP
    kc_refs, vc_refs = refs[:pps], refs[pps:2 * pps]
    kn_ref, vn_ref, blast_ref, bnew_ref, lam_ref, ghn_ref, o_ref, qm_sc, m_sc, l_sc, acc_sc = refs[2 * pps:]
    j = pl.program_id(1)
    R = SUBLANES
    hsl = lambda h: slice(h * R, (h + 1) * R)
    lsl = lambda h: slice(h * LANES, (h + 1) * LANES)

    @pl.when(j == 0)
    def _():
        q = q_ref[0]
        row = lax.broadcasted_iota(jnp.int32, (R, LANES), 0)
        lane = lax.broadcasted_iota(jnp.int32, (R, LANES), 1)
        keep = (row < t_new) == (lane < B_DK)
        for h in range(heads):
            qm_sc[hsl(h), :] = jnp.where(keep, q[:, lsl(h)], 0.0)
        m_sc[...] = jnp.full_like(m_sc, -jnp.inf)
        l_sc[...] = jnp.zeros_like(l_sc)
        acc_sc[...] = jnp.zeros_like(acc_sc)

    qm = qm_sc[...].astype(BF16)

    def update(s, pv_of):
        m_old = m_sc[...]
        m_new = jnp.maximum(m_old, jnp.max(s, axis=1, keepdims=True))
        alpha = jnp.exp(m_old - m_new)
        p = jnp.exp(s - m_new)
        l_sc[...] = alpha * l_sc[...] + jnp.sum(p, axis=1, keepdims=True)
        pb = p.astype(BF16)
        acc_sc[...] = alpha * acc_sc[...] + jnp.concatenate([pv_of(h, pb[hsl(h)]) for h in range(heads)], axis=0)
        m_sc[...] = m_new

    s = jnp.concatenate(
        [jnp.concatenate([jnp.dot(qm[hsl(h)], kc[0, lsl(h), :].astype(BF16), preferred_element_type=F32)
                          for h in range(heads)], axis=0) for kc in kc_refs], axis=1)
    s = s + jnp.where(j == n_steps - 1, blast_ref[...], 0.0)

    def pv_cached(h, ph):
        parts = [jnp.dot(ph[:, u * page:(u + 1) * page], vc[0, pl.ds(h, page, stride=heads), :].astype(BF16),
                         preferred_element_type=F32) for u, vc in enumerate(vc_refs)]
        return functools.reduce(lambda a, b: a + b, parts)

    update(s, pv_cached)

    @pl.when(j == n_steps - 1)
    def _():
        kn = kn_ref[0].astype(BF16)
        vn = vn_ref[0].astype(BF16)
        s_new = jnp.concatenate([_nt_dot(qm[hsl(h)], kn[:, lsl(h)]) for h in range(heads)], axis=0) + bnew_ref[...]
        update(s_new, lambda h, ph: jnp.dot(ph, vn[:, lsl(h)], preferred_element_type=F32))
        lam = _lambda(lam_ref, lam_init)
        full = acc_sc[...] / l_sc[...]
        for h in range(heads):
            fh = full[hsl(h)]
            o = fh - lam * pltpu.roll(fh, R - t_new, 0)
            on = o * lax.rsqrt(jnp.mean(o * o, axis=1, keepdims=True) + EPS)
            o_ref[0, :, h * LANES:(h + 1) * LANES] = on * ghn_ref[:, h * LANES:(h + 1) * LANES] * (1.0 - lam_init)


def _attn_paged(q, cache_k, cache_v, page_table, k_new, v_new, tbl, lam, ghn, lam_init):
    B, t_new, W = q.shape
    H = W // LANES
    n_pool, page = cache_k.shape[:2]
    n_pages = page_table.shape[1]
    past = n_pages * page
    R = SUBLANES
    pps = PAGES_PER_STEP
    assert 2 * t_new == R and page >= RPB_MAX_DIST and n_pages % pps == 0
    n_steps = n_pages // pps
    pad = lambda a: jnp.concatenate([a, jnp.zeros((B, R - t_new, W), a.dtype)], axis=1)
    q8 = jnp.concatenate([q, q], axis=1)
    t = np.arange(R)[:, None] % t_new
    d_last = past + t - ((n_pages - 1) * page + np.arange(page)[None, :])
    c = np.arange(R)[None, :]
    d_new = np.where(c < t_new, t - c, -1)
    flat = lambda b: b.reshape(H * R, b.shape[-1])
    bias_last = jnp.pad(flat(_bias_of_distance(tbl, d_last)), ((0, 0), ((pps - 1) * page, 0)))
    bias_new = flat(_bias_of_distance(tbl, d_new))
    kern = functools.partial(_attn_paged_kernel, heads=H, t_new=t_new, page=page, n_steps=n_steps, lam_init=lam_init)
    page_spec = lambda rows, width, u: pl.BlockSpec(
        (1, rows, width), lambda b, j, pt: (pt[b * n_pages + j * pps + u], 0, 0))
    per_b = pl.BlockSpec((1, R, W), lambda b, j, pt: (b, 0, 0))
    full = lambda a: pl.BlockSpec(a.shape, lambda b, j, pt: (0,) * a.ndim)
    grid_spec = pltpu.PrefetchScalarGridSpec(
        num_scalar_prefetch=1,
        grid=(B, n_steps),
        in_specs=[per_b] + [page_spec(W, page, u) for u in range(pps)]
                 + [page_spec(page * H, B_DV, u) for u in range(pps)]
                 + [per_b, per_b, full(bias_last), full(bias_new), full(lam), full(ghn)],
        out_specs=per_b,
        scratch_shapes=[pltpu.VMEM((H * R, LANES), F32), pltpu.VMEM((H * R, 1), F32),
                        pltpu.VMEM((H * R, 1), F32), pltpu.VMEM((H * R, B_DV), F32)],
    )
    ck = jnp.transpose(cache_k, (0, 2, 3, 4, 1)).reshape(n_pool, W, page)
    cv = cache_v.reshape(n_pool, page * H, B_DV)
    out = pl.pallas_call(
        kern,
        grid_spec=grid_spec,
        out_shape=jax.ShapeDtypeStruct((B, R, W), F32),
        compiler_params=_params("parallel", "arbitrary"),
        name="attn_paged",
    )(page_table.reshape(-1), q8, *([ck] * pps), *([cv] * pps), pad(k_new), pad(v_new),
      bias_last, bias_new, lam, ghn)
    return out[:, :t_new]


def _trunk(x, mods, mods_kv, state, past, wts):
    B, T, D = x.shape
    N = B * T
    tm = ROW_TILE
    tb = min(MOE_BLOCK, N)
    tr = min(ROUTER_TILE, N)
    mod = lambda m: _Mod(m, T, tm)
    split3 = lambda m: (mod(m[:, :D]), mod(m[:, D:2 * D]), mod(m[:, 2 * D:]))
    rmod = lambda m: _Mod(m, T, tr)
    x2 = x.reshape(N, D)

    heads = A_HEADS
    inner = wts["w_out16"].shape[0]
    dh = inner // heads
    gate = rmod(mods[0][:, 2 * D:])
    if state is None:
        t_rows, x_in = T, x2
    else:
        assert T <= SUBLANES
        t_rows = SUBLANES
        x_in = jnp.concatenate([x, jnp.zeros((B, t_rows - T, D), x.dtype)], axis=1).reshape(B * t_rows, D)
    t_in = min(INPROJ_TILE, B * t_rows)
    proj, gates = _inproj(x_in, _Mod(mods[0][:, :D], t_rows, t_in), _Mod(mods[0][:, D:2 * D], t_rows, t_in),
                          wts["g_norm"][0, 0][None], wts["w_in16"], wts["w_gate"], t_in)
    conv_new = proj.reshape(B, t_rows, 4 * inner)[:, T - (A_CONV - 1):T, :2 * inner]
    proj = proj.reshape(B, t_rows, 4 * inner)
    gates = gates.reshape(B, t_rows, LANES)
    if state is None:
        L, t_valid = math.gcd(T, MLSTM_CHUNK), None
        conv_init = jnp.zeros((B, SUBLANES, 2 * inner), F32)
        c0 = jnp.zeros((B, heads, dh, dh), F32)
        n0 = jnp.zeros((B, heads, 1, dh), F32)
        m0 = jnp.zeros((B, heads, 1, 1), F32)
    else:
        conv_st, c_st, n_st, m_st = state
        L, t_valid = SUBLANES, T
        conv_init = jnp.concatenate([jnp.zeros((B, SUBLANES - (A_CONV - 1), 2 * inner), F32), conv_st], axis=1)
        c0, n0, m0 = c_st, n_st[:, :, None, :], m_st[:, :, None, None]
    gates_t = jnp.swapaxes(gates[:, :, :SUBLANES], 1, 2)
    hs, c1, n1, m1 = _mlstm(proj, gates, gates_t, wts["bg"], wts["bgt"], conv_init, c0, n0, m0,
                            wts["w_conv"], wts["b_conv"], wts["g_hn_a"], L, t_valid)
    hs = hs[:, :T].reshape(N, inner)
    new_state = (conv_new, c1, n1[:, :, 0, :], m1[:, :, 0, 0])

    x2, h16, rg = _proj_router(hs, wts["w_out16"], x2, gate, rmod(mods[1][:, :D]), rmod(mods[1][:, D:2 * D]),
                               wts["g_norm"][0, 1][None], wts["w_router"], wts["b_router"], tr)
    x2 = _moe(h16, rg, x2, _Mod(mods[1][:, 2 * D:], T, tb), wts["wg16"], wts["wu16"], wts["wd16"], 0, tb)

    shift_kv, scale_kv = mod(mods_kv[:, :D]), mod(mods_kv[:, D:])
    shift, scale, gate = split3(mods[2])
    kvq = _kvq(x2, shift_kv, scale_kv, wts["g_kv"], shift, scale, wts["g_norm"][1, 0][None],
               wts["w_kv16"], wts["w_q16"], wts["gmat"], wts["gmat_t"], wts["g_kn"], wts["g_qn"], tm, T)
    v32 = kvq[1]
    W = v32.shape[1]
    H = W // LANES
    lam_init = 0.8 - 0.6 * math.exp(-0.3 * 1)
    if past is None:
        kt32, _, q, k16, vt16 = kvq
        o = _attn_prompt(q.reshape(B, T, W), k16.reshape(B, T, W), vt16, wts["rpb_tbl"], wts["lam"],
                         wts["g_hn_b"].reshape(W, 1), lam_init, math.gcd(T, ATTN_TQ), math.gcd(T, ATTN_TK))
        o = o.reshape(N, W)
        k_out = jnp.transpose(kt32.reshape(B, H, 2, B_DK, T), (0, 4, 1, 2, 3))
    else:
        k32, _, q = kvq
        k_out = k32.reshape(B, T, H, 2, B_DK)
        cache_k, cache_v, page_table = past
        o = _attn_paged(q.reshape(B, T, W), cache_k, cache_v, page_table, k32.reshape(B, T, W),
                        v32.reshape(B, T, W), wts["rpb_tbl"], wts["lam"], wts["g_hn_b"], lam_init)
        o = o.reshape(N, W).astype(BF16)

    x2, h16, rg = _proj_router(o, wts["w_o16"], x2, rmod(mods[2][:, 2 * D:]), rmod(mods[3][:, :D]),
                               rmod(mods[3][:, D:2 * D]), wts["g_norm"][1, 1][None], wts["w_router"],
                               wts["b_router"], tr)
    x2 = _moe(h16, rg, x2, _Mod(mods[3][:, 2 * D:], T, tb), wts["wg16"], wts["wu16"], wts["wd16"], 1, tb)

    return x2.reshape(B, T, D), new_state, k_out, v32.reshape(B, T, H, B_DV)


def kernel(x_prompt, x_sample, c_prompt, c_sample, state_conv, state_C, state_n, state_m, cache_k, cache_v, page_table, w_ada, b_ada, g_norm, w_in_a, b_gate_a, w_conv_a, b_conv_a, g_hn_a, w_out_a, g_kv, w_ada_kv, b_ada_kv, w_kv, g_kn, w_q_b, g_qn_b, lam_b, g_hn_b, w_o_b, rpb, w_router, b_router, w_gate_e, w_up_e, w_down_e):
    Bp, Tp, D = x_prompt.shape
    Bs = x_sample.shape[0]
    inner = w_out_a.shape[1]
    heads_b = g_hn_b.shape[1]
    W = heads_b * B_DV

    n_c = Bp + Bs
    c_all = jnp.concatenate([c_prompt, c_sample, jnp.zeros((-n_c % SUBLANES, D), F32)], axis=0)
    mods = _ada(c_all, w_ada.reshape(-1, D, 3 * D), b_ada.reshape(-1, 1, 3 * D))
    mods_kv = _ada(c_all, w_ada_kv[None], b_ada_kv[None, None])[0]

    n_gate = 2 * A_HEADS
    w_in_t = jnp.swapaxes(w_in_a[0], 0, 1)
    group_of_lane = np.arange(W) // B_DK
    gmat = jnp.asarray((group_of_lane[:, None] == np.arange(LANES)[None, :]).astype(np.float32)).astype(BF16)
    wts = {
        "g_norm": g_norm,
        "w_in16": w_in_t[:4 * inner].astype(BF16),
        "w_gate": jnp.pad(w_in_t[4 * inner:], ((0, LANES - n_gate), (0, 0))),
        "bg": jnp.pad(b_gate_a[0], (0, LANES - n_gate))[None, :],
        "bgt": b_gate_a[0][:, None],
        "w_conv": w_conv_a[0], "b_conv": b_conv_a[0][None, :],
        "g_hn_a": g_hn_a[0].reshape(1, inner),
        "w_out16": w_out_a[0].astype(BF16),
        "g_kv": g_kv[None, :],
        "w_kv16": w_kv.astype(BF16), "w_q16": w_q_b[0].astype(BF16),
        "gmat": gmat, "gmat_t": gmat.T,
        "g_kn": jnp.tile(g_kn.reshape(-1), heads_b)[None, :],
        "g_qn": jnp.tile(g_qn_b[0].reshape(-1), heads_b)[None, :],
        "lam": lam_b[0], "g_hn_b": g_hn_b[0].reshape(1, W),
        "w_o16": w_o_b[0].astype(BF16),
        "rpb_tbl": _bias_table(rpb),
        "w_router": jnp.pad(w_router, ((0, 0), (0, LANES - N_EXPERTS))),
        "b_router": b_router[:, None],
        "wg16": w_gate_e.astype(BF16), "wu16": w_up_e.astype(BF16), "wd16": w_down_e.astype(BF16),
    }

    y_p, st_p, k_p, v_p = _trunk(x_prompt, mods[:, :Bp], mods_kv[:Bp], None, None, wts)
    y_s, st_s, k_s, v_s = _trunk(x_sample, mods[:, Bp:n_c], mods_kv[Bp:n_c],
                                 (state_conv[0], state_C[0], state_n[0], state_m[0]),
                                 (cache_k, cache_v, page_table), wts)
    stack = lambda st: tuple(a[None] for a in st)
    return (y_p, y_s) + stack(st_p) + (k_p, v_p) + stack(st_s) + (k_s, v_s)
```

```python
import functools
import math

import numpy as np
import jax
import jax.numpy as jnp
from jax import lax
from jax.experimental import pallas as pl
from jax.experimental.pallas import tpu as pltpu

F32, BF16 = jnp.float32, jnp.bfloat16
HIGHEST = lax.Precision.HIGHEST
EPS = 1e-6

A_HEADS = 4
A_CONV = 4
B_DK = 64
B_DV = 128
N_EXPERTS = 16
N_GROUPS = 4
RPB_BUCKETS = 32
RPB_MAX_DIST = 128

LANES = 128
SUBLANES = 8
VMEM_LIMIT_BYTES = 56 * 1024 * 1024

ROW_TILE = 512
INPROJ_TILE = 1024
ROUTER_TILE = 1024
MLSTM_CHUNK = 256
ATTN_TQ = 512
ATTN_TK = 512


def _params(*sem):
    return pltpu.CompilerParams(dimension_semantics=sem, vmem_limit_bytes=VMEM_LIMIT_BYTES)


def _nt_dot(a, b):
    return lax.dot_general(a, b, (((1,), (1,)), ((), ())), preferred_element_type=F32)


def _tn_dot(a, b):
    return lax.dot_general(a, b, (((0,), (0,)), ((), ())), preferred_element_type=F32)


def _silu(x):
    return x * jax.nn.sigmoid(x)


def _rms_mod(x, g, scale, shift):
    y = x * lax.rsqrt(jnp.mean(x * x, axis=-1, keepdims=True) + EPS)
    return (y * g) * (1.0 + scale) + shift


def _dot_split(a, b16):
    hi = a.astype(BF16)
    lo = (a - hi.astype(F32)).astype(BF16)
    return (jnp.dot(hi, b16, preferred_element_type=F32) + jnp.dot(lo, b16, preferred_element_type=F32))


def _dot_x3(a, b, dims):
    a_hi, b_hi = a.astype(BF16), b.astype(BF16)
    a_lo = (a - a_hi.astype(F32)).astype(BF16)
    b_lo = (b - b_hi.astype(F32)).astype(BF16)
    dot = lambda x, y: lax.dot_general(x, y, dims, preferred_element_type=F32)
    return dot(a_hi, b_hi) + dot(a_hi, b_lo) + dot(a_lo, b_hi)


def _group_rms(x, gmat, gmat_t, g, group):
    ss = jnp.dot((x * x).astype(BF16), gmat, preferred_element_type=F32)
    r = lax.rsqrt(ss * (1.0 / group) + EPS)
    rf = _dot_split(r, gmat_t)
    return x * rf * g


def _ada_kernel(c_ref, w_ref, b_ref, o_ref):
    a = _silu(c_ref[...])
    o_ref[0] = _dot_x3(a, w_ref[0], (((1,), (0,)), ((), ()))) + b_ref[0]


def _ada(c_all, w, b):
    S, D, Fo = w.shape
    R = c_all.shape[0]
    tn = 1024
    return pl.pallas_call(
        _ada_kernel,
        grid=(S, Fo // tn),
        in_specs=[pl.BlockSpec((R, D), lambda s, j: (0, 0)),
                  pl.BlockSpec((1, D, tn), lambda s, j: (s, 0, j)),
                  pl.BlockSpec((1, 1, tn), lambda s, j: (s, 0, j))],
        out_specs=pl.BlockSpec((1, R, tn), lambda s, j: (s, 0, j)),
        out_shape=jax.ShapeDtypeStruct((S, R, Fo), F32),
        compiler_params=_params("parallel", "parallel"),
        name="ada",
    )(c_all, w, b)


class _Mod:
    def __init__(self, m, T, tm):
        B, D = m.shape
        if T % tm == 0:
            self.arr, self.tiles_per_group = m[:, None, :], T // tm
        else:
            assert (B * T) % tm == 0
            self.arr, self.tiles_per_group = jnp.repeat(m, T, axis=0).reshape(-1, tm, D), 1

    def spec(self, grid_rank):
        R, D = self.arr.shape[1:]
        tpg = self.tiles_per_group
        if grid_rank == 1:
            return pl.BlockSpec((1, R, D), lambda i: (i // tpg, 0, 0))
        return pl.BlockSpec((1, R, D), lambda i, j: (i // tpg, 0, 0))


def _causal_conv(x, tail, w, b):
    L, width = x.shape
    row8 = lax.broadcasted_iota(jnp.int32, (SUBLANES, width), 0)
    acc = b + x * w[A_CONV - 1:A_CONV]
    for s in range(1, A_CONV):
        xs = pltpu.roll(x, s, 0)
        top = jnp.where(row8 < s, pltpu.roll(tail, s, 0), xs[:SUBLANES])
        xs = top if L == SUBLANES else jnp.concatenate([top, xs[SUBLANES:]], axis=0)
        acc = acc + xs * w[A_CONV - 1 - s:A_CONV - s]
    return acc


def _inproj_kernel(x_ref, sh_ref, sc_ref, g_ref, w_ref, wg_ref, o_ref, og_ref, h_sc):
    @pl.when(pl.program_id(1) == 0)
    def _():
        h = _rms_mod(x_ref[...], g_ref[...], sc_ref[0], sh_ref[0])
        h_sc[...] = h.astype(BF16)
        og_ref[...] = _dot_x3(h, wg_ref[...], (((1,), (1,)), ((), ())))

    o_ref[...] = _nt_dot(h_sc[...], w_ref[...])


def _inproj(x2, shift, scale, g, w16_t, wgate_t, tm):
    N, D = x2.shape
    Fo = w16_t.shape[0]
    tn = 2048
    return pl.pallas_call(
        _inproj_kernel,
        grid=(N // tm, Fo // tn),
        in_specs=[pl.BlockSpec((tm, D), lambda i, j: (i, 0)),
                  shift.spec(2), scale.spec(2),
                  pl.BlockSpec((1, D), lambda i, j: (0, 0)),
                  pl.BlockSpec((tn, D), lambda i, j: (j, 0)),
                  pl.BlockSpec((LANES, D), lambda i, j: (0, 0))],
        out_specs=[pl.BlockSpec((tm, tn), lambda i, j: (i, j)),
                   pl.BlockSpec((tm, LANES), lambda i, j: (i, 0))],
        out_shape=[jax.ShapeDtypeStruct((N, Fo), F32), jax.ShapeDtypeStruct((N, LANES), F32)],
        scratch_shapes=[pltpu.VMEM((tm, D), BF16)],
        compiler_params=_params("parallel", "arbitrary"),
        name="mlstm_inproj",
    )(x2, shift.arr, scale.arr, g, w16_t, wgate_t)


def _mlstm_kernel(q_ref, k_ref, v_ref, o_ref, gt_ref, gtt_ref, bg_ref, bgt_ref, cinit_ref,
                  c0_ref, n0_ref, m0_ref, wconv_ref, bconv_ref, ghn_ref,
                  hs_ref, c_ref, n_ref, m_ref, tail_sc, *, L, dh, heads, t_valid):
    inner = heads * dh

    @pl.when(pl.program_id(1) == 0)
    def _():
        c_ref[...] = c0_ref[...]
        n_ref[...] = n0_ref[...]
        m_ref[...] = m0_ref[...]
        tail_sc[...] = cinit_ref[0]

    gt = gt_ref[0] + bg_ref[...]
    gtt = gtt_ref[0] + bgt_ref[...]
    ti = lax.broadcasted_iota(jnp.int32, (L, L), 0)
    si = lax.broadcasted_iota(jnp.int32, (L, L), 1)
    causal = si <= ti
    tcol = lax.broadcasted_iota(jnp.int32, (L, 1), 0)
    trow = lax.broadcasted_iota(jnp.int32, (1, L), 1)

    for h in range(heads):
        sl = slice(h * dh, (h + 1) * dh)
        slk = slice(inner + h * dh, inner + (h + 1) * dh)
        qh = _silu(_causal_conv(q_ref[0, :, sl], tail_sc[:, sl], wconv_ref[:, sl], bconv_ref[:, sl]))
        kh = _silu(_causal_conv(k_ref[0, :, sl], tail_sc[:, slk], wconv_ref[:, slk], bconv_ref[:, slk])) * (dh ** -0.5)
        vb = v_ref[0, :, sl].astype(BF16)

        ig_col = gt[:, h:h + 1]
        lf_col = jax.nn.log_sigmoid(gt[:, heads + h:heads + h + 1])
        ig_row = gtt[h:h + 1, :]
        lf_row = jax.nn.log_sigmoid(gtt[heads + h:heads + h + 1, :])
        if t_valid is not None:
            ig_col = jnp.where(tcol < t_valid, ig_col, -jnp.inf)
            lf_col = jnp.where(tcol < t_valid, lf_col, 0.0)
            ig_row = jnp.where(trow < t_valid, ig_row, -jnp.inf)
            lf_row = jnp.where(trow < t_valid, lf_row, 0.0)

        b_col = jnp.sum(jnp.where(causal, lf_row, 0.0), axis=1, keepdims=True)
        b_row = jnp.sum(jnp.where(ti <= si, lf_col, 0.0), axis=0, keepdims=True)
        dlog = jnp.where(causal, b_col - b_row + ig_row, -jnp.inf)
        g_col = b_col + m_ref[0, h]
        m_col = jnp.maximum(g_col, jnp.max(dlog, axis=1, keepdims=True))
        w_intra = jnp.exp(dlog - m_col)
        w_inter = jnp.exp(g_col - m_col)

        qb = qh.astype(BF16)
        kb = kh.astype(BF16)
        s = w_intra * _nt_dot(qb, kb)
        ch = c_ref[0, h]
        nh = n_ref[0, h]
        num = w_inter * _nt_dot(qb, ch.astype(BF16)) + jnp.dot(s.astype(BF16), vb, preferred_element_type=F32)
        den = w_inter * jnp.sum(qh * nh, axis=1, keepdims=True) + jnp.sum(s, axis=1, keepdims=True)
        hv = num / jnp.maximum(jnp.abs(den), jnp.exp(-m_col))

        m_end = m_col[L - 1:L]
        we_inter = jnp.exp(g_col[L - 1:L] - m_end)
        we_col = jnp.exp(b_col[L - 1:L] - b_col + ig_col - m_end)
        kw = kh * we_col
        c_ref[0, h] = we_inter * ch + _tn_dot(vb, kw.astype(BF16))
        n_ref[0, h] = we_inter * nh + jnp.sum(kw, axis=0, keepdims=True)
        m_ref[0, h] = m_end

        hn = hv * lax.rsqrt(jnp.mean(hv * hv, axis=1, keepdims=True) + EPS) * ghn_ref[:, sl]
        hs_ref[0, :, sl] = (jax.nn.sigmoid(o_ref[0, :, sl]) * hn).astype(hs_ref.dtype)

    tail_sc[:, :inner] = q_ref[0, L - SUBLANES:, :]
    tail_sc[:, inner:] = k_ref[0, L - SUBLANES:, :]


def _mlstm(proj, gates, gates_t, bg, bgt, conv_init, c0, n0, m0, wconv, bconv, ghn, L, t_valid):
    B, Tp, _ = proj.shape
    heads, dh = c0.shape[1], c0.shape[2]
    inner = heads * dh
    nc = Tp // L
    kern = functools.partial(_mlstm_kernel, L=L, dh=dh, heads=heads, t_valid=t_valid)
    col = lambda j: pl.BlockSpec((1, L, inner), lambda b, c: (b, c, j))
    full = lambda shape: pl.BlockSpec(shape, lambda b, c: (0,) * len(shape))
    per_b = lambda shape: pl.BlockSpec((1,) + shape, lambda b, c: (b,) + (0,) * len(shape))
    return pl.pallas_call(
        kern,
        grid=(B, nc),
        in_specs=[col(0), col(1), col(2), col(3),
                  pl.BlockSpec((1, L, LANES), lambda b, c: (b, c, 0)),
                  pl.BlockSpec((1, SUBLANES, L), lambda b, c: (b, 0, c)),
                  full((1, LANES)), full((SUBLANES, 1)),
                  per_b((SUBLANES, 2 * inner)),
                  per_b((heads, dh, dh)), per_b((heads, 1, dh)), per_b((heads, 1, 1)),
                  full((A_CONV, 2 * inner)), full((1, 2 * inner)), full((1, inner))],
        out_specs=[pl.BlockSpec((1, L, inner), lambda b, c: (b, c, 0)),
                   per_b((heads, dh, dh)), per_b((heads, 1, dh)), per_b((heads, 1, 1))],
        out_shape=[jax.ShapeDtypeStruct((B, Tp, inner), BF16),
                   jax.ShapeDtypeStruct((B, heads, dh, dh), F32),
                   jax.ShapeDtypeStruct((B, heads, 1, dh), F32),
                   jax.ShapeDtypeStruct((B, heads, 1, 1), F32)],
        scratch_shapes=[pltpu.VMEM((SUBLANES, 2 * inner), F32)],
        compiler_params=_params("parallel", "arbitrary"),
        name="mlstm",
    )(proj, proj, proj, proj, gates, gates_t, bg, bgt, conv_init, c0, n0, m0, wconv, bconv, ghn)


ROUTE_ROWS = SUBLANES


def _route(h, wr, br):
    tm = h.shape[0]
    per = N_EXPERTS // N_GROUPS
    logits = _dot_x3(h, wr, (((1,), (0,)), ((), ())))
    lt = logits.T[:N_EXPERTS]
    s = jax.nn.sigmoid(lt)
    sel = s + br
    neg = jnp.full((1, tm), -jnp.inf, F32)
    izero = jnp.zeros((1, tm), jnp.int32)

    best_score = best_e1 = best_e2 = best_w1 = best_w2 = None
    for grp in range(N_GROUPS):
        rows = [sel[grp * per + j:grp * per + j + 1] for j in range(per)]
        srow = [s[grp * per + j:grp * per + j + 1] for j in range(per)]
        t1, i1, w1 = rows[0], izero, srow[0]
        for j in range(1, per):
            better = rows[j] > t1
            t1 = jnp.where(better, rows[j], t1)
            i1 = jnp.where(better, j, i1)
            w1 = jnp.where(better, srow[j], w1)
        t2, i2, w2 = neg, izero, srow[0]
        for j in range(per):
            better = jnp.where(i1 == j, neg, rows[j]) > t2
            t2 = jnp.where(better, rows[j], t2)
            i2 = jnp.where(better, j, i2)
            w2 = jnp.where(better, srow[j], w2)
        score = t1 + t2
        e1, e2 = i1 + grp * per, i2 + grp * per
        if grp == 0:
            best_score, best_e1, best_e2, best_w1, best_w2 = score, e1, e2, w1, w2
        else:
            better = score > best_score
            best_score = jnp.where(better, score, best_score)
            best_e1 = jnp.where(better, e1, best_e1)
            best_e2 = jnp.where(better, e2, best_e2)
            best_w1 = jnp.where(better, w1, best_w1)
            best_w2 = jnp.where(better, w2, best_w2)
    tot = best_w1 + best_w2
    zero = jnp.zeros((ROUTE_ROWS - 4, tm), F32)
    return jnp.concatenate([best_e1.astype(F32), best_e2.astype(F32), best_w1 / tot, best_w2 / tot, zero], axis=0)


def _proj_router_kernel(a_ref, w_ref, x_ref, gate_ref, sh_ref, sc_ref, g_ref, wr_ref, br_ref,
                        xo_ref, h_ref, route_ref):
    mix = jnp.dot(a_ref[...], w_ref[...], preferred_element_type=F32)
    x = x_ref[...] + gate_ref[0] * mix
    xo_ref[...] = x
    h = _rms_mod(x, g_ref[...], sc_ref[0], sh_ref[0])
    h_ref[...] = h.astype(BF16)
    route_ref[...] = _route(h, wr_ref[...], br_ref[...])


def _proj_router(a16, w16, x2, gate, shift, scale, g, wr, br, tm):
    N, D = x2.shape
    K = a16.shape[1]
    row = lambda w: pl.BlockSpec((tm, w), lambda i: (i, 0))
    full = lambda shape: pl.BlockSpec(shape, lambda i: (0,) * len(shape))
    return pl.pallas_call(
        _proj_router_kernel,
        grid=(N // tm,),
        in_specs=[row(K), full((K, D)), row(D), gate.spec(1), shift.spec(1), scale.spec(1),
                  full((1, D)), full((D, LANES)), full((N_EXPERTS, 1))],
        out_specs=[row(D), row(D), pl.BlockSpec((ROUTE_ROWS, tm), lambda i: (0, i))],
        out_shape=[jax.ShapeDtypeStruct((N, D), F32), jax.ShapeDtypeStruct((N, D), BF16),
                   jax.ShapeDtypeStruct((ROUTE_ROWS, N), F32)],
        compiler_params=_params("parallel"),
        name="proj_router",
    )(a16, w16, x2, gate.arr, shift.arr, scale.arr, g, wr, br)


MOE_BLOCK = 1024
MOE_WINDOW = 256
MOE_CHUNK = 256
SEG_ALIGN = 16


def _moe_kernel(route_ref, h_ref, x_ref, gate_ref, tri_ref, wg_ref, wu_ref, wd_ref, o_ref,
                xs_sc, ys_sc, gs_sc, tok_sc, seg_sc, *, n_experts, n_sorted):
    e = pl.program_id(1)
    tb = h_ref.shape[0]
    S, CH, RW = n_sorted, MOE_CHUNK, MOE_WINDOW
    one_hot = lambda a, b: jnp.where(a, 1.0, jnp.where(b, 1.0, 0.0))

    @pl.when(e == 0)
    def _dispatch():
        route = route_ref[...]
        e1, e2, w1, w2 = route[0:1], route[1:2], route[2:3], route[3:4]
        eid = lax.broadcasted_iota(jnp.int32, (n_experts, tb), 0).astype(F32)
        hit1, hit2 = eid == e1, eid == e2
        routed = one_hot(hit1, hit2)
        earlier = jnp.dot(routed.astype(BF16), tri_ref[...], preferred_element_type=F32)
        count = jnp.sum(routed, axis=1, keepdims=True)
        padded = jnp.floor((count + (SEG_ALIGN - 1)) * (1.0 / SEG_ALIGN)) * SEG_ALIGN
        below = jnp.where(lax.broadcasted_iota(jnp.int32, (n_experts, n_experts), 1)
                          < lax.broadcasted_iota(jnp.int32, (n_experts, n_experts), 0), 1.0, 0.0)
        start = jnp.dot(below, jnp.broadcast_to(padded, (n_experts, LANES)), precision=HIGHEST,
                        preferred_element_type=F32)[:, :1]
        pos = start + earlier
        pos1 = jnp.sum(jnp.where(hit1, pos, 0.0), axis=0, keepdims=True)
        pos2 = jnp.sum(jnp.where(hit2, pos, 0.0), axis=0, keepdims=True)
        for ex in range(n_experts):
            seg_sc[0, ex] = start[ex, 0].astype(jnp.int32)
            seg_sc[1, ex] = (start[ex, 0] + count[ex, 0]).astype(jnp.int32)
        tok_sc[...] = jnp.concatenate([pos1, pos2, jnp.zeros((LANES - 2, tb), F32)], axis=0).T

        def gather(c, carry):
            r0 = pl.multiple_of(c * CH, CH)
            row = (lax.broadcasted_iota(jnp.int32, (CH, tb), 0) + r0).astype(F32)
            is1, is2 = row == pos1, row == pos2
            xs_sc[pl.ds(r0, CH), :] = jnp.dot(one_hot(is1, is2).astype(BF16), h_ref[...],
                                              preferred_element_type=F32).astype(BF16)
            gs_sc[pl.ds(r0, CH), :] = jnp.sum(jnp.where(is1, w1, jnp.where(is2, w2, 0.0)), axis=1, keepdims=True)
            return carry

        lax.fori_loop(0, S // CH, gather, 0)
        xs_sc[S:, :] = jnp.zeros((RW, xs_sc.shape[1]), BF16)
        gs_sc[S:, :] = jnp.zeros((RW, 1), F32)
        ys_sc[...] = jnp.zeros_like(ys_sc)

    seg_start, seg_end = seg_sc[0, e], seg_sc[1, e]

    def window(w, carry):
        r0 = pl.multiple_of(seg_start + w * RW, SEG_ALIGN)
        rows = xs_sc[pl.ds(r0, RW), :]
        a = jnp.dot(rows, wg_ref[0], preferred_element_type=F32)
        u = jnp.dot(rows, wu_ref[0], preferred_element_type=F32)
        mine = lax.broadcasted_iota(jnp.int32, (RW, 1), 0) + r0 < seg_end
        act = jnp.where(mine, _silu(a) * u * gs_sc[pl.ds(r0, RW), :], 0.0)
        ys_sc[pl.ds(r0, RW), :] += jnp.dot(act.astype(BF16), wd_ref[0], preferred_element_type=F32)
        return carry

    lax.fori_loop(0, (seg_end - seg_start + (RW - 1)) // RW, window, 0)

    @pl.when(e == n_experts - 1)
    def _combine():
        def to_bf16(c, carry):
            r0 = pl.multiple_of(c * CH, CH)
            xs_sc[pl.ds(r0, CH), :] = ys_sc[pl.ds(r0, CH), :].astype(BF16)
            return carry

        lax.fori_loop(0, S // CH, to_bf16, 0)

        def scatter(c, carry):
            t0 = pl.multiple_of(c * CH, CH)
            rec = tok_sc[pl.ds(t0, CH), :]
            col = lax.broadcasted_iota(jnp.int32, (CH, S), 1).astype(F32)
            sel = one_hot(col == rec[:, 0:1], col == rec[:, 1:2]).astype(BF16)
            y = jnp.dot(sel, xs_sc[0:S, :], preferred_element_type=F32)
            g = gate_ref[0] if gate_ref.shape[1] == 1 else gate_ref[0, pl.ds(t0, CH), :]
            o_ref[pl.ds(t0, CH), :] = x_ref[pl.ds(t0, CH), :] + g * y
            return carry

        lax.fori_loop(0, tb // CH, scatter, 0)


def _moe(h16, route, x2, gate, wg16, wu16, wd16, layer, tb):
    N, D = x2.shape
    _, E, _, Fe = wg16.shape
    n_sorted = -(-(2 * tb + E * SEG_ALIGN) // MOE_CHUNK) * MOE_CHUNK
    tri = jnp.asarray(np.triu(np.ones((tb, tb), np.float32), 1), dtype=BF16)
    row = lambda w: pl.BlockSpec((tb, w), lambda i, e: (i, 0))
    kern = functools.partial(_moe_kernel, n_experts=E, n_sorted=n_sorted)
    return pl.pallas_call(
        kern,
        grid=(N // tb, E),
        in_specs=[pl.BlockSpec((ROUTE_ROWS, tb), lambda i, e: (0, i)), row(D), row(D), gate.spec(2),
                  pl.BlockSpec((tb, tb), lambda i, e: (0, 0)),
                  pl.BlockSpec((None, 1, D, Fe), lambda i, e: (layer, e, 0, 0)),
                  pl.BlockSpec((None, 1, D, Fe), lambda i, e: (layer, e, 0, 0)),
                  pl.BlockSpec((None, 1, Fe, D), lambda i, e: (layer, e, 0, 0))],
        out_specs=row(D),
        out_shape=jax.ShapeDtypeStruct((N, D), F32),
        scratch_shapes=[pltpu.VMEM((n_sorted + MOE_WINDOW, D), BF16), pltpu.VMEM((n_sorted + MOE_WINDOW, D), F32),
                        pltpu.VMEM((n_sorted + MOE_WINDOW, 1), F32), pltpu.VMEM((tb, LANES), F32),
                        pltpu.SMEM((2, E), jnp.int32)],
        compiler_params=_params("parallel", "arbitrary"),
        name="moe",
    )(route, h16, x2, gate.arr, tri, wg16, wu16, wd16)


def _kvq_kernel(x_ref, shk_ref, sck_ref, gk_ref, shq_ref, scq_ref, gq_ref, wkv_ref, wq_ref,
                gmat_ref, gmatt_ref, gkn_ref, gqn_ref, k_ref, v32_ref, q_ref, *maybe_attn_refs):
    x = x_ref[...]
    y = x * lax.rsqrt(jnp.mean(x * x, axis=-1, keepdims=True) + EPS)
    hk = ((y * gk_ref[...]) * (1.0 + sck_ref[0]) + shk_ref[0]).astype(BF16)
    hq = ((y * gq_ref[...]) * (1.0 + scq_ref[0]) + shq_ref[0]).astype(BF16)
    W = v32_ref.shape[1]
    kv = jnp.dot(hk, wkv_ref[...], preferred_element_type=F32)
    k = _group_rms(kv[:, :W], gmat_ref[...], gmatt_ref[...], gkn_ref[...], B_DK)
    v = kv[:, W:]
    v32_ref[...] = v
    q = jnp.dot(hq, wq_ref[...], preferred_element_type=F32)
    q = _group_rms(q, gmat_ref[...], gmatt_ref[...], gqn_ref[...], B_DK)
    q_ref[...] = (q * (B_DK ** -0.5)).astype(q_ref.dtype)
    if maybe_attn_refs:
        k16_ref, vt_ref = maybe_attn_refs
        k_ref[0] = k.T
        k16_ref[...] = k.astype(BF16)
        vt_ref[0] = v.T.astype(BF16)
    else:
        k_ref[...] = k


def _kvq(x2, shk, sck, gk, shq, scq, gq, wkv16, wq16, gmat, gmat_t, gkn, gqn, tm, seq_len):
    N, D = x2.shape
    W = wq16.shape[1]
    row = lambda w: pl.BlockSpec((tm, w), lambda i: (i, 0))
    full = lambda shape: pl.BlockSpec(shape, lambda i: (0,) * len(shape))
    if seq_len % tm == 0:
        tps = seq_len // tm
        col = pl.BlockSpec((1, W, tm), lambda i: (i // tps, 0, i % tps))
        out_specs = [col, row(W), row(W), row(W), col]
        out_shape = [jax.ShapeDtypeStruct((N // seq_len, W, seq_len), F32), jax.ShapeDtypeStruct((N, W), F32),
                     jax.ShapeDtypeStruct((N, W), BF16), jax.ShapeDtypeStruct((N, W), BF16),
                     jax.ShapeDtypeStruct((N // seq_len, W, seq_len), BF16)]
    else:
        out_specs = [row(W), row(W), row(W)]
        out_shape = [jax.ShapeDtypeStruct((N, W), F32)] * 3
    return pl.pallas_call(
        _kvq_kernel,
        grid=(N // tm,),
        in_specs=[row(D), shk.spec(1), sck.spec(1), full((1, D)), shq.spec(1), scq.spec(1), full((1, D)),
                  full((D, 2 * W)), full((D, W)), full((W, LANES)), full((LANES, W)),
                  full((1, W)), full((1, W))],
        out_specs=out_specs,
        out_shape=out_shape,
        compiler_params=_params("parallel"),
        name="kvq",
    )(x2, shk.arr, sck.arr, gk, shq.arr, scq.arr, gq, wkv16, wq16, gmat, gmat_t, gkn, gqn)


def _lambda(lam_ref, lam_init):
    lv = lam_ref[...]
    a = jnp.sum(lv[0:1] * lv[1:2], axis=1, keepdims=True)
    b = jnp.sum(lv[2:3] * lv[3:4], axis=1, keepdims=True)
    return jnp.exp(a) - jnp.exp(b) + lam_init


ATTN_HEADS_PER_STEP = 8
ONES_ROWS = SUBLANES


def _attn_kernel(qi_ref, kj_ref, ty_ref, fin_ref, q_ref, k_ref, vt_ref, bias_ref, lam_ref, ghn_ref, o_ref,
                 qm_sc, m_sc, acc_sc, *, n_types, lam_init):
    step = pl.program_id(2)
    ty = ty_ref[step]
    tq = q_ref.shape[1]
    heads = ATTN_HEADS_PER_STEP
    hl = lambda hh: slice(hh * LANES, (hh + 1) * LANES)

    @pl.when(kj_ref[step] == 0)
    def _():
        q = q_ref[0]
        lane = lax.broadcasted_iota(jnp.int32, (tq, LANES), 1)
        zero = jnp.zeros((tq, LANES), BF16)
        for hh in range(heads):
            qh = q[:, hl(hh)]
            qm_sc[hh, 0:tq, :] = jnp.where(lane < B_DK, qh, zero)
            qm_sc[hh, tq:2 * tq, :] = jnp.where(lane >= B_DK, qh, zero)
        m_sc[...] = jnp.full_like(m_sc, -jnp.inf)
        acc_sc[...] = jnp.zeros_like(acc_sc)

    def update(adj_of):
        ones = jnp.ones((ONES_ROWS, k_ref.shape[1]), BF16)

        def scores(hh):
            return _nt_dot(k_ref[0, :, hl(hh)], qm_sc[hh])

        def softmax(hh, s):
            if adj_of is not None:
                s = jnp.concatenate([s[:, :tq] + adj_of(hh), s[:, tq:] + adj_of(hh)], axis=1)
            m_old = m_sc[hh]
            m_new = jnp.maximum(m_old, jnp.max(s, axis=0, keepdims=True))
            m_sc[hh] = m_new
            return jnp.exp(m_old - m_new), jnp.exp(s - m_new).astype(BF16)

        def weighted_values(hh, alpha, p):
            vt1 = jnp.concatenate([vt_ref[0, hl(hh), :], ones], axis=0)
            acc_sc[hh] = alpha * acc_sc[hh] + jnp.dot(vt1, p, preferred_element_type=F32)

        s_of, ap_of = {}, {}
        for t in range(heads + 3):
            if t - 3 >= 0:
                weighted_values(t - 3, *ap_of.pop(t - 3))
            if 0 <= t - 2 < heads:
                ap_of[t - 2] = softmax(t - 2, s_of.pop(t - 2))
            if t < heads:
                s_of[t] = scores(t)

    @pl.when(ty < 0)
    def _():
        update(None)

    for t in range(n_types):
        @pl.when(ty == t)
        def _(t=t):
            update(lambda hh, t=t: bias_ref[hh, t])

    @pl.when(fin_ref[step] == 1)
    def _():
        lam = _lambda(lam_ref, lam_init)
        for hh in range(heads):
            acc = acc_sc[hh]
            both = acc[0:B_DV] / acc[B_DV:B_DV + 1]
            ot = both[:, 0:tq] - lam * both[:, tq:2 * tq]
            on = ot * lax.rsqrt(jnp.mean(ot * ot, axis=0, keepdims=True) + EPS) * ghn_ref[hl(hh), :] * (1.0 - lam_init)
            o_ref[0, :, hl(hh)] = on.T.astype(o_ref.dtype)


def _attn_schedule(T, tq, tk):
    offsets = sorted({qi * tq - kj * tk for qi in range(T // tq) for kj in range(T // tk)
                      if qi * tq + tq - 1 >= kj * tk and qi * tq - kj * tk - (tk - 1) < RPB_MAX_DIST})
    qi_l, kj_l, ty_l, fin_l = [], [], [], []
    for qi in range(T // tq):
        kjs = [kj for kj in range(T // tk) if qi * tq + tq - 1 >= kj * tk]
        for kj in kjs:
            off = qi * tq - kj * tk
            qi_l.append(qi)
            kj_l.append(kj)
            ty_l.append(offsets.index(off) if off in offsets else -1)
            fin_l.append(int(kj == kjs[-1]))
    as_i32 = lambda v: jnp.asarray(np.asarray(v, np.int32))
    return offsets, as_i32(qi_l), as_i32(kj_l), as_i32(ty_l), as_i32(fin_l)


def _bias_table(rpb):
    n = jnp.arange(RPB_MAX_DIST, dtype=jnp.int32)
    max_exact = RPB_BUCKETS // 2
    nf = jnp.maximum(n, 1).astype(F32)
    large = max_exact + (jnp.log(nf / max_exact) / math.log(RPB_MAX_DIST / max_exact)
                         * (RPB_BUCKETS - max_exact)).astype(jnp.int32)
    bucket = jnp.where(n < max_exact, n, jnp.minimum(large, RPB_BUCKETS - 1))
    return (rpb[bucket] - rpb[RPB_BUCKETS - 1][None, :]).T.astype(F32)


def _bias_of_distance(tbl, dist):
    d = np.asarray(dist)
    idx = jnp.asarray(np.clip(d, 0, RPB_MAX_DIST - 1).astype(np.int32))
    vals = jnp.take(tbl, idx, axis=1)
    vals = jnp.where(jnp.asarray(d >= RPB_MAX_DIST), 0.0, vals)
    return jnp.where(jnp.asarray(d < 0), -jnp.inf, vals)


def _toeplitz_kernel(v_ref, o_ref):
    tk, tq = o_ref.shape[1:]
    rows = jnp.broadcast_to(v_ref[0], (tk, v_ref.shape[2]))
    o_ref[0] = pltpu.roll(rows, 0, 1, stride=1, stride_axis=0)[:, :tq]


def _bias_tiles_t(tbl, offsets, tq, tk):
    period = tq + tk
    w = np.arange(period)
    u = np.where(w < tq, w, w - period)
    vext = _bias_of_distance(tbl, np.stack([off + u for off in offsets]))
    H, n_types = vext.shape[:2]
    tiles = pl.pallas_call(
        _toeplitz_kernel,
        grid=(H * n_types,),
        in_specs=[pl.BlockSpec((1, 1, period), lambda i: (i, 0, 0))],
        out_specs=pl.BlockSpec((1, tk, tq), lambda i: (i, 0, 0)),
        out_shape=jax.ShapeDtypeStruct((H * n_types, tk, tq), F32),
        compiler_params=_params("parallel"),
        name="bias_tiles",
    )(vext.reshape(H * n_types, 1, period))
    return tiles.reshape(H, n_types, tk, tq)


def _attn_prompt(q16, k16, vt16, tbl, lam, ghn_col, lam_init, tq, tk):
    B, T, W = q16.shape
    H = W // LANES
    hps = ATTN_HEADS_PER_STEP
    hw = hps * LANES
    assert H % hps == 0
    offsets, qi, kj, ty, fin = _attn_schedule(T, tq, tk)
    bias = _bias_tiles_t(tbl, offsets, tq, tk)
    kern = functools.partial(_attn_kernel, n_types=len(offsets), lam_init=lam_init)
    grid_spec = pltpu.PrefetchScalarGridSpec(
        num_scalar_prefetch=4,
        grid=(H // hps, B, int(qi.shape[0])),
        in_specs=[pl.BlockSpec((1, tq, hw), lambda h, b, s, qi, kj, ty, fin: (b, qi[s], h)),
                  pl.BlockSpec((1, tk, hw), lambda h, b, s, qi, kj, ty, fin: (b, kj[s], h)),
                  pl.BlockSpec((1, hw, tk), lambda h, b, s, qi, kj, ty, fin: (b, h, kj[s])),
                  pl.BlockSpec((hps, len(offsets), tk, tq), lambda h, b, s, *_: (h, 0, 0, 0)),
                  pl.BlockSpec(lam.shape, lambda h, b, s, *_: (0, 0)),
                  pl.BlockSpec((hw, 1), lambda h, b, s, *_: (h, 0))],
        out_specs=pl.BlockSpec((1, tq, hw), lambda h, b, s, qi, kj, ty, fin: (b, qi[s], h)),
        scratch_shapes=[pltpu.VMEM((hps, 2 * tq, LANES), BF16), pltpu.VMEM((hps, 1, 2 * tq), F32),
                        pltpu.VMEM((hps, B_DV + ONES_ROWS, 2 * tq), F32)],
    )
    return pl.pallas_call(
        kern,
        grid_spec=grid_spec,
        out_shape=jax.ShapeDtypeStruct((B, T, W), BF16),
        compiler_params=_params("parallel", "parallel", "arbitrary"),
        name="attn_prompt",
    )(qi, kj, ty, fin, q16, k16, vt16, bias, lam, ghn_col)


PAGES_PER_STEP = 16


def _attn_paged_kernel(pt_ref, q_ref, *refs, heads, t_new, page, n_steps, lam_init):
    pps = PAGES_PER_STEP
    kc_refs, vc_refs = refs[:pps], refs[pps:2 * pps]
    kn_ref, vn_ref, blast_ref, bnew_ref, lam_ref, ghn_ref, o_ref, qm_sc, m_sc, l_sc, acc_sc = refs[2 * pps:]
    j = pl.program_id(1)
    R = SUBLANES
    hsl = lambda h: slice(h * R, (h + 1) * R)
    lsl = lambda h: slice(h * LANES, (h + 1) * LANES)

    @pl.when(j == 0)
    def _():
        q = q_ref[0]
        row = lax.broadcasted_iota(jnp.int32, (R, LANES), 0)
        lane = lax.broadcasted_iota(jnp.int32, (R, LANES), 1)
        keep = (row < t_new) == (lane < B_DK)
        for h in range(heads):
            qm_sc[hsl(h), :] = jnp.where(keep, q[:, lsl(h)], 0.0)
        m_sc[...] = jnp.full_like(m_sc, -jnp.inf)
        l_sc[...] = jnp.zeros_like(l_sc)
        acc_sc[...] = jnp.zeros_like(acc_sc)

    qm = qm_sc[...].astype(BF16)

    def update(s, pv_of):
        m_old = m_sc[...]
        m_new = jnp.maximum(m_old, jnp.max(s, axis=1, keepdims=True))
        alpha = jnp.exp(m_old - m_new)
        p = jnp.exp(s - m_new)
        l_sc[...] = alpha * l_sc[...] + jnp.sum(p, axis=1, keepdims=True)
        pb = p.astype(BF16)
        acc_sc[...] = alpha * acc_sc[...] + jnp.concatenate([pv_of(h, pb[hsl(h)]) for h in range(heads)], axis=0)
        m_sc[...] = m_new

    s = jnp.concatenate(
        [jnp.concatenate([jnp.dot(qm[hsl(h)], kc[0, lsl(h), :].astype(BF16), preferred_element_type=F32)
                          for h in range(heads)], axis=0) for kc in kc_refs], axis=1)
    s = s + jnp.where(j == n_steps - 1, blast_ref[...], 0.0)

    def pv_cached(h, ph):
        parts = [jnp.dot(ph[:, u * page:(u + 1) * page], vc[0, pl.ds(h, page, stride=heads), :].astype(BF16),
                         preferred_element_type=F32) for u, vc in enumerate(vc_refs)]
        return functools.reduce(lambda a, b: a + b, parts)

    update(s, pv_cached)

    @pl.when(j == n_steps - 1)
    def _():
        kn = kn_ref[0].astype(BF16)
        vn = vn_ref[0].astype(BF16)
        s_new = jnp.concatenate([_nt_dot(qm[hsl(h)], kn[:, lsl(h)]) for h in range(heads)], axis=0) + bnew_ref[...]
        update(s_new, lambda h, ph: jnp.dot(ph, vn[:, lsl(h)], preferred_element_type=F32))
        lam = _lambda(lam_ref, lam_init)
        full = acc_sc[...] / l_sc[...]
        for h in range(heads):
            fh = full[hsl(h)]
            o = fh - lam * pltpu.roll(fh, R - t_new, 0)
            on = o * lax.rsqrt(jnp.mean(o * o, axis=1, keepdims=True) + EPS)
            o_ref[0, :, h * LANES:(h + 1) * LANES] = on * ghn_ref[:, h * LANES:(h + 1) * LANES] * (1.0 - lam_init)


def _attn_paged(q, cache_k, cache_v, page_table, k_new, v_new, tbl, lam, ghn, lam_init):
    B, t_new, W = q.shape
    H = W // LANES
    n_pool, page = cache_k.shape[:2]
    n_pages = page_table.shape[1]
    past = n_pages * page
    R = SUBLANES
    pps = PAGES_PER_STEP
    assert 2 * t_new == R and page >= RPB_MAX_DIST and n_pages % pps == 0
    n_steps = n_pages // pps
    pad = lambda a: jnp.concatenate([a, jnp.zeros((B, R - t_new, W), a.dtype)], axis=1)
    q8 = jnp.concatenate([q, q], axis=1)
    t = np.arange(R)[:, None] % t_new
    d_last = past + t - ((n_pages - 1) * page + np.arange(page)[None, :])
    c = np.arange(R)[None, :]
    d_new = np.where(c < t_new, t - c, -1)
    flat = lambda b: b.reshape(H * R, b.shape[-1])
    bias_last = jnp.pad(flat(_bias_of_distance(tbl, d_last)), ((0, 0), ((pps - 1) * page, 0)))
    bias_new = flat(_bias_of_distance(tbl, d_new))
    kern = functools.partial(_attn_paged_kernel, heads=H, t_new=t_new, page=page, n_steps=n_steps, lam_init=lam_init)
    page_spec = lambda rows, width, u: pl.BlockSpec(
        (1, rows, width), lambda b, j, pt: (pt[b * n_pages + j * pps + u], 0, 0))
    per_b = pl.BlockSpec((1, R, W), lambda b, j, pt: (b, 0, 0))
    full = lambda a: pl.BlockSpec(a.shape, lambda b, j, pt: (0,) * a.ndim)
    grid_spec = pltpu.PrefetchScalarGridSpec(
        num_scalar_prefetch=1,
        grid=(B, n_steps),
        in_specs=[per_b] + [page_spec(W, page, u) for u in range(pps)]
                 + [page_spec(page * H, B_DV, u) for u in range(pps)]
                 + [per_b, per_b, full(bias_last), full(bias_new), full(lam), full(ghn)],
        out_specs=per_b,
        scratch_shapes=[pltpu.VMEM((H * R, LANES), F32), pltpu.VMEM((H * R, 1), F32),
                        pltpu.VMEM((H * R, 1), F32), pltpu.VMEM((H * R, B_DV), F32)],
    )
    ck = jnp.transpose(cache_k, (0, 2, 3, 4, 1)).reshape(n_pool, W, page)
    cv = cache_v.reshape(n_pool, page * H, B_DV)
    out = pl.pallas_call(
        kern,
        grid_spec=grid_spec,
        out_shape=jax.ShapeDtypeStruct((B, R, W), F32),
        compiler_params=_params("parallel", "arbitrary"),
        name="attn_paged",
    )(page_table.reshape(-1), q8, *([ck] * pps), *([cv] * pps), pad(k_new), pad(v_new),
      bias_last, bias_new, lam, ghn)
    return out[:, :t_new]


def _trunk(x, mods, mods_kv, state, past, wts):
    B, T, D = x.shape
    N = B * T
    tm = ROW_TILE
    tb = min(MOE_BLOCK, N)
    tr = min(ROUTER_TILE, N)
    mod = lambda m: _Mod(m, T, tm)
    split3 = lambda m: (mod(m[:, :D]), mod(m[:, D:2 * D]), mod(m[:, 2 * D:]))
    rmod = lambda m: _Mod(m, T, tr)
    x2 = x.reshape(N, D)

    heads = A_HEADS
    inner = wts["w_out16"].shape[0]
    dh = inner // heads
    gate = rmod(mods[0][:, 2 * D:])
    if state is None:
        t_rows, x_in = T, x2
    else:
        assert T <= SUBLANES
        t_rows = SUBLANES
        x_in = jnp.concatenate([x, jnp.zeros((B, t_rows - T, D), x.dtype)], axis=1).reshape(B * t_rows, D)
    t_in = min(INPROJ_TILE, B * t_rows)
    proj, gates = _inproj(x_in, _Mod(mods[0][:, :D], t_rows, t_in), _Mod(mods[0][:, D:2 * D], t_rows, t_in),
                          wts["g_norm"][0, 0][None], wts["w_in16"], wts["w_gate"], t_in)
    conv_new = proj.reshape(B, t_rows, 4 * inner)[:, T - (A_CONV - 1):T, :2 * inner]
    proj = proj.reshape(B, t_rows, 4 * inner)
    gates = gates.reshape(B, t_rows, LANES)
    if state is None:
        L, t_valid = math.gcd(T, MLSTM_CHUNK), None
        conv_init = jnp.zeros((B, SUBLANES, 2 * inner), F32)
        c0 = jnp.zeros((B, heads, dh, dh), F32)
        n0 = jnp.zeros((B, heads, 1, dh), F32)
        m0 = jnp.zeros((B, heads, 1, 1), F32)
    else:
        conv_st, c_st, n_st, m_st = state
        L, t_valid = SUBLANES, T
        conv_init = jnp.concatenate([jnp.zeros((B, SUBLANES - (A_CONV - 1), 2 * inner), F32), conv_st], axis=1)
        c0, n0, m0 = c_st, n_st[:, :, None, :], m_st[:, :, None, None]
    gates_t = jnp.swapaxes(gates[:, :, :SUBLANES], 1, 2)
    hs, c1, n1, m1 = _mlstm(proj, gates, gates_t, wts["bg"], wts["bgt"], conv_init, c0, n0, m0,
                            wts["w_conv"], wts["b_conv"], wts["g_hn_a"], L, t_valid)
    hs = hs[:, :T].reshape(N, inner)
    new_state = (conv_new, c1, n1[:, :, 0, :], m1[:, :, 0, 0])

    x2, h16, rg = _proj_router(hs, wts["w_out16"], x2, gate, rmod(mods[1][:, :D]), rmod(mods[1][:, D:2 * D]),
                               wts["g_norm"][0, 1][None], wts["w_router"], wts["b_router"], tr)
    x2 = _moe(h16, rg, x2, _Mod(mods[1][:, 2 * D:], T, tb), wts["wg16"], wts["wu16"], wts["wd16"], 0, tb)

    shift_kv, scale_kv = mod(mods_kv[:, :D]), mod(mods_kv[:, D:])
    shift, scale, gate = split3(mods[2])
    kvq = _kvq(x2, shift_kv, scale_kv, wts["g_kv"], shift, scale, wts["g_norm"][1, 0][None],
               wts["w_kv16"], wts["w_q16"], wts["gmat"], wts["gmat_t"], wts["g_kn"], wts["g_qn"], tm, T)
    v32 = kvq[1]
    W = v32.shape[1]
    H = W // LANES
    lam_init = 0.8 - 0.6 * math.exp(-0.3 * 1)
    if past is None:
        kt32, _, q, k16, vt16 = kvq
        o = _attn_prompt(q.reshape(B, T, W), k16.reshape(B, T, W), vt16, wts["rpb_tbl"], wts["lam"],
                         wts["g_hn_b"].reshape(W, 1), lam_init, math.gcd(T, ATTN_TQ), math.gcd(T, ATTN_TK))
        o = o.reshape(N, W)
        k_out = jnp.transpose(kt32.reshape(B, H, 2, B_DK, T), (0, 4, 1, 2, 3))
    else:
        k32, _, q = kvq
        k_out = k32.reshape(B, T, H, 2, B_DK)
        cache_k, cache_v, page_table = past
        o = _attn_paged(q.reshape(B, T, W), cache_k, cache_v, page_table, k32.reshape(B, T, W),
                        v32.reshape(B, T, W), wts["rpb_tbl"], wts["lam"], wts["g_hn_b"], lam_init)
        o = o.reshape(N, W).astype(BF16)

    x2, h16, rg = _proj_router(o, wts["w_o16"], x2, rmod(mods[2][:, 2 * D:]), rmod(mods[3][:, :D]),
                               rmod(mods[3][:, D:2 * D]), wts["g_norm"][1, 1][None], wts["w_router"],
                               wts["b_router"], tr)
    x2 = _moe(h16, rg, x2, _Mod(mods[3][:, 2 * D:], T, tb), wts["wg16"], wts["wu16"], wts["wd16"], 1, tb)

    return x2.reshape(B, T, D), new_state, k_out, v32.reshape(B, T, H, B_DV)


def kernel(x_prompt, x_sample, c_prompt, c_sample, state_conv, state_C, state_n, state_m, cache_k, cache_v, page_table, w_ada, b_ada, g_norm, w_in_a, b_gate_a, w_conv_a, b_conv_a, g_hn_a, w_out_a, g_kv, w_ada_kv, b_ada_kv, w_kv, g_kn, w_q_b, g_qn_b, lam_b, g_hn_b, w_o_b, rpb, w_router, b_router, w_gate_e, w_up_e, w_down_e):
    Bp, Tp, D = x_prompt.shape
    Bs = x_sample.shape[0]
    inner = w_out_a.shape[1]
    heads_b = g_hn_b.shape[1]
    W = heads_b * B_DV

    n_c = Bp + Bs
    c_all = jnp.concatenate([c_prompt, c_sample, jnp.zeros((-n_c % SUBLANES, D), F32)], axis=0)
    mods = _ada(c_all, w_ada.reshape(-1, D, 3 * D), b_ada.reshape(-1, 1, 3 * D))
    mods_kv = _ada(c_all, w_ada_kv[None], b_ada_kv[None, None])[0]

    n_gate = 2 * A_HEADS
    w_in_t = jnp.swapaxes(w_in_a[0], 0, 1)
    group_of_lane = np.arange(W) // B_DK
    gmat = jnp.asarray((group_of_lane[:, None] == np.arange(LANES)[None, :]).astype(np.float32)).astype(BF16)
    wts = {
        "g_norm": g_norm,
        "w_in16": w_in_t[:4 * inner].astype(BF16),
        "w_gate": jnp.pad(w_in_t[4 * inner:], ((0, LANES - n_gate), (0, 0))),
        "bg": jnp.pad(b_gate_a[0], (0, LANES - n_gate))[None, :],
        "bgt": b_gate_a[0][:, None],
        "w_conv": w_conv_a[0], "b_conv": b_conv_a[0][None, :],
        "g_hn_a": g_hn_a[0].reshape(1, inner),
        "w_out16": w_out_a[0].astype(BF16),
        "g_kv": g_kv[None, :],
        "w_kv16": w_kv.astype(BF16), "w_q16": w_q_b[0].astype(BF16),
        "gmat": gmat, "gmat_t": gmat.T,
        "g_kn": jnp.tile(g_kn.reshape(-1), heads_b)[None, :],
        "g_qn": jnp.tile(g_qn_b[0].reshape(-1), heads_b)[None, :],
        "lam": lam_b[0], "g_hn_b": g_hn_b[0].reshape(1, W),
        "w_o16": w_o_b[0].astype(BF16),
        "rpb_tbl": _bias_table(rpb),
        "w_router": jnp.pad(w_router, ((0, 0), (0, LANES - N_EXPERTS))),
        "b_router": b_router[:, None],
        "wg16": w_gate_e.astype(BF16), "wu16": w_up_e.astype(BF16), "wd16": w_down_e.astype(BF16),
    }

    y_p, st_p, k_p, v_p = _trunk(x_prompt, mods[:, :Bp], mods_kv[:Bp], None, None, wts)
    y_s, st_s, k_s, v_s = _trunk(x_sample, mods[:, Bp:n_c], mods_kv[Bp:n_c],
                                 (state_conv[0], state_C[0], state_n[0], state_m[0]),
                                 (cache_k, cache_v, page_table), wts)
    stack = lambda st: tuple(a[None] for a in st)
    return (y_p, y_s) + stack(st_p) + (k_p, v_p) + stack(st_s) + (k_s, v_s)
```

```python
import functools
import math

import numpy as np
import jax
import jax.numpy as jnp
from jax import lax
from jax.experimental import pallas as pl
from jax.experimental.pallas import tpu as pltpu

F32, BF16 = jnp.float32, jnp.bfloat16
HIGHEST = lax.Precision.HIGHEST
EPS = 1e-6

A_HEADS = 4
A_CONV = 4
B_DK = 64
B_DV = 128
N_EXPERTS = 16
N_GROUPS = 4
RPB_BUCKETS = 32
RPB_MAX_DIST = 128

LANES = 128
SUBLANES = 8
VMEM_LIMIT_BYTES = 56 * 1024 * 1024

ROW_TILE = 512
INPROJ_TILE = 1024
ROUTER_TILE = 1024
MLSTM_CHUNK = 256
ATTN_TQ = 512
ATTN_TK = 512


def _params(*sem):
    return pltpu.CompilerParams(dimension_semantics=sem, vmem_limit_bytes=VMEM_LIMIT_BYTES)


def _nt_dot(a, b):
    return lax.dot_general(a, b, (((1,), (1,)), ((), ())), preferred_element_type=F32)


def _tn_dot(a, b):
    return lax.dot_general(a, b, (((0,), (0,)), ((), ())), preferred_element_type=F32)


def _silu(x):
    half = 0.5 * x
    return half + half * jnp.tanh(half)


def _rms_mod(x, g, scale, shift):
    y = x * lax.rsqrt(jnp.mean(x * x, axis=-1, keepdims=True) + EPS)
    return (y * g) * (1.0 + scale) + shift


def _dot_split(a, b16):
    hi = a.astype(BF16)
    lo = (a - hi.astype(F32)).astype(BF16)
    return (jnp.dot(hi, b16, preferred_element_type=F32) + jnp.dot(lo, b16, preferred_element_type=F32))


def _dot_x3(a, b, dims):
    a_hi, b_hi = a.astype(BF16), b.astype(BF16)
    a_lo = (a - a_hi.astype(F32)).astype(BF16)
    b_lo = (b - b_hi.astype(F32)).astype(BF16)
    dot = lambda x, y: lax.dot_general(x, y, dims, preferred_element_type=F32)
    return dot(a_hi, b_hi) + dot(a_hi, b_lo) + dot(a_lo, b_hi)


def _group_rms(x, gmat, gmat_t, g, group):
    ss = jnp.dot((x * x).astype(BF16), gmat, preferred_element_type=F32)
    r = lax.rsqrt(ss * (1.0 / group) + EPS)
    rf = _dot_split(r, gmat_t)
    return x * rf * g


def _ada_kernel(c_ref, w_ref, b_ref, o_ref):
    a = _silu(c_ref[...])
    o_ref[0] = _dot_x3(a, w_ref[0], (((1,), (0,)), ((), ()))) + b_ref[0]


def _ada(c_all, w, b):
    S, D, Fo = w.shape
    R = c_all.shape[0]
    tn = 1024
    return pl.pallas_call(
        _ada_kernel,
        grid=(S, Fo // tn),
        in_specs=[pl.BlockSpec((R, D), lambda s, j: (0, 0)),
                  pl.BlockSpec((1, D, tn), lambda s, j: (s, 0, j)),
                  pl.BlockSpec((1, 1, tn), lambda s, j: (s, 0, j))],
        out_specs=pl.BlockSpec((1, R, tn), lambda s, j: (s, 0, j)),
        out_shape=jax.ShapeDtypeStruct((S, R, Fo), F32),
        compiler_params=_params("parallel", "parallel"),
        name="ada",
    )(c_all, w, b)


class _Mod:
    def __init__(self, m, T, tm):
        B, D = m.shape
        if T % tm == 0:
            self.arr, self.tiles_per_group = m[:, None, :], T // tm
        else:
            assert (B * T) % tm == 0
            self.arr, self.tiles_per_group = jnp.repeat(m, T, axis=0).reshape(-1, tm, D), 1

    def spec(self, grid_rank):
        R, D = self.arr.shape[1:]
        tpg = self.tiles_per_group
        if grid_rank == 1:
            return pl.BlockSpec((1, R, D), lambda i: (i // tpg, 0, 0))
        return pl.BlockSpec((1, R, D), lambda i, j: (i // tpg, 0, 0))


def _causal_conv(x, tail, w, b):
    L, width = x.shape
    row8 = lax.broadcasted_iota(jnp.int32, (SUBLANES, width), 0)
    acc = b + x * w[A_CONV - 1:A_CONV]
    for s in range(1, A_CONV):
        xs = pltpu.roll(x, s, 0)
        top = jnp.where(row8 < s, pltpu.roll(tail, s, 0), xs[:SUBLANES])
        xs = top if L == SUBLANES else jnp.concatenate([top, xs[SUBLANES:]], axis=0)
        acc = acc + xs * w[A_CONV - 1 - s:A_CONV - s]
    return acc


def _inproj_kernel(x_ref, sh_ref, sc_ref, g_ref, w_ref, wg_ref, o_ref, og_ref, h_sc):
    @pl.when(pl.program_id(1) == 0)
    def _():
        h = _rms_mod(x_ref[...], g_ref[...], sc_ref[0], sh_ref[0])
        h_sc[...] = h.astype(BF16)
        og_ref[...] = _dot_x3(h, wg_ref[...], (((1,), (1,)), ((), ())))

    o_ref[...] = _nt_dot(h_sc[...], w_ref[...])


def _inproj(x2, shift, scale, g, w16_t, wgate_t, tm):
    N, D = x2.shape
    Fo = w16_t.shape[0]
    tn = 2048
    return pl.pallas_call(
        _inproj_kernel,
        grid=(N // tm, Fo // tn),
        in_specs=[pl.BlockSpec((tm, D), lambda i, j: (i, 0)),
                  shift.spec(2), scale.spec(2),
                  pl.BlockSpec((1, D), lambda i, j: (0, 0)),
                  pl.BlockSpec((tn, D), lambda i, j: (j, 0)),
                  pl.BlockSpec((LANES, D), lambda i, j: (0, 0))],
        out_specs=[pl.BlockSpec((tm, tn), lambda i, j: (i, j)),
                   pl.BlockSpec((tm, LANES), lambda i, j: (i, 0))],
        out_shape=[jax.ShapeDtypeStruct((N, Fo), F32), jax.ShapeDtypeStruct((N, LANES), F32)],
        scratch_shapes=[pltpu.VMEM((tm, D), BF16)],
        compiler_params=_params("parallel", "arbitrary"),
        name="mlstm_inproj",
    )(x2, shift.arr, scale.arr, g, w16_t, wgate_t)


def _mlstm_kernel(q_ref, k_ref, v_ref, o_ref, gt_ref, gtt_ref, bg_ref, bgt_ref, cinit_ref,
                  c0_ref, n0_ref, m0_ref, wconv_ref, bconv_ref, ghn_ref,
                  hs_ref, c_ref, n_ref, m_ref, tail_sc, *, L, dh, heads, t_valid):
    inner = heads * dh

    @pl.when(pl.program_id(1) == 0)
    def _():
        c_ref[...] = c0_ref[...]
        n_ref[...] = n0_ref[...]
        m_ref[...] = m0_ref[...]
        tail_sc[...] = cinit_ref[0]

    gt = gt_ref[0] + bg_ref[...]
    gtt = gtt_ref[0] + bgt_ref[...]
    ti = lax.broadcasted_iota(jnp.int32, (L, L), 0)
    si = lax.broadcasted_iota(jnp.int32, (L, L), 1)
    causal = si <= ti
    tcol = lax.broadcasted_iota(jnp.int32, (L, 1), 0)
    trow = lax.broadcasted_iota(jnp.int32, (1, L), 1)

    for h in range(heads):
        sl = slice(h * dh, (h + 1) * dh)
        slk = slice(inner + h * dh, inner + (h + 1) * dh)
        qh = _silu(_causal_conv(q_ref[0, :, sl], tail_sc[:, sl], wconv_ref[:, sl], bconv_ref[:, sl]))
        kh = _silu(_causal_conv(k_ref[0, :, sl], tail_sc[:, slk], wconv_ref[:, slk], bconv_ref[:, slk])) * (dh ** -0.5)
        vb = v_ref[0, :, sl].astype(BF16)

        ig_col = gt[:, h:h + 1]
        lf_col = jax.nn.log_sigmoid(gt[:, heads + h:heads + h + 1])
        ig_row = gtt[h:h + 1, :]
        lf_row = jax.nn.log_sigmoid(gtt[heads + h:heads + h + 1, :])
        if t_valid is not None:
            ig_col = jnp.where(tcol < t_valid, ig_col, -jnp.inf)
            lf_col = jnp.where(tcol < t_valid, lf_col, 0.0)
            ig_row = jnp.where(trow < t_valid, ig_row, -jnp.inf)
            lf_row = jnp.where(trow < t_valid, lf_row, 0.0)

        b_col = jnp.sum(jnp.where(causal, lf_row, 0.0), axis=1, keepdims=True)
        b_row = jnp.sum(jnp.where(ti <= si, lf_col, 0.0), axis=0, keepdims=True)
        dlog = jnp.where(causal, b_col - b_row + ig_row, -jnp.inf)
        g_col = b_col + m_ref[0, h]
        m_col = jnp.maximum(g_col, jnp.max(dlog, axis=1, keepdims=True))
        w_intra = jnp.exp(dlog - m_col)
        w_inter = jnp.exp(g_col - m_col)

        qb = qh.astype(BF16)
        kb = kh.astype(BF16)
        s = w_intra * _nt_dot(qb, kb)
        ch = c_ref[0, h]
        nh = n_ref[0, h]
        num = w_inter * _nt_dot(qb, ch.astype(BF16)) + jnp.dot(s.astype(BF16), vb, preferred_element_type=F32)
        den = w_inter * jnp.sum(qh * nh, axis=1, keepdims=True) + jnp.sum(s, axis=1, keepdims=True)
        hv = num / jnp.maximum(jnp.abs(den), jnp.exp(-m_col))

        m_end = m_col[L - 1:L]
        we_inter = jnp.exp(g_col[L - 1:L] - m_end)
        we_col = jnp.exp(b_col[L - 1:L] - b_col + ig_col - m_end)
        kw = kh * we_col
        c_ref[0, h] = we_inter * ch + _tn_dot(vb, kw.astype(BF16))
        n_ref[0, h] = we_inter * nh + jnp.sum(kw, axis=0, keepdims=True)
        m_ref[0, h] = m_end

        hn = hv * lax.rsqrt(jnp.mean(hv * hv, axis=1, keepdims=True) + EPS) * ghn_ref[:, sl]
        hs_ref[0, :, sl] = (jax.nn.sigmoid(o_ref[0, :, sl]) * hn).astype(hs_ref.dtype)

    tail_sc[:, :inner] = q_ref[0, L - SUBLANES:, :]
    tail_sc[:, inner:] = k_ref[0, L - SUBLANES:, :]


def _mlstm(proj, gates, gates_t, bg, bgt, conv_init, c0, n0, m0, wconv, bconv, ghn, L, t_valid):
    B, Tp, _ = proj.shape
    heads, dh = c0.shape[1], c0.shape[2]
    inner = heads * dh
    nc = Tp // L
    kern = functools.partial(_mlstm_kernel, L=L, dh=dh, heads=heads, t_valid=t_valid)
    col = lambda j: pl.BlockSpec((1, L, inner), lambda b, c: (b, c, j))
    full = lambda shape: pl.BlockSpec(shape, lambda b, c: (0,) * len(shape))
    per_b = lambda shape: pl.BlockSpec((1,) + shape, lambda b, c: (b,) + (0,) * len(shape))
    return pl.pallas_call(
        kern,
        grid=(B, nc),
        in_specs=[col(0), col(1), col(2), col(3),
                  pl.BlockSpec((1, L, LANES), lambda b, c: (b, c, 0)),
                  pl.BlockSpec((1, SUBLANES, L), lambda b, c: (b, 0, c)),
                  full((1, LANES)), full((SUBLANES, 1)),
                  per_b((SUBLANES, 2 * inner)),
                  per_b((heads, dh, dh)), per_b((heads, 1, dh)), per_b((heads, 1, 1)),
                  full((A_CONV, 2 * inner)), full((1, 2 * inner)), full((1, inner))],
        out_specs=[pl.BlockSpec((1, L, inner), lambda b, c: (b, c, 0)),
                   per_b((heads, dh, dh)), per_b((heads, 1, dh)), per_b((heads, 1, 1))],
        out_shape=[jax.ShapeDtypeStruct((B, Tp, inner), BF16),
                   jax.ShapeDtypeStruct((B, heads, dh, dh), F32),
                   jax.ShapeDtypeStruct((B, heads, 1, dh), F32),
                   jax.ShapeDtypeStruct((B, heads, 1, 1), F32)],
        scratch_shapes=[pltpu.VMEM((SUBLANES, 2 * inner), F32)],
        compiler_params=_params("parallel", "arbitrary"),
        name="mlstm",
    )(proj, proj, proj, proj, gates, gates_t, bg, bgt, conv_init, c0, n0, m0, wconv, bconv, ghn)


ROUTE_ROWS = SUBLANES


def _route(h, wr, br):
    tm = h.shape[0]
    per = N_EXPERTS // N_GROUPS
    logits = _dot_x3(h, wr, (((1,), (0,)), ((), ())))
    lt = logits.T[:N_EXPERTS]
    s = jax.nn.sigmoid(lt)
    sel = s + br
    neg = jnp.full((1, tm), -jnp.inf, F32)
    izero = jnp.zeros((1, tm), jnp.int32)

    best_score = best_e1 = best_e2 = best_w1 = best_w2 = None
    for grp in range(N_GROUPS):
        rows = [sel[grp * per + j:grp * per + j + 1] for j in range(per)]
        srow = [s[grp * per + j:grp * per + j + 1] for j in range(per)]
        t1, i1, w1 = rows[0], izero, srow[0]
        for j in range(1, per):
            better = rows[j] > t1
            t1 = jnp.where(better, rows[j], t1)
            i1 = jnp.where(better, j, i1)
            w1 = jnp.where(better, srow[j], w1)
        t2, i2, w2 = neg, izero, srow[0]
        for j in range(per):
            better = jnp.where(i1 == j, neg, rows[j]) > t2
            t2 = jnp.where(better, rows[j], t2)
            i2 = jnp.where(better, j, i2)
            w2 = jnp.where(better, srow[j], w2)
        score = t1 + t2
        e1, e2 = i1 + grp * per, i2 + grp * per
        if grp == 0:
            best_score, best_e1, best_e2, best_w1, best_w2 = score, e1, e2, w1, w2
        else:
            better = score > best_score
            best_score = jnp.where(better, score, best_score)
            best_e1 = jnp.where(better, e1, best_e1)
            best_e2 = jnp.where(better, e2, best_e2)
            best_w1 = jnp.where(better, w1, best_w1)
            best_w2 = jnp.where(better, w2, best_w2)
    tot = best_w1 + best_w2
    zero = jnp.zeros((ROUTE_ROWS - 4, tm), F32)
    return jnp.concatenate([best_e1.astype(F32), best_e2.astype(F32), best_w1 / tot, best_w2 / tot, zero], axis=0)


def _proj_router_kernel(a_ref, w_ref, x_ref, gate_ref, sh_ref, sc_ref, g_ref, wr_ref, br_ref,
                        xo_ref, h_ref, route_ref):
    mix = jnp.dot(a_ref[...], w_ref[...], preferred_element_type=F32)
    x = x_ref[...] + gate_ref[0] * mix
    xo_ref[...] = x
    h = _rms_mod(x, g_ref[...], sc_ref[0], sh_ref[0])
    h_ref[...] = h.astype(BF16)
    route_ref[...] = _route(h, wr_ref[...], br_ref[...])


def _proj_router(a16, w16, x2, gate, shift, scale, g, wr, br, tm):
    N, D = x2.shape
    K = a16.shape[1]
    row = lambda w: pl.BlockSpec((tm, w), lambda i: (i, 0))
    full = lambda shape: pl.BlockSpec(shape, lambda i: (0,) * len(shape))
    return pl.pallas_call(
        _proj_router_kernel,
        grid=(N // tm,),
        in_specs=[row(K), full((K, D)), row(D), gate.spec(1), shift.spec(1), scale.spec(1),
                  full((1, D)), full((D, LANES)), full((N_EXPERTS, 1))],
        out_specs=[row(D), row(D), pl.BlockSpec((ROUTE_ROWS, tm), lambda i: (0, i))],
        out_shape=[jax.ShapeDtypeStruct((N, D), F32), jax.ShapeDtypeStruct((N, D), BF16),
                   jax.ShapeDtypeStruct((ROUTE_ROWS, N), F32)],
        compiler_params=_params("parallel"),
        name="proj_router",
    )(a16, w16, x2, gate.arr, shift.arr, scale.arr, g, wr, br)


MOE_BLOCK = 1024
MOE_WINDOW = 256
MOE_CHUNK = 256
SEG_ALIGN = 16


def _moe_kernel(route_ref, h_ref, x_ref, gate_ref, tri_ref, wg_ref, wu_ref, wd_ref, o_ref,
                xs_sc, ys_sc, gs_sc, tok_sc, seg_sc, *, n_experts, n_sorted):
    e = pl.program_id(1)
    tb = h_ref.shape[0]
    S, CH, RW = n_sorted, MOE_CHUNK, MOE_WINDOW
    one_hot = lambda a, b: jnp.where(a, 1.0, jnp.where(b, 1.0, 0.0))

    @pl.when(e == 0)
    def _dispatch():
        route = route_ref[...]
        e1, e2, w1, w2 = route[0:1], route[1:2], route[2:3], route[3:4]
        eid = lax.broadcasted_iota(jnp.int32, (n_experts, tb), 0).astype(F32)
        hit1, hit2 = eid == e1, eid == e2
        routed = one_hot(hit1, hit2)
        earlier = jnp.dot(routed.astype(BF16), tri_ref[...], preferred_element_type=F32)
        count = jnp.sum(routed, axis=1, keepdims=True)
        padded = jnp.floor((count + (SEG_ALIGN - 1)) * (1.0 / SEG_ALIGN)) * SEG_ALIGN
        below = jnp.where(lax.broadcasted_iota(jnp.int32, (n_experts, n_experts), 1)
                          < lax.broadcasted_iota(jnp.int32, (n_experts, n_experts), 0), 1.0, 0.0)
        start = jnp.dot(below, jnp.broadcast_to(padded, (n_experts, LANES)), precision=HIGHEST,
                        preferred_element_type=F32)[:, :1]
        pos = start + earlier
        pos1 = jnp.sum(jnp.where(hit1, pos, 0.0), axis=0, keepdims=True)
        pos2 = jnp.sum(jnp.where(hit2, pos, 0.0), axis=0, keepdims=True)
        for ex in range(n_experts):
            seg_sc[0, ex] = start[ex, 0].astype(jnp.int32)
            seg_sc[1, ex] = (start[ex, 0] + count[ex, 0]).astype(jnp.int32)
        tok_sc[...] = jnp.concatenate([pos1, pos2, jnp.zeros((LANES - 2, tb), F32)], axis=0).T

        def gather(c, carry):
            r0 = pl.multiple_of(c * CH, CH)
            row = (lax.broadcasted_iota(jnp.int32, (CH, tb), 0) + r0).astype(F32)
            is1, is2 = row == pos1, row == pos2
            xs_sc[pl.ds(r0, CH), :] = jnp.dot(one_hot(is1, is2).astype(BF16), h_ref[...],
                                              preferred_element_type=F32).astype(BF16)
            gs_sc[pl.ds(r0, CH), :] = jnp.sum(jnp.where(is1, w1, jnp.where(is2, w2, 0.0)), axis=1, keepdims=True)
            return carry

        lax.fori_loop(0, S // CH, gather, 0)
        xs_sc[S:, :] = jnp.zeros((RW, xs_sc.shape[1]), BF16)
        gs_sc[S:, :] = jnp.zeros((RW, 1), F32)
        ys_sc[...] = jnp.zeros_like(ys_sc)

    seg_start, seg_end = seg_sc[0, e], seg_sc[1, e]

    def window(w, carry):
        r0 = pl.multiple_of(seg_start + w * RW, SEG_ALIGN)
        rows = xs_sc[pl.ds(r0, RW), :]
        a = jnp.dot(rows, wg_ref[0], preferred_element_type=F32)
        u = jnp.dot(rows, wu_ref[0], preferred_element_type=F32)
        mine = lax.broadcasted_iota(jnp.int32, (RW, 1), 0) + r0 < seg_end
        act = jnp.where(mine, _silu(a) * u * gs_sc[pl.ds(r0, RW), :], 0.0)
        ys_sc[pl.ds(r0, RW), :] += jnp.dot(act.astype(BF16), wd_ref[0], preferred_element_type=F32)
        return carry

    lax.fori_loop(0, (seg_end - seg_start + (RW - 1)) // RW, window, 0)

    @pl.when(e == n_experts - 1)
    def _combine():
        def to_bf16(c, carry):
            r0 = pl.multiple_of(c * CH, CH)
            xs_sc[pl.ds(r0, CH), :] = ys_sc[pl.ds(r0, CH), :].astype(BF16)
            return carry

        lax.fori_loop(0, S // CH, to_bf16, 0)

        def scatter(c, carry):
            t0 = pl.multiple_of(c * CH, CH)
            rec = tok_sc[pl.ds(t0, CH), :]
            col = lax.broadcasted_iota(jnp.int32, (CH, S), 1).astype(F32)
            sel = one_hot(col == rec[:, 0:1], col == rec[:, 1:2]).astype(BF16)
            y = jnp.dot(sel, xs_sc[0:S, :], preferred_element_type=F32)
            g = gate_ref[0] if gate_ref.shape[1] == 1 else gate_ref[0, pl.ds(t0, CH), :]
            o_ref[pl.ds(t0, CH), :] = x_ref[pl.ds(t0, CH), :] + g * y
            return carry

        lax.fori_loop(0, tb // CH, scatter, 0)


def _moe(h16, route, x2, gate, wg16, wu16, wd16, layer, tb):
    N, D = x2.shape
    _, E, _, Fe = wg16.shape
    n_sorted = -(-(2 * tb + E * SEG_ALIGN) // MOE_CHUNK) * MOE_CHUNK
    tri = jnp.asarray(np.triu(np.ones((tb, tb), np.float32), 1), dtype=BF16)
    row = lambda w: pl.BlockSpec((tb, w), lambda i, e: (i, 0))
    kern = functools.partial(_moe_kernel, n_experts=E, n_sorted=n_sorted)
    return pl.pallas_call(
        kern,
        grid=(N // tb, E),
        in_specs=[pl.BlockSpec((ROUTE_ROWS, tb), lambda i, e: (0, i)), row(D), row(D), gate.spec(2),
                  pl.BlockSpec((tb, tb), lambda i, e: (0, 0)),
                  pl.BlockSpec((None, 1, D, Fe), lambda i, e: (layer, e, 0, 0)),
                  pl.BlockSpec((None, 1, D, Fe), lambda i, e: (layer, e, 0, 0)),
                  pl.BlockSpec((None, 1, Fe, D), lambda i, e: (layer, e, 0, 0))],
        out_specs=row(D),
        out_shape=jax.ShapeDtypeStruct((N, D), F32),
        scratch_shapes=[pltpu.VMEM((n_sorted + MOE_WINDOW, D), BF16), pltpu.VMEM((n_sorted + MOE_WINDOW, D), F32),
                        pltpu.VMEM((n_sorted + MOE_WINDOW, 1), F32), pltpu.VMEM((tb, LANES), F32),
                        pltpu.SMEM((2, E), jnp.int32)],
        compiler_params=_params("parallel", "arbitrary"),
        name="moe",
    )(route, h16, x2, gate.arr, tri, wg16, wu16, wd16)


def _kvq_kernel(x_ref, shk_ref, sck_ref, gk_ref, shq_ref, scq_ref, gq_ref, wkv_ref, wq_ref,
                gmat_ref, gmatt_ref, gkn_ref, gqn_ref, k_ref, v32_ref, q_ref, *maybe_attn_refs):
    x = x_ref[...]
    y = x * lax.rsqrt(jnp.mean(x * x, axis=-1, keepdims=True) + EPS)
    hk = ((y * gk_ref[...]) * (1.0 + sck_ref[0]) + shk_ref[0]).astype(BF16)
    hq = ((y * gq_ref[...]) * (1.0 + scq_ref[0]) + shq_ref[0]).astype(BF16)
    W = v32_ref.shape[1]
    kv = jnp.dot(hk, wkv_ref[...], preferred_element_type=F32)
    k = _group_rms(kv[:, :W], gmat_ref[...], gmatt_ref[...], gkn_ref[...], B_DK)
    v = kv[:, W:]
    v32_ref[...] = v
    q = jnp.dot(hq, wq_ref[...], preferred_element_type=F32)
    q = _group_rms(q, gmat_ref[...], gmatt_ref[...], gqn_ref[...], B_DK)
    q_ref[...] = (q * (B_DK ** -0.5)).astype(q_ref.dtype)
    if maybe_attn_refs:
        k16_ref, vt_ref = maybe_attn_refs
        k_ref[0] = k.T
        k16_ref[...] = k.astype(BF16)
        vt_ref[0] = v.T.astype(BF16)
    else:
        k_ref[...] = k


def _kvq(x2, shk, sck, gk, shq, scq, gq, wkv16, wq16, gmat, gmat_t, gkn, gqn, tm, seq_len):
    N, D = x2.shape
    W = wq16.shape[1]
    row = lambda w: pl.BlockSpec((tm, w), lambda i: (i, 0))
    full = lambda shape: pl.BlockSpec(shape, lambda i: (0,) * len(shape))
    if seq_len % tm == 0:
        tps = seq_len // tm
        col = pl.BlockSpec((1, W, tm), lambda i: (i // tps, 0, i % tps))
        out_specs = [col, row(W), row(W), row(W), col]
        out_shape = [jax.ShapeDtypeStruct((N // seq_len, W, seq_len), F32), jax.ShapeDtypeStruct((N, W), F32),
                     jax.ShapeDtypeStruct((N, W), BF16), jax.ShapeDtypeStruct((N, W), BF16),
                     jax.ShapeDtypeStruct((N // seq_len, W, seq_len), BF16)]
    else:
        out_specs = [row(W), row(W), row(W)]
        out_shape = [jax.ShapeDtypeStruct((N, W), F32)] * 3
    return pl.pallas_call(
        _kvq_kernel,
        grid=(N // tm,),
        in_specs=[row(D), shk.spec(1), sck.spec(1), full((1, D)), shq.spec(1), scq.spec(1), full((1, D)),
                  full((D, 2 * W)), full((D, W)), full((W, LANES)), full((LANES, W)),
                  full((1, W)), full((1, W))],
        out_specs=out_specs,
        out_shape=out_shape,
        compiler_params=_params("parallel"),
        name="kvq",
    )(x2, shk.arr, sck.arr, gk, shq.arr, scq.arr, gq, wkv16, wq16, gmat, gmat_t, gkn, gqn)


def _lambda(lam_ref, lam_init):
    lv = lam_ref[...]
    a = jnp.sum(lv[0:1] * lv[1:2], axis=1, keepdims=True)
    b = jnp.sum(lv[2:3] * lv[3:4], axis=1, keepdims=True)
    return jnp.exp(a) - jnp.exp(b) + lam_init


ATTN_HEADS_PER_STEP = 8
ONES_ROWS = SUBLANES


def _attn_kernel(qi_ref, kj_ref, ty_ref, fin_ref, q_ref, k_ref, vt_ref, bias_ref, lam_ref, ghn_ref, o_ref,
                 qm_sc, m_sc, acc_sc, *, n_types, lam_init):
    step = pl.program_id(2)
    ty = ty_ref[step]
    tq = q_ref.shape[1]
    heads = ATTN_HEADS_PER_STEP
    hl = lambda hh: slice(hh * LANES, (hh + 1) * LANES)

    @pl.when(kj_ref[step] == 0)
    def _():
        q = q_ref[0]
        lane = lax.broadcasted_iota(jnp.int32, (tq, LANES), 1)
        zero = jnp.zeros((tq, LANES), BF16)
        for hh in range(heads):
            qh = q[:, hl(hh)]
            qm_sc[hh, 0:tq, :] = jnp.where(lane < B_DK, qh, zero)
            qm_sc[hh, tq:2 * tq, :] = jnp.where(lane >= B_DK, qh, zero)
        m_sc[...] = jnp.full_like(m_sc, -jnp.inf)
        acc_sc[...] = jnp.zeros_like(acc_sc)

    def update(adj_of):
        ones = jnp.ones((ONES_ROWS, k_ref.shape[1]), BF16)

        def scores(hh):
            return _nt_dot(k_ref[0, :, hl(hh)], qm_sc[hh])

        def softmax(hh, s):
            if adj_of is not None:
                s = jnp.concatenate([s[:, :tq] + adj_of(hh), s[:, tq:] + adj_of(hh)], axis=1)
            m_old = m_sc[hh]
            m_new = jnp.maximum(m_old, jnp.max(s, axis=0, keepdims=True))
            m_sc[hh] = m_new
            return jnp.exp(m_old - m_new), jnp.exp(s - m_new).astype(BF16)

        def weighted_values(hh, alpha, p):
            vt1 = jnp.concatenate([vt_ref[0, hl(hh), :], ones], axis=0)
            acc_sc[hh] = alpha * acc_sc[hh] + jnp.dot(vt1, p, preferred_element_type=F32)

        s_of, ap_of = {}, {}
        for t in range(heads + 3):
            if t - 3 >= 0:
                weighted_values(t - 3, *ap_of.pop(t - 3))
            if 0 <= t - 2 < heads:
                ap_of[t - 2] = softmax(t - 2, s_of.pop(t - 2))
            if t < heads:
                s_of[t] = scores(t)

    @pl.when(ty < 0)
    def _():
        update(None)

    for t in range(n_types):
        @pl.when(ty == t)
        def _(t=t):
            update(lambda hh, t=t: bias_ref[hh, t])

    @pl.when(fin_ref[step] == 1)
    def _():
        lam = _lambda(lam_ref, lam_init)
        for hh in range(heads):
            acc = acc_sc[hh]
            both = acc[0:B_DV] / acc[B_DV:B_DV + 1]
            ot = both[:, 0:tq] - lam * both[:, tq:2 * tq]
            on = ot * lax.rsqrt(jnp.mean(ot * ot, axis=0, keepdims=True) + EPS) * ghn_ref[hl(hh), :] * (1.0 - lam_init)
            o_ref[0, :, hl(hh)] = on.T.astype(o_ref.dtype)


def _attn_schedule(T, tq, tk):
    offsets = sorted({qi * tq - kj * tk for qi in range(T // tq) for kj in range(T // tk)
                      if qi * tq + tq - 1 >= kj * tk and qi * tq - kj * tk - (tk - 1) < RPB_MAX_DIST})
    qi_l, kj_l, ty_l, fin_l = [], [], [], []
    for qi in range(T // tq):
        kjs = [kj for kj in range(T // tk) if qi * tq + tq - 1 >= kj * tk]
        for kj in kjs:
            off = qi * tq - kj * tk
            qi_l.append(qi)
            kj_l.append(kj)
            ty_l.append(offsets.index(off) if off in offsets else -1)
            fin_l.append(int(kj == kjs[-1]))
    as_i32 = lambda v: jnp.asarray(np.asarray(v, np.int32))
    return offsets, as_i32(qi_l), as_i32(kj_l), as_i32(ty_l), as_i32(fin_l)


def _bias_table(rpb):
    n = jnp.arange(RPB_MAX_DIST, dtype=jnp.int32)
    max_exact = RPB_BUCKETS // 2
    nf = jnp.maximum(n, 1).astype(F32)
    large = max_exact + (jnp.log(nf / max_exact) / math.log(RPB_MAX_DIST / max_exact)
                         * (RPB_BUCKETS - max_exact)).astype(jnp.int32)
    bucket = jnp.where(n < max_exact, n, jnp.minimum(large, RPB_BUCKETS - 1))
    return (rpb[bucket] - rpb[RPB_BUCKETS - 1][None, :]).T.astype(F32)


def _bias_of_distance(tbl, dist):
    d = np.asarray(dist)
    idx = jnp.asarray(np.clip(d, 0, RPB_MAX_DIST - 1).astype(np.int32))
    vals = jnp.take(tbl, idx, axis=1)
    vals = jnp.where(jnp.asarray(d >= RPB_MAX_DIST), 0.0, vals)
    return jnp.where(jnp.asarray(d < 0), -jnp.inf, vals)


def _toeplitz_kernel(v_ref, o_ref):
    tk, tq = o_ref.shape[1:]
    rows = jnp.broadcast_to(v_ref[0], (tk, v_ref.shape[2]))
    o_ref[0] = pltpu.roll(rows, 0, 1, stride=1, stride_axis=0)[:, :tq]


def _bias_tiles_t(tbl, offsets, tq, tk):
    period = tq + tk
    w = np.arange(period)
    u = np.where(w < tq, w, w - period)
    vext = _bias_of_distance(tbl, np.stack([off + u for off in offsets]))
    H, n_types = vext.shape[:2]
    tiles = pl.pallas_call(
        _toeplitz_kernel,
        grid=(H * n_types,),
        in_specs=[pl.BlockSpec((1, 1, period), lambda i: (i, 0, 0))],
        out_specs=pl.BlockSpec((1, tk, tq), lambda i: (i, 0, 0)),
        out_shape=jax.ShapeDtypeStruct((H * n_types, tk, tq), F32),
        compiler_params=_params("parallel"),
        name="bias_tiles",
    )(vext.reshape(H * n_types, 1, period))
    return tiles.reshape(H, n_types, tk, tq)


def _attn_prompt(q16, k16, vt16, tbl, lam, ghn_col, lam_init, tq, tk):
    B, T, W = q16.shape
    H = W // LANES
    hps = ATTN_HEADS_PER_STEP
    hw = hps * LANES
    assert H % hps == 0
    offsets, qi, kj, ty, fin = _attn_schedule(T, tq, tk)
    bias = _bias_tiles_t(tbl, offsets, tq, tk)
    kern = functools.partial(_attn_kernel, n_types=len(offsets), lam_init=lam_init)
    grid_spec = pltpu.PrefetchScalarGridSpec(
        num_scalar_prefetch=4,
        grid=(H // hps, B, int(qi.shape[0])),
        in_specs=[pl.BlockSpec((1, tq, hw), lambda h, b, s, qi, kj, ty, fin: (b, qi[s], h)),
                  pl.BlockSpec((1, tk, hw), lambda h, b, s, qi, kj, ty, fin: (b, kj[s], h)),
                  pl.BlockSpec((1, hw, tk), lambda h, b, s, qi, kj, ty, fin: (b, h, kj[s])),
                  pl.BlockSpec((hps, len(offsets), tk, tq), lambda h, b, s, *_: (h, 0, 0, 0)),
                  pl.BlockSpec(lam.shape, lambda h, b, s, *_: (0, 0)),
                  pl.BlockSpec((hw, 1), lambda h, b, s, *_: (h, 0))],
        out_specs=pl.BlockSpec((1, tq, hw), lambda h, b, s, qi, kj, ty, fin: (b, qi[s], h)),
        scratch_shapes=[pltpu.VMEM((hps, 2 * tq, LANES), BF16), pltpu.VMEM((hps, 1, 2 * tq), F32),
                        pltpu.VMEM((hps, B_DV + ONES_ROWS, 2 * tq), F32)],
    )
    return pl.pallas_call(
        kern,
        grid_spec=grid_spec,
        out_shape=jax.ShapeDtypeStruct((B, T, W), BF16),
        compiler_params=_params("parallel", "parallel", "arbitrary"),
        name="attn_prompt",
    )(qi, kj, ty, fin, q16, k16, vt16, bias, lam, ghn_col)


PAGES_PER_STEP = 16


def _attn_paged_kernel(pt_ref, q_ref, *refs, heads, t_new, page, n_steps, lam_init):
    pps = PAGES_PER_STEP
    kc_refs, vc_refs = refs[:pps], refs[pps:2 * pps]
    kn_ref, vn_ref, blast_ref, bnew_ref, lam_ref, ghn_ref, o_ref, qm_sc, m_sc, l_sc, acc_sc = refs[2 * pps:]
    j = pl.program_id(1)
    R = SUBLANES
    hsl = lambda h: slice(h * R, (h + 1) * R)
    lsl = lambda h: slice(h * LANES, (h + 1) * LANES)

    @pl.when(j == 0)
    def _():
        q = q_ref[0]
        row = lax.broadcasted_iota(jnp.int32, (R, LANES), 0)
        lane = lax.broadcasted_iota(jnp.int32, (R, LANES), 1)
        keep = (row < t_new) == (lane < B_DK)
        for h in range(heads):
            qm_sc[hsl(h), :] = jnp.where(keep, q[:, lsl(h)], 0.0)
        m_sc[...] = jnp.full_like(m_sc, -jnp.inf)
        l_sc[...] = jnp.zeros_like(l_sc)
        acc_sc[...] = jnp.zeros_like(acc_sc)

    qm = qm_sc[...].astype(BF16)

    def update(s, pv_of):
        m_old = m_sc[...]
        m_new = jnp.maximum(m_old, jnp.max(s, axis=1, keepdims=True))
        alpha = jnp.exp(m_old - m_new)
        p = jnp.exp(s - m_new)
        l_sc[...] = alpha * l_sc[...] + jnp.sum(p, axis=1, keepdims=True)
        pb = p.astype(BF16)
        acc_sc[...] = alpha * acc_sc[...] + jnp.concatenate([pv_of(h, pb[hsl(h)]) for h in range(heads)], axis=0)
        m_sc[...] = m_new

    s = jnp.concatenate(
        [jnp.concatenate([jnp.dot(qm[hsl(h)], kc[0, lsl(h), :].astype(BF16), preferred_element_type=F32)
                          for h in range(heads)], axis=0) for kc in kc_refs], axis=1)
    s = s + jnp.where(j == n_steps - 1, blast_ref[...], 0.0)

    def pv_cached(h, ph):
        parts = [jnp.dot(ph[:, u * page:(u + 1) * page], vc[0, pl.ds(h, page, stride=heads), :].astype(BF16),
                         preferred_element_type=F32) for u, vc in enumerate(vc_refs)]
        return functools.reduce(lambda a, b: a + b, parts)

    update(s, pv_cached)

    @pl.when(j == n_steps - 1)
    def _():
        kn = kn_ref[0].astype(BF16)
        vn = vn_ref[0].astype(BF16)
        s_new = jnp.concatenate([_nt_dot(qm[hsl(h)], kn[:, lsl(h)]) for h in range(heads)], axis=0) + bnew_ref[...]
        update(s_new, lambda h, ph: jnp.dot(ph, vn[:, lsl(h)], preferred_element_type=F32))
        lam = _lambda(lam_ref, lam_init)
        full = acc_sc[...] / l_sc[...]
        for h in range(heads):
            fh = full[hsl(h)]
            o = fh - lam * pltpu.roll(fh, R - t_new, 0)
            on = o * lax.rsqrt(jnp.mean(o * o, axis=1, keepdims=True) + EPS)
            o_ref[0, :, h * LANES:(h + 1) * LANES] = on * ghn_ref[:, h * LANES:(h + 1) * LANES] * (1.0 - lam_init)


def _attn_paged(q, cache_k, cache_v, page_table, k_new, v_new, tbl, lam, ghn, lam_init):
    B, t_new, W = q.shape
    H = W // LANES
    n_pool, page = cache_k.shape[:2]
    n_pages = page_table.shape[1]
    past = n_pages * page
    R = SUBLANES
    pps = PAGES_PER_STEP
    assert 2 * t_new == R and page >= RPB_MAX_DIST and n_pages % pps == 0
    n_steps = n_pages // pps
    pad = lambda a: jnp.concatenate([a, jnp.zeros((B, R - t_new, W), a.dtype)], axis=1)
    q8 = jnp.concatenate([q, q], axis=1)
    t = np.arange(R)[:, None] % t_new
    d_last = past + t - ((n_pages - 1) * page + np.arange(page)[None, :])
    c = np.arange(R)[None, :]
    d_new = np.where(c < t_new, t - c, -1)
    flat = lambda b: b.reshape(H * R, b.shape[-1])
    bias_last = jnp.pad(flat(_bias_of_distance(tbl, d_last)), ((0, 0), ((pps - 1) * page, 0)))
    bias_new = flat(_bias_of_distance(tbl, d_new))
    kern = functools.partial(_attn_paged_kernel, heads=H, t_new=t_new, page=page, n_steps=n_steps, lam_init=lam_init)
    page_spec = lambda rows, width, u: pl.BlockSpec(
        (1, rows, width), lambda b, j, pt: (pt[b * n_pages + j * pps + u], 0, 0))
    per_b = pl.BlockSpec((1, R, W), lambda b, j, pt: (b, 0, 0))
    full = lambda a: pl.BlockSpec(a.shape, lambda b, j, pt: (0,) * a.ndim)
    grid_spec = pltpu.PrefetchScalarGridSpec(
        num_scalar_prefetch=1,
        grid=(B, n_steps),
        in_specs=[per_b] + [page_spec(W, page, u) for u in range(pps)]
                 + [page_spec(page * H, B_DV, u) for u in range(pps)]
                 + [per_b, per_b, full(bias_last), full(bias_new), full(lam), full(ghn)],
        out_specs=per_b,
        scratch_shapes=[pltpu.VMEM((H * R, LANES), F32), pltpu.VMEM((H * R, 1), F32),
                        pltpu.VMEM((H * R, 1), F32), pltpu.VMEM((H * R, B_DV), F32)],
    )
    ck = jnp.transpose(cache_k, (0, 2, 3, 4, 1)).reshape(n_pool, W, page)
    cv = cache_v.reshape(n_pool, page * H, B_DV)
    out = pl.pallas_call(
        kern,
        grid_spec=grid_spec,
        out_shape=jax.ShapeDtypeStruct((B, R, W), F32),
        compiler_params=_params("parallel", "arbitrary"),
        name="attn_paged",
    )(page_table.reshape(-1), q8, *([ck] * pps), *([cv] * pps), pad(k_new), pad(v_new),
      bias_last, bias_new, lam, ghn)
    return out[:, :t_new]


def _trunk(x, mods, mods_kv, state, past, wts):
    B, T, D = x.shape
    N = B * T
    tm = ROW_TILE
    tb = min(MOE_BLOCK, N)
    tr = min(ROUTER_TILE, N)
    mod = lambda m: _Mod(m, T, tm)
    split3 = lambda m: (mod(m[:, :D]), mod(m[:, D:2 * D]), mod(m[:, 2 * D:]))
    rmod = lambda m: _Mod(m, T, tr)
    x2 = x.reshape(N, D)

    heads = A_HEADS
    inner = wts["w_out16"].shape[0]
    dh = inner // heads
    gate = rmod(mods[0][:, 2 * D:])
    if state is None:
        t_rows, x_in = T, x2
    else:
        assert T <= SUBLANES
        t_rows = SUBLANES
        x_in = jnp.concatenate([x, jnp.zeros((B, t_rows - T, D), x.dtype)], axis=1).reshape(B * t_rows, D)
    t_in = min(INPROJ_TILE, B * t_rows)
    proj, gates = _inproj(x_in, _Mod(mods[0][:, :D], t_rows, t_in), _Mod(mods[0][:, D:2 * D], t_rows, t_in),
                          wts["g_norm"][0, 0][None], wts["w_in16"], wts["w_gate"], t_in)
    conv_new = proj.reshape(B, t_rows, 4 * inner)[:, T - (A_CONV - 1):T, :2 * inner]
    proj = proj.reshape(B, t_rows, 4 * inner)
    gates = gates.reshape(B, t_rows, LANES)
    if state is None:
        L, t_valid = math.gcd(T, MLSTM_CHUNK), None
        conv_init = jnp.zeros((B, SUBLANES, 2 * inner), F32)
        c0 = jnp.zeros((B, heads, dh, dh), F32)
        n0 = jnp.zeros((B, heads, 1, dh), F32)
        m0 = jnp.zeros((B, heads, 1, 1), F32)
    else:
        conv_st, c_st, n_st, m_st = state
        L, t_valid = SUBLANES, T
        conv_init = jnp.concatenate([jnp.zeros((B, SUBLANES - (A_CONV - 1), 2 * inner), F32), conv_st], axis=1)
        c0, n0, m0 = c_st, n_st[:, :, None, :], m_st[:, :, None, None]
    gates_t = jnp.swapaxes(gates[:, :, :SUBLANES], 1, 2)
    hs, c1, n1, m1 = _mlstm(proj, gates, gates_t, wts["bg"], wts["bgt"], conv_init, c0, n0, m0,
                            wts["w_conv"], wts["b_conv"], wts["g_hn_a"], L, t_valid)
    hs = hs[:, :T].reshape(N, inner)
    new_state = (conv_new, c1, n1[:, :, 0, :], m1[:, :, 0, 0])

    x2, h16, rg = _proj_router(hs, wts["w_out16"], x2, gate, rmod(mods[1][:, :D]), rmod(mods[1][:, D:2 * D]),
                               wts["g_norm"][0, 1][None], wts["w_router"], wts["b_router"], tr)
    x2 = _moe(h16, rg, x2, _Mod(mods[1][:, 2 * D:], T, tb), wts["wg16"], wts["wu16"], wts["wd16"], 0, tb)

    shift_kv, scale_kv = mod(mods_kv[:, :D]), mod(mods_kv[:, D:])
    shift, scale, gate = split3(mods[2])
    kvq = _kvq(x2, shift_kv, scale_kv, wts["g_kv"], shift, scale, wts["g_norm"][1, 0][None],
               wts["w_kv16"], wts["w_q16"], wts["gmat"], wts["gmat_t"], wts["g_kn"], wts["g_qn"], tm, T)
    v32 = kvq[1]
    W = v32.shape[1]
    H = W // LANES
    lam_init = 0.8 - 0.6 * math.exp(-0.3 * 1)
    if past is None:
        kt32, _, q, k16, vt16 = kvq
        o = _attn_prompt(q.reshape(B, T, W), k16.reshape(B, T, W), vt16, wts["rpb_tbl"], wts["lam"],
                         wts["g_hn_b"].reshape(W, 1), lam_init, math.gcd(T, ATTN_TQ), math.gcd(T, ATTN_TK))
        o = o.reshape(N, W)
        k_out = jnp.transpose(kt32.reshape(B, H, 2, B_DK, T), (0, 4, 1, 2, 3))
    else:
        k32, _, q = kvq
        k_out = k32.reshape(B, T, H, 2, B_DK)
        cache_k, cache_v, page_table = past
        o = _attn_paged(q.reshape(B, T, W), cache_k, cache_v, page_table, k32.reshape(B, T, W),
                        v32.reshape(B, T, W), wts["rpb_tbl"], wts["lam"], wts["g_hn_b"], lam_init)
        o = o.reshape(N, W).astype(BF16)

    x2, h16, rg = _proj_router(o, wts["w_o16"], x2, rmod(mods[2][:, 2 * D:]), rmod(mods[3][:, :D]),
                               rmod(mods[3][:, D:2 * D]), wts["g_norm"][1, 1][None], wts["w_router"],
                               wts["b_router"], tr)
    x2 = _moe(h16, rg, x2, _Mod(mods[3][:, 2 * D:], T, tb), wts["wg16"], wts["wu16"], wts["wd16"], 1, tb)

    return x2.reshape(B, T, D), new_state, k_out, v32.reshape(B, T, H, B_DV)


def kernel(x_prompt, x_sample, c_prompt, c_sample, state_conv, state_C, state_n, state_m, cache_k, cache_v, page_table, w_ada, b_ada, g_norm, w_in_a, b_gate_a, w_conv_a, b_conv_a, g_hn_a, w_out_a, g_kv, w_ada_kv, b_ada_kv, w_kv, g_kn, w_q_b, g_qn_b, lam_b, g_hn_b, w_o_b, rpb, w_router, b_router, w_gate_e, w_up_e, w_down_e):
    Bp, Tp, D = x_prompt.shape
    Bs = x_sample.shape[0]
    inner = w_out_a.shape[1]
    heads_b = g_hn_b.shape[1]
    W = heads_b * B_DV

    n_c = Bp + Bs
    c_all = jnp.concatenate([c_prompt, c_sample, jnp.zeros((-n_c % SUBLANES, D), F32)], axis=0)
    mods = _ada(c_all, w_ada.reshape(-1, D, 3 * D), b_ada.reshape(-1, 1, 3 * D))
    mods_kv = _ada(c_all, w_ada_kv[None], b_ada_kv[None, None])[0]

    n_gate = 2 * A_HEADS
    w_in_t = jnp.swapaxes(w_in_a[0], 0, 1)
    group_of_lane = np.arange(W) // B_DK
    gmat = jnp.asarray((group_of_lane[:, None] == np.arange(LANES)[None, :]).astype(np.float32)).astype(BF16)
    wts = {
        "g_norm": g_norm,
        "w_in16": w_in_t[:4 * inner].astype(BF16),
        "w_gate": jnp.pad(w_in_t[4 * inner:], ((0, LANES - n_gate), (0, 0))),
        "bg": jnp.pad(b_gate_a[0], (0, LANES - n_gate))[None, :],
        "bgt": b_gate_a[0][:, None],
        "w_conv": w_conv_a[0], "b_conv": b_conv_a[0][None, :],
        "g_hn_a": g_hn_a[0].reshape(1, inner),
        "w_out16": w_out_a[0].astype(BF16),
        "g_kv": g_kv[None, :],
        "w_kv16": w_kv.astype(BF16), "w_q16": w_q_b[0].astype(BF16),
        "gmat": gmat, "gmat_t": gmat.T,
        "g_kn": jnp.tile(g_kn.reshape(-1), heads_b)[None, :],
        "g_qn": jnp.tile(g_qn_b[0].reshape(-1), heads_b)[None, :],
        "lam": lam_b[0], "g_hn_b": g_hn_b[0].reshape(1, W),
        "w_o16": w_o_b[0].astype(BF16),
        "rpb_tbl": _bias_table(rpb),
        "w_router": jnp.pad(w_router, ((0, 0), (0, LANES - N_EXPERTS))),
        "b_router": b_router[:, None],
        "wg16": w_gate_e.astype(BF16), "wu16": w_up_e.astype(BF16), "wd16": w_down_e.astype(BF16),
    }

    y_p, st_p, k_p, v_p = _trunk(x_prompt, mods[:, :Bp], mods_kv[:Bp], None, None, wts)
    y_s, st_s, k_s, v_s = _trunk(x_sample, mods[:, Bp:n_c], mods_kv[Bp:n_c],
                                 (state_conv[0], state_C[0], state_n[0], state_m[0]),
                                 (cache_k, cache_v, page_table), wts)
    stack = lambda st: tuple(a[None] for a in st)
    return (y_p, y_s) + stack(st_p) + (k_p, v_p) + stack(st_s) + (k_s, v_s)
```
